```python
import jax
import jax.numpy as jnp
from jax import lax
import numpy as np

D_MODEL = 1024
BATCH = 1
SEQ = 16384
DEPTH = 2

HEAD_DIM = 64
ROPE_THETA = 10000.0
NORM_EPS = 1e-6
Q_BLOCK = 128
NEG_INF = -1e30
FORCE_SELECT = 1e9

NSA_HEADS = 8
NSA_KV_GROUPS = 2
NSA_HEADS_PER_GROUP = NSA_HEADS // NSA_KV_GROUPS
NSA_CMP_LEN = 32
NSA_CMP_STRIDE = 16
NSA_SEL_LEN = 64
NSA_SEL_TOPK = 16
NSA_WINDOW = 512
NSA_BRANCHES = 3

MOBA_HEADS = 8
MOBA_BLOCK = 256
MOBA_TOPK = 3

MEM_LEN = 256
MEM_HEADS = 4

RET_HEADS = 4
RET_QK_DIM = D_MODEL // RET_HEADS
RET_V_DIM = 2 * D_MODEL // RET_HEADS
RET_CHUNK = 128

NSA_Q_W = NSA_HEADS * HEAD_DIM
NSA_KV_W = NSA_KV_GROUPS * HEAD_DIM
MOBA_W = MOBA_HEADS * HEAD_DIM
MEM_W = MEM_HEADS * HEAD_DIM
RET_QK_W = RET_HEADS * RET_QK_DIM
RET_V_W = RET_HEADS * RET_V_DIM

EVEN_SPLITS = (NSA_Q_W, NSA_KV_W, NSA_KV_W, NSA_KV_W, NSA_KV_W, NSA_KV_W, NSA_KV_W,
               NSA_BRANCHES * NSA_HEADS, NSA_Q_W,
               MOBA_W, MOBA_W, MOBA_W, MOBA_W,
               MEM_W, MEM_W)
ODD_SPLITS = (RET_QK_W, RET_QK_W, RET_V_W, RET_V_W, MEM_W, MEM_W)
EVEN_IN_W = sum(EVEN_SPLITS)
EVEN_OUT_W = NSA_Q_W + MOBA_W + MEM_W
ODD_IN_W = sum(ODD_SPLITS)
ODD_OUT_W = RET_V_W + MEM_W

kernel_name = 'hybrid_nsa_moba_retention_memory'


def _split(h, sizes):
    cuts = np.cumsum(np.array(sizes))[:-1].tolist()
    return jnp.split(h, cuts, axis=-1)


def rms_norm(x, g):
    xf = x.astype(jnp.float32)
    y = xf * lax.rsqrt(jnp.mean(xf * xf, axis=-1, keepdims=True) + NORM_EPS)
    return (y * g.astype(jnp.float32)).astype(x.dtype)


def rotary_tables(positions, inv_freq):
    ang = positions.astype(jnp.float32)[..., None] * inv_freq
    return jnp.cos(ang), jnp.sin(ang)


def apply_rotary(x, cos, sin):
    x1, x2 = jnp.split(x, 2, axis=-1)
    c = cos[:, :, None, :].astype(x.dtype)
    s = sin[:, :, None, :].astype(x.dtype)
    return jnp.concatenate([x1 * c - x2 * s, x1 * s + x2 * c], axis=-1)


def masked_softmax(s, mask):
    s = jnp.where(mask, s.astype(jnp.float32), NEG_INF)
    m = jnp.max(s, axis=-1, keepdims=True)
    p = jnp.where(mask, jnp.exp(s - m), 0.0)
    return p / jnp.maximum(jnp.sum(p, axis=-1, keepdims=True), 1e-30)


def nsa_compress(a, pe, w1, w2):
    T = a.shape[1]
    n_cmp = (T - NSA_CMP_LEN) // NSA_CMP_STRIDE + 1
    idx = np.arange(n_cmp)[:, None] * NSA_CMP_STRIDE + np.arange(NSA_CMP_LEN)[None, :]
    blocks = a[:, idx] + pe[None, None, :, None, :].astype(a.dtype)
    hid = jax.nn.silu(jnp.einsum('bnlgd,lde->bnge', blocks, w1))
    return jnp.einsum('bnge,ef->bngf', hid, w2)


def nsa_attention(q, kc, vc, ks, vs, kw, vw, gates):
    Bsz, T = q.shape[0], q.shape[1]
    G, I, D = NSA_KV_GROUPS, NSA_HEADS_PER_GROUP, HEAD_DIM
    n_cmp = kc.shape[1]
    n_sel = T // NSA_SEL_LEN
    k_sel = min(NSA_SEL_TOPK, n_sel)
    scale = HEAD_DIM ** -0.5
    cmp_start = jnp.arange(n_cmp) * NSA_CMP_STRIDE
    cmp_end = cmp_start + NSA_CMP_LEN - 1
    sel_start = jnp.arange(n_sel) * NSA_SEL_LEN
    overlap = ((cmp_start[:, None] < sel_start[None, :] + NSA_SEL_LEN) &
               (cmp_start[:, None] + NSA_CMP_LEN > sel_start[None, :])).astype(jnp.float32)
    ks_blk = ks.reshape(Bsz, n_sel, NSA_SEL_LEN, G, D).transpose(0, 3, 1, 2, 4)
    vs_blk = vs.reshape(Bsz, n_sel, NSA_SEL_LEN, G, D).transpose(0, 3, 1, 2, 4)
    kw_pad = jnp.pad(kw, ((0, 0), (NSA_WINDOW, 0), (0, 0), (0, 0)))
    vw_pad = jnp.pad(vw, ((0, 0), (NSA_WINDOW, 0), (0, 0), (0, 0)))
    b_idx = jnp.arange(Bsz)[:, None, None, None]
    g_idx = jnp.arange(G)[None, :, None, None]
    sel_blocks = jnp.arange(n_sel)

    def block(bi):
        t0 = bi * Q_BLOCK
        t = t0 + jnp.arange(Q_BLOCK)
        qb = lax.dynamic_slice_in_dim(q, t0, Q_BLOCK, axis=1).reshape(Bsz, Q_BLOCK, G, I, D)
        gb = lax.dynamic_slice_in_dim(gates, t0, Q_BLOCK, axis=1).reshape(Bsz, Q_BLOCK, NSA_BRANCHES, G, I)
        s_c = jnp.einsum('bqgid,bngd->bgiqn', qb, kc) * scale
        p_c = masked_softmax(s_c, cmp_end[None, :] <= t[:, None])
        o_c = jnp.einsum('bgiqn,bngd->bqgid', p_c.astype(vc.dtype), vc)
        imp = jnp.einsum('bgiqn,ns->bgqs', p_c, overlap)
        cur = t // NSA_SEL_LEN
        forced = ((sel_blocks[None, :] == 0) | (sel_blocks[None, :] == cur[:, None]) |
                  (sel_blocks[None, :] == cur[:, None] - 1))
        imp = jnp.where(forced, FORCE_SELECT, imp)
        imp = jnp.where(sel_blocks[None, :] <= cur[:, None], imp, NEG_INF)
        _, sel = lax.top_k(imp, k_sel)
        k_g = ks_blk[b_idx, g_idx, sel]
        v_g = vs_blk[b_idx, g_idx, sel]
        kpos = sel[..., None] * NSA_SEL_LEN + jnp.arange(NSA_SEL_LEN)
        mask_s = (kpos <= t[None, None, :, None, None]).reshape(Bsz, G, 1, Q_BLOCK, k_sel * NSA_SEL_LEN)
        s_s = jnp.einsum('bqgid,bgqkld->bgiqkl', qb, k_g).reshape(Bsz, G, I, Q_BLOCK, k_sel * NSA_SEL_LEN) * scale
        p_s = masked_softmax(s_s, mask_s).reshape(Bsz, G, I, Q_BLOCK, k_sel, NSA_SEL_LEN)
        o_s = jnp.einsum('bgiqkl,bgqkld->bqgid', p_s.astype(v_g.dtype), v_g)
        kwb = lax.dynamic_slice_in_dim(kw_pad, t0, Q_BLOCK + NSA_WINDOW, axis=1)
        vwb = lax.dynamic_slice_in_dim(vw_pad, t0, Q_BLOCK + NSA_WINDOW, axis=1)
        wpos = t0 - NSA_WINDOW + jnp.arange(Q_BLOCK + NSA_WINDOW)
        mask_w = ((wpos[None, :] <= t[:, None]) & (wpos[None, :] > t[:, None] - NSA_WINDOW) &
                  (wpos[None, :] >= 0))
        s_w = jnp.einsum('bqgid,bjgd->bgiqj', qb, kwb) * scale
        p_w = masked_softmax(s_w, mask_w)
        o_w = jnp.einsum('bgiqj,bjgd->bqgid', p_w.astype(vwb.dtype), vwb)
        o = (gb[:, :, 0, :, :, None] * o_c + gb[:, :, 1, :, :, None] * o_s +
             gb[:, :, 2, :, :, None] * o_w)
        return o.reshape(Bsz, Q_BLOCK, NSA_Q_W)

    out = lax.map(block, jnp.arange(T // Q_BLOCK))
    return out.transpose(1, 0, 2, 3).reshape(Bsz, T, NSA_Q_W)


def moba_attention(q, k, v):
    Bsz, T, H, D = q.shape
    n_blk = -(-T // MOBA_BLOCK)
    pad = n_blk * MOBA_BLOCK - T
    k_pad = jnp.pad(k, ((0, 0), (0, pad), (0, 0), (0, 0)))
    v_pad = jnp.pad(v, ((0, 0), (0, pad), (0, 0), (0, 0)))
    k_blk = k_pad.reshape(Bsz, n_blk, MOBA_BLOCK, H, D)
    k_mean = jnp.mean(k_blk.astype(jnp.float32), axis=2).astype(k.dtype)
    k_blk = k_blk.transpose(0, 3, 1, 2, 4)
    v_blk = v_pad.reshape(Bsz, n_blk, MOBA_BLOCK, H, D).transpose(0, 3, 1, 2, 4)
    top = max(1, min(MOBA_TOPK, n_blk - 1))
    scale = HEAD_DIM ** -0.5
    b_idx = jnp.arange(Bsz)[:, None, None, None]
    h_idx = jnp.arange(H)[None, :, None, None]
    blocks = jnp.arange(n_blk)

    def block(bi):
        t0 = bi * Q_BLOCK
        t = t0 + jnp.arange(Q_BLOCK)
        cur = t0 // MOBA_BLOCK
        qb = lax.dynamic_slice_in_dim(q, t0, Q_BLOCK, axis=1)
        gate = jnp.einsum('bqhd,bnhd->bhqn', qb, k_mean).astype(jnp.float32)
        gate = jnp.where(blocks < cur, gate, NEG_INF)
        _, sel = lax.top_k(gate, top)
        k_g = k_blk[b_idx, h_idx, sel]
        v_g = v_blk[b_idx, h_idx, sel]
        s_sel = jnp.einsum('bqhd,bhqksd->bhqks', qb, k_g).reshape(Bsz, H, Q_BLOCK, top * MOBA_BLOCK)
        m_sel = jnp.broadcast_to((sel < cur)[..., None],
                                 (Bsz, H, Q_BLOCK, top, MOBA_BLOCK)).reshape(Bsz, H, Q_BLOCK, top * MOBA_BLOCK)
        k_own = lax.dynamic_slice_in_dim(k_pad, cur * MOBA_BLOCK, MOBA_BLOCK, axis=1)
        v_own = lax.dynamic_slice_in_dim(v_pad, cur * MOBA_BLOCK, MOBA_BLOCK, axis=1)
        s_own = jnp.einsum('bqhd,bshd->bhqs', qb, k_own)
        own_pos = cur * MOBA_BLOCK + jnp.arange(MOBA_BLOCK)
        m_own = jnp.broadcast_to(own_pos[None, :] <= t[:, None], (Bsz, H, Q_BLOCK, MOBA_BLOCK))
        p = masked_softmax(jnp.concatenate([s_sel, s_own], axis=-1) * scale,
                           jnp.concatenate([m_sel, m_own], axis=-1)).astype(v.dtype)
        p_sel = p[..., :top * MOBA_BLOCK].reshape(Bsz, H, Q_BLOCK, top, MOBA_BLOCK)
        p_own = p[..., top * MOBA_BLOCK:]
        o = (jnp.einsum('bhqks,bhqksd->bqhd', p_sel, v_g) +
             jnp.einsum('bhqs,bshd->bqhd', p_own, v_own))
        return o.reshape(Bsz, Q_BLOCK, H * D)

    out = lax.map(block, jnp.arange(T // Q_BLOCK))
    return out.transpose(1, 0, 2, 3).reshape(Bsz, T, H * D)


def memory_attention(q, mem_n, w_mem_kv):
    Bsz, T = q.shape[0], q.shape[1]
    k, v = jnp.split(mem_n @ w_mem_kv, 2, axis=-1)
    k = k.reshape(Bsz, -1, MEM_HEADS, HEAD_DIM)
    v = v.reshape(Bsz, -1, MEM_HEADS, HEAD_DIM)
    s = jnp.einsum('bthd,bmhd->bhtm', q, k).astype(jnp.float32) * (HEAD_DIM ** -0.5)
    p = jax.nn.softmax(s, axis=-1).astype(v.dtype)
    return jnp.einsum('bhtm,bmhd->bthd', p, v).reshape(Bsz, T, MEM_W)


def retention(q, k, v):
    Bsz, T, H, _ = q.shape
    C = RET_CHUNK
    NC = T // C
    log_g = jnp.log(1.0 - 2.0 ** (-5.0 - jnp.arange(H, dtype=jnp.float32)))
    i = jnp.arange(C, dtype=jnp.float32)
    diff = i[:, None] - i[None, :]
    decay = jnp.where(diff >= 0, jnp.exp(jnp.maximum(diff, 0.0)[None] * log_g[:, None, None]), 0.0)
    q_decay = jnp.exp((i + 1.0)[None, :] * log_g[:, None])
    k_decay = jnp.exp((C - 1.0 - i)[None, :] * log_g[:, None])
    chunk_decay = jnp.exp(C * log_g)

    def to_chunks(a):
        return a.astype(jnp.float32).reshape(Bsz, NC, C, H, -1).transpose(1, 0, 3, 2, 4)

    def step(state, inp):
        qc, kc, vc = inp
        inner = jnp.einsum('bhid,bhjd->bhij', qc, kc) * decay
        o = (jnp.einsum('bhij,bhjv->bhiv', inner, vc) +
             jnp.einsum('bhid,bhdv->bhiv', qc, state) * q_decay[None, :, :, None])
        state = (state * chunk_decay[None, :, None, None] +
                 jnp.einsum('bhjd,bhjv->bhdv', kc * k_decay[None, :, :, None], vc))
        return state, o

    state0 = jnp.zeros((Bsz, H, q.shape[-1], v.shape[-1]), jnp.float32)
    _, o = lax.scan(step, state0, (to_chunks(q), to_chunks(k), to_chunks(v)))
    return o.transpose(1, 0, 3, 2, 4).reshape(Bsz, T, H, v.shape[-1])


def head_group_norm(o):
    mu = jnp.mean(o, axis=-1, keepdims=True)
    var = jnp.mean(jnp.square(o - mu), axis=-1, keepdims=True)
    return (o - mu) * lax.rsqrt(var + NORM_EPS)


def even_layer(x, mem_n, cos, sin, norm_g, w_in, gate_b, pe_k, w1_k, w2_k, pe_v, w1_v, w2_v, w_mem_kv, w_out):
    Bsz, T, _ = x.shape
    h = rms_norm(x, norm_g)
    (nq, nkc, nvc, nks, nvs, nkw, nvw, ng, nz, mq, mk, mv, mz, eq, ez) = _split(h @ w_in, EVEN_SPLITS)

    def heads(a):
        return a.reshape(Bsz, T, -1, HEAD_DIM)

    def rot(a):
        return apply_rotary(heads(a), cos, sin)

    kc = nsa_compress(rot(nkc), pe_k, w1_k, w2_k)
    vc = nsa_compress(heads(nvc), pe_v, w1_v, w2_v)
    gates = jax.nn.sigmoid(ng + gate_b).reshape(Bsz, T, NSA_BRANCHES, NSA_HEADS)
    a_out = nsa_attention(rot(nq), kc, vc, rot(nks), heads(nvs), rot(nkw), heads(nvw), gates)
    b_out = moba_attention(rot(mq), rot(mk), heads(mv))
    m_out = memory_attention(heads(eq), mem_n, w_mem_kv)
    y = jnp.concatenate([a_out * jax.nn.silu(nz), b_out * jax.nn.silu(mz),
                         m_out * jax.nn.silu(ez)], axis=-1) @ w_out
    return x + y


def odd_layer(x, mem_n, rcos, rsin, norm_g, w_in, w_mem_kv, w_out):
    Bsz, T, _ = x.shape
    h = rms_norm(x, norm_g)
    rq, rk, rv, rz, eq, ez = _split(h @ w_in, ODD_SPLITS)
    q = apply_rotary(rq.reshape(Bsz, T, RET_HEADS, RET_QK_DIM), rcos, rsin)
    k = apply_rotary(rk.reshape(Bsz, T, RET_HEADS, RET_QK_DIM), rcos, rsin) * (RET_QK_DIM ** -0.5)
    v = rv.reshape(Bsz, T, RET_HEADS, RET_V_DIM)
    r_out = head_group_norm(retention(q, k, v)).reshape(Bsz, T, RET_V_W).astype(x.dtype)
    m_out = memory_attention(eq.reshape(Bsz, T, MEM_HEADS, HEAD_DIM), mem_n, w_mem_kv)
    y = jnp.concatenate([r_out * jax.nn.silu(rz), m_out * jax.nn.silu(ez)], axis=-1) @ w_out
    return x + y


def setup_inputs(seed: int = 0) -> dict:
    key = jax.random.key(seed)
    ks = jax.random.split(key, 24)

    def nrm(k, shape, scale):
        return jax.random.normal(k, shape, jnp.float32) * scale

    def gain(k):
        return 1.0 + nrm(k, (D_MODEL,), 0.02)

    x = nrm(ks[0], (BATCH, SEQ, D_MODEL), 1.0)
    mem = nrm(ks[1], (BATCH, MEM_LEN, D_MODEL), 1.0)
    positions = jnp.broadcast_to(jnp.arange(SEQ, dtype=jnp.int32)[None, :], (BATCH, SEQ))
    cmp_in = NSA_CMP_LEN * HEAD_DIM
    return {
        'x': x,
        'mem': mem,
        'positions': positions,
        'l0_norm_g': gain(ks[2]),
        'l0_w_in': nrm(ks[3], (D_MODEL, EVEN_IN_W), D_MODEL ** -0.5),
        'l0_nsa_gate_b': nrm(ks[4], (NSA_BRANCHES * NSA_HEADS,), 0.1),
        'l0_cmp_pe_k': nrm(ks[5], (NSA_CMP_LEN, HEAD_DIM), 0.1),
        'l0_cmp_w1_k': nrm(ks[6], (NSA_CMP_LEN, HEAD_DIM, HEAD_DIM), cmp_in ** -0.5),
        'l0_cmp_w2_k': nrm(ks[7], (HEAD_DIM, HEAD_DIM), HEAD_DIM ** -0.5),
        'l0_cmp_pe_v': nrm(ks[8], (NSA_CMP_LEN, HEAD_DIM), 0.1),
        'l0_cmp_w1_v': nrm(ks[9], (NSA_CMP_LEN, HEAD_DIM, HEAD_DIM), cmp_in ** -0.5),
        'l0_cmp_w2_v': nrm(ks[10], (HEAD_DIM, HEAD_DIM), HEAD_DIM ** -0.5),
        'l0_w_mem_kv': nrm(ks[11], (D_MODEL, 2 * MEM_W), D_MODEL ** -0.5),
        'l0_w_out': nrm(ks[12], (EVEN_OUT_W, D_MODEL), EVEN_OUT_W ** -0.5),
        'l1_norm_g': gain(ks[13]),
        'l1_w_in': nrm(ks[14], (D_MODEL, ODD_IN_W), D_MODEL ** -0.5),
        'l1_w_mem_kv': nrm(ks[15], (D_MODEL, 2 * MEM_W), D_MODEL ** -0.5),
        'l1_w_out': nrm(ks[16], (ODD_OUT_W, D_MODEL), ODD_OUT_W ** -0.5),
        'mem_norm_g': gain(ks[17]),
        'final_norm_g': gain(ks[18]),
    }


def reference(x, mem, positions, l0_norm_g, l0_w_in, l0_nsa_gate_b, l0_cmp_pe_k, l0_cmp_w1_k, l0_cmp_w2_k,
              l0_cmp_pe_v, l0_cmp_w1_v, l0_cmp_w2_v, l0_w_mem_kv, l0_w_out,
              l1_norm_g, l1_w_in, l1_w_mem_kv, l1_w_out, mem_norm_g, final_norm_g):
    attn_inv = 1.0 / (ROPE_THETA ** (jnp.arange(0, HEAD_DIM, 2, dtype=jnp.float32) / HEAD_DIM))
    ret_inv = 1.0 / (ROPE_THETA ** jnp.linspace(0.0, 1.0, RET_QK_DIM // 2, dtype=jnp.float32))
    cos, sin = rotary_tables(positions, attn_inv)
    rcos, rsin = rotary_tables(positions, ret_inv)
    mem_n = rms_norm(mem, mem_norm_g)
    layer_params = (
        (l0_norm_g, l0_w_in, l0_nsa_gate_b, l0_cmp_pe_k, l0_cmp_w1_k, l0_cmp_w2_k,
         l0_cmp_pe_v, l0_cmp_w1_v, l0_cmp_w2_v, l0_w_mem_kv, l0_w_out),
        (l1_norm_g, l1_w_in, l1_w_mem_kv, l1_w_out),
    )
    for layer in range(DEPTH):
        if layer % 2 == 0:
            x = even_layer(x, mem_n, cos, sin, *layer_params[layer])
        else:
            x = odd_layer(x, mem_n, rcos, rsin, *layer_params[layer])
    return rms_norm(x, final_norm_g)
```

```python
import functools

import numpy as np
import jax
import jax.numpy as jnp
from jax import lax
from jax.experimental import pallas as pl
from jax.experimental.pallas import tpu as pltpu

F32 = jnp.float32
BF16 = jnp.bfloat16

D_MODEL = 1024
HEAD_DIM = 64
ROPE_THETA = 10000.0
NORM_EPS = 1e-6
NEG_INF = -1e30
FORCE_SELECT = 1e9
MASK_BIAS = -1e9

NSA_HEADS = 8
NSA_GROUP_HEADS = 4
NSA_CMP_LEN = 32
NSA_CMP_STRIDE = 16
NSA_SEL_LEN = 64
NSA_SEL_TOPK = 16
NSA_WINDOW = 512
MOBA_HEADS = 8
MOBA_BLOCK = 256
MOBA_TOPK = 3
MEM_LEN = 256
MEM_HEADS = 4
RET_HEADS = 4
RET_QK_DIM = 256
RET_V_DIM = 512
RET_CHUNK = 128

LANES = 128
PROJ_ROWS = 256
PROJ_COLS = 512
Q_TILE = 128
K_TILE = 512
VMEM_LIMIT = 56 * 1024 * 1024

NSA_HEAD_ORDER = (0, 4, 1, 5, 2, 6, 3, 7)


def _dot(a, b):
    return jnp.dot(a, b, preferred_element_type=F32)


def _dot_nt(a, b):
    return lax.dot_general(a, b, (((1,), (1,)), ((), ())), preferred_element_type=F32)


def _dot_tn(a, b):
    return lax.dot_general(a, b, (((0,), (0,)), ((), ())), preferred_element_type=F32)


def _silu(y):
    return y * (1.0 / (1.0 + jnp.exp(-y)))


def _rms(x, g):
    return x * lax.rsqrt(jnp.mean(x * x, axis=-1, keepdims=True) + NORM_EPS) * g


def _params(*sem):
    return pltpu.CompilerParams(dimension_semantics=sem, vmem_limit_bytes=VMEM_LIMIT)


def _proj_kernel(x_ref, g_ref, w_ref, c_ref, s_ref, b_ref, *out_refs, groups):
    h = _rms(x_ref[...], g_ref[...]).astype(BF16)
    rows = x_ref.shape[0]
    lane = lax.broadcasted_iota(jnp.int32, (rows, LANES), 1)
    low_half = (lane & 32) == 0
    col = 0
    oi = 0
    for width, kind, scale in groups:
        out_ref = out_refs[oi]
        oi += 1
        mean_ref = None
        if kind == "rot64_mean":
            mean_ref = out_refs[oi]
            oi += 1
        for c0 in range(0, width, PROJ_COLS):
            cw = min(PROJ_COLS, width - c0)
            y = _dot(h, w_ref[:, col + c0:col + c0 + cw])
            if kind in ("rot64", "rot64_mean"):
                c = c_ref[...]
                s = s_ref[...]
                pieces = []
                for j in range(cw // LANES):
                    p = y[:, j * LANES:(j + 1) * LANES]
                    partner = jnp.where(low_half, pltpu.roll(p, LANES - 32, 1), pltpu.roll(p, 32, 1))
                    pieces.append(p * c + partner * s)
                y = pieces[0] if len(pieces) == 1 else jnp.concatenate(pieces, axis=1)
            elif kind == "rot256":
                c = c_ref[...]
                s = s_ref[...]
                pieces = []
                for j in range(cw // RET_QK_DIM):
                    x1 = y[:, j * RET_QK_DIM:j * RET_QK_DIM + LANES]
                    x2 = y[:, j * RET_QK_DIM + LANES:(j + 1) * RET_QK_DIM]
                    pieces.append(x1 * c - x2 * s)
                    pieces.append(x1 * s + x2 * c)
                y = jnp.concatenate(pieces, axis=1)
            elif kind == "silu":
                y = _silu(y)
            elif kind == "gate":
                y = 1.0 / (1.0 + jnp.exp(-(y + b_ref[...])))
            if scale != 1.0:
                y = y * scale
            if mean_ref is not None:
                mean_ref[0, :, c0:c0 + cw] = jnp.mean(y, axis=0, keepdims=True)
            out_ref[:, c0:c0 + cw] = y.astype(out_ref.dtype)
        col += width


def _proj(x, g, w, cos_t, sin_t, bias, groups, out_dtypes, name):
    T = x.shape[0]
    n_tiles = T // PROJ_ROWS
    out_shapes = []
    out_specs = []
    for (width, kind, _), dt in zip(groups, out_dtypes):
        out_shapes.append(jax.ShapeDtypeStruct((T, width), dt))
        out_specs.append(pl.BlockSpec((PROJ_ROWS, width), lambda i: (i, 0)))
        if kind == "rot64_mean":
            out_shapes.append(jax.ShapeDtypeStruct((n_tiles, 1, width), F32))
            out_specs.append(pl.BlockSpec((1, 1, width), lambda i: (i, 0, 0)))
    return pl.pallas_call(
        functools.partial(_proj_kernel, groups=tuple(groups)),
        grid=(n_tiles,),
        in_specs=[
            pl.BlockSpec((PROJ_ROWS, D_MODEL), lambda i: (i, 0)),
            pl.BlockSpec((1, D_MODEL), lambda i: (0, 0)),
            pl.BlockSpec(w.shape, lambda i: (0, 0)),
            pl.BlockSpec((PROJ_ROWS, LANES), lambda i: (i, 0)),
            pl.BlockSpec((PROJ_ROWS, LANES), lambda i: (i, 0)),
            pl.BlockSpec((1, LANES), lambda i: (0, 0)),
        ],
        out_specs=out_specs,
        out_shape=out_shapes,
        compiler_params=_params("arbitrary"),
        name=name,
    )(x, g, w, cos_t, sin_t, bias)


def _compress_kernel(a_ref, an_ref, pea_ref, peb_ref, w1a_ref, w1b_ref, w2_ref, out_ref):
    xa = (a_ref[...].astype(F32) + pea_ref[...]).astype(BF16)
    xb = (an_ref[...].astype(F32) + peb_ref[...]).astype(BF16)
    hid = _silu(_dot(xa, w1a_ref[...]) + _dot(xb, w1b_ref[...]))
    out_ref[...] = _dot(hid.astype(BF16), w2_ref[...]).astype(out_ref.dtype)


def _compress(a, pe, w1, w2, name):
    T = a.shape[0]
    n_rows = T // NSA_CMP_STRIDE
    row_w = NSA_CMP_STRIDE * LANES
    a2 = a.reshape(n_rows, row_w)
    a2n = jnp.concatenate([a2[1:], jnp.zeros((1, row_w), a2.dtype)], axis=0)
    eye2 = jnp.eye(2, dtype=F32)

    def expand(w1_half):
        return jnp.einsum("lde,gh->lgdhe", w1_half, eye2).reshape(row_w, LANES).astype(BF16)

    def pe_row(pe_half):
        return jnp.broadcast_to(pe_half[:, None, :], (NSA_CMP_STRIDE, 2, HEAD_DIM)).reshape(1, row_w)

    w2bd = jnp.einsum("ef,gh->gehf", w2, eye2).reshape(LANES, LANES).astype(BF16)
    tile = min(256, n_rows)
    return pl.pallas_call(
        _compress_kernel,
        grid=(n_rows // tile,),
        in_specs=[
            pl.BlockSpec((tile, row_w), lambda i: (i, 0)),
            pl.BlockSpec((tile, row_w), lambda i: (i, 0)),
            pl.BlockSpec((1, row_w), lambda i: (0, 0)),
            pl.BlockSpec((1, row_w), lambda i: (0, 0)),
            pl.BlockSpec((row_w, LANES), lambda i: (0, 0)),
            pl.BlockSpec((row_w, LANES), lambda i: (0, 0)),
            pl.BlockSpec((LANES, LANES), lambda i: (0, 0)),
        ],
        out_specs=pl.BlockSpec((tile, LANES), lambda i: (i, 0)),
        out_shape=jax.ShapeDtypeStruct((n_rows, LANES), BF16),
        compiler_params=_params("arbitrary"),
        name=name,
    )(a2, a2n, pe_row(pe[:NSA_CMP_STRIDE]), pe_row(pe[NSA_CMP_STRIDE:]),
      expand(w1[:NSA_CMP_STRIDE]), expand(w1[NSA_CMP_STRIDE:]), w2bd)


def _nsa_head_query(q, h, lane):
    colv = q[:, (h % 4) * LANES:(h % 4 + 1) * LANES]
    keep = (lane < HEAD_DIM) if h < NSA_GROUP_HEADS else (lane >= HEAD_DIM)
    return jnp.where(keep, colv, jnp.zeros_like(colv))


def _masked_softmax_parts(s, mask):
    s = jnp.where(mask, s, NEG_INF)
    m = jnp.max(s, axis=-1, keepdims=True)
    p = jnp.where(mask, jnp.exp(s - m), 0.0)
    l = jnp.sum(p, axis=-1, keepdims=True)
    return p, 1.0 / jnp.maximum(l, 1e-30)


def _top_k_flags(x, idx, k, n_lanes):
    sel = jnp.zeros(x.shape, jnp.bool_)
    for _ in range(k):
        mx = jnp.max(x, axis=-1, keepdims=True)
        first = jnp.min(jnp.where(x == mx, idx, n_lanes), axis=-1, keepdims=True)
        hit = idx == first
        sel = sel | hit
        x = jnp.where(hit, -jnp.inf, x)
    return sel


def _nsa_cmp_kernel(q_ref, kc_ref, vc_ref, ov_ref, gate_ref, oc_ref, bias_ref, *, k_sel):
    tq = q_ref.shape[0]
    n_cmp = kc_ref.shape[0]
    n_selp = ov_ref.shape[1]
    t0 = pl.program_id(0) * tq
    lane = lax.broadcasted_iota(jnp.int32, (tq, LANES), 1)
    t_col = t0 + lax.broadcasted_iota(jnp.int32, (tq, 1), 0)
    cmp_end = lax.broadcasted_iota(jnp.int32, (1, n_cmp), 1) * NSA_CMP_STRIDE + (NSA_CMP_LEN - 1)
    mask = cmp_end <= t_col
    q = q_ref[...]
    kc = kc_ref[...]
    vc = vc_ref[...]
    ov = ov_ref[...]
    gates = gate_ref[...]
    imp = [jnp.zeros((tq, n_selp), F32), jnp.zeros((tq, n_selp), F32)]
    outs = []
    for h in range(NSA_HEADS):
        p, rl = _masked_softmax_parts(_dot_nt(_nsa_head_query(q, h, lane), kc), mask)
        pb = p.astype(BF16)
        outs.append(_dot(pb, vc) * (rl * gates[:, h:h + 1]))
        imp[h // NSA_GROUP_HEADS] = imp[h // NSA_GROUP_HEADS] + _dot(pb, ov) * rl
    for j in range(4):
        oc_ref[:, j * LANES:(j + 1) * LANES] = jnp.where(lane < HEAD_DIM, outs[j], outs[4 + j])
    blk = lax.broadcasted_iota(jnp.int32, (tq, n_selp), 1)
    cur = t_col >> 6
    valid = blk <= cur
    forced = (blk == 0) | (blk == cur) | (blk == cur - 1)
    for g in range(2):
        x = jnp.where(forced, FORCE_SELECT, imp[g])
        x = jnp.where(valid, x, NEG_INF)
        sel = _top_k_flags(x, blk, k_sel, n_selp)
        bias_ref[g] = jnp.where(sel & valid, 0.0, MASK_BIAS).astype(BF16)


def _nsa_cmp(nq, kc, vc, ov, gates, k_sel):
    T = nq.shape[0]
    n_cmp = kc.shape[0]
    n_selp = ov.shape[1]
    return pl.pallas_call(
        functools.partial(_nsa_cmp_kernel, k_sel=k_sel),
        grid=(T // Q_TILE,),
        in_specs=[
            pl.BlockSpec((Q_TILE, 512), lambda i: (i, 0)),
            pl.BlockSpec((n_cmp, LANES), lambda i: (0, 0)),
            pl.BlockSpec((n_cmp, LANES), lambda i: (0, 0)),
            pl.BlockSpec((n_cmp, n_selp), lambda i: (0, 0)),
            pl.BlockSpec((Q_TILE, LANES), lambda i: (i, 0)),
        ],
        out_specs=[
            pl.BlockSpec((Q_TILE, 512), lambda i: (i, 0)),
            pl.BlockSpec((2, Q_TILE, n_selp), lambda i: (0, i, 0)),
        ],
        out_shape=[
            jax.ShapeDtypeStruct((T, 512), F32),
            jax.ShapeDtypeStruct((2, T, n_selp), BF16),
        ],
        compiler_params=_params("arbitrary"),
        name="nsa_cmp_select",
    )(nq, kc, vc, ov, gates)


def _nsa_win_kernel(q_ref, kw_ref, vw_ref, gate_ref, ow_ref):
    tq = q_ref.shape[0]
    span = NSA_WINDOW + tq
    t0 = pl.program_id(0) * tq
    start = pl.multiple_of(jnp.maximum(t0 - NSA_WINDOW, 0), tq)
    lane = lax.broadcasted_iota(jnp.int32, (tq, LANES), 1)
    t_col = t0 + lax.broadcasted_iota(jnp.int32, (tq, 1), 0)
    wpos = start + lax.broadcasted_iota(jnp.int32, (1, span), 1)
    mask = (wpos <= t_col) & (wpos > t_col - NSA_WINDOW)
    q = q_ref[...]
    kw = kw_ref[pl.ds(start, span), :]
    vw = vw_ref[pl.ds(start, span), :]
    gates = gate_ref[...]
    outs = []
    for h in range(NSA_HEADS):
        p, rl = _masked_softmax_parts(_dot_nt(_nsa_head_query(q, h, lane), kw), mask)
        outs.append(_dot(p.astype(BF16), vw) * (rl * gates[:, 2 * NSA_HEADS + h:2 * NSA_HEADS + h + 1]))
    for j in range(4):
        ow_ref[:, j * LANES:(j + 1) * LANES] = jnp.where(lane < HEAD_DIM, outs[j], outs[4 + j])


def _nsa_win(nq, kw, vw, gates):
    T = nq.shape[0]
    return pl.pallas_call(
        _nsa_win_kernel,
        grid=(T // Q_TILE,),
        in_specs=[
            pl.BlockSpec((Q_TILE, 512), lambda i: (i, 0)),
            pl.BlockSpec((T, LANES), lambda i: (0, 0)),
            pl.BlockSpec((T, LANES), lambda i: (0, 0)),
            pl.BlockSpec((Q_TILE, LANES), lambda i: (i, 0)),
        ],
        out_specs=pl.BlockSpec((Q_TILE, 512), lambda i: (i, 0)),
        out_shape=jax.ShapeDtypeStruct((T, 512), F32),
        compiler_params=_params("arbitrary"),
        name="nsa_window",
    )(nq, kw, vw, gates)


def _online_softmax_step(s, v, m_ref, l_ref, acc_ref, rows):
    m_old = m_ref[rows, :]
    m_new = jnp.maximum(m_old, jnp.max(s, axis=-1, keepdims=True))
    alpha = jnp.exp(m_old - m_new)
    p = jnp.exp(s - m_new)
    l_ref[rows, :] = alpha * l_ref[rows, :] + jnp.sum(p, axis=-1, keepdims=True)
    acc_ref[rows, :] = alpha * acc_ref[rows, :] + _dot(p.astype(BF16), v)
    m_ref[rows, :] = m_new


def _nsa_sel_kernel(q_ref, bias_ref, ks_ref, vs_ref, e_ref, gate_ref, oc_ref, ow_ref, sz_ref, out_ref,
                    lhs_ref, m_ref, l_ref, acc_ref):
    tq = q_ref.shape[0]
    tk = ks_ref.shape[0]
    n_half = lhs_ref.shape[0]
    qi = pl.program_id(0)
    kt = pl.program_id(1)
    t0 = qi * tq
    last_kt = (t0 + tq - 1) // tk
    lane = lax.broadcasted_iota(jnp.int32, (tq, LANES), 1)

    @pl.when(kt == 0)
    def _():
        q = q_ref[...]
        for half in range(n_half):
            for h in range(NSA_HEADS):
                b = bias_ref[h // NSA_GROUP_HEADS, :, half * LANES:(half + 1) * LANES]
                lhs_ref[half, h * tq:(h + 1) * tq, :] = jnp.concatenate([_nsa_head_query(q, h, lane), b], axis=1)
        m_ref[...] = jnp.full(m_ref.shape, NEG_INF, F32)
        l_ref[...] = jnp.zeros(l_ref.shape, F32)
        acc_ref[...] = jnp.zeros(acc_ref.shape, F32)

    @pl.when(kt <= last_kt)
    def _():
        half = (kt * tk) // (NSA_SEL_LEN * LANES)
        rhs = jnp.concatenate([ks_ref[...], e_ref[...]], axis=1)
        s = _dot_nt(lhs_ref[half], rhs)
        kpos = kt * tk + lax.broadcasted_iota(jnp.int32, (1, tk), 1)
        t_rows = t0 + (lax.broadcasted_iota(jnp.int32, (NSA_HEADS * tq, 1), 0) & (tq - 1))
        s = jnp.where(kpos <= t_rows, s, NEG_INF)
        _online_softmax_step(s, vs_ref[...], m_ref, l_ref, acc_ref, slice(None))

    @pl.when(kt == pl.num_programs(1) - 1)
    def _():
        gates = gate_ref[...]
        o = acc_ref[...] / l_ref[...]
        outs = [o[h * tq:(h + 1) * tq] * gates[:, NSA_HEADS + h:NSA_HEADS + h + 1] for h in range(NSA_HEADS)]
        for j in range(4):
            cs = slice(j * LANES, (j + 1) * LANES)
            a = jnp.where(lane < HEAD_DIM, outs[j], outs[4 + j]) + oc_ref[:, cs] + ow_ref[:, cs]
            out_ref[:, cs] = (a * sz_ref[:, cs].astype(F32)).astype(out_ref.dtype)


def _nsa_sel(nq, bias, ks, vs, e_sel, gates, oc, ow, sz):
    T = nq.shape[0]
    n_selp = bias.shape[2]
    n_half = n_selp // LANES

    def kv_map(qi, kt):
        return (jnp.minimum(kt, (qi * Q_TILE + Q_TILE - 1) // K_TILE), 0)

    return pl.pallas_call(
        _nsa_sel_kernel,
        grid=(T // Q_TILE, T // K_TILE),
        in_specs=[
            pl.BlockSpec((Q_TILE, 512), lambda qi, kt: (qi, 0)),
            pl.BlockSpec((2, Q_TILE, n_selp), lambda qi, kt: (0, qi, 0)),
            pl.BlockSpec((K_TILE, LANES), kv_map),
            pl.BlockSpec((K_TILE, LANES), kv_map),
            pl.BlockSpec((K_TILE, LANES), kv_map),
            pl.BlockSpec((Q_TILE, LANES), lambda qi, kt: (qi, 0)),
            pl.BlockSpec((Q_TILE, 512), lambda qi, kt: (qi, 0)),
            pl.BlockSpec((Q_TILE, 512), lambda qi, kt: (qi, 0)),
            pl.BlockSpec((Q_TILE, 512), lambda qi, kt: (qi, 0)),
        ],
        out_specs=pl.BlockSpec((Q_TILE, 512), lambda qi, kt: (qi, 0)),
        out_shape=jax.ShapeDtypeStruct((T, 512), BF16),
        scratch_shapes=[
            pltpu.VMEM((n_half, NSA_HEADS * Q_TILE, 2 * LANES), BF16),
            pltpu.VMEM((NSA_HEADS * Q_TILE, 1), F32),
            pltpu.VMEM((NSA_HEADS * Q_TILE, 1), F32),
            pltpu.VMEM((NSA_HEADS * Q_TILE, LANES), F32),
        ],
        compiler_params=_params("arbitrary", "arbitrary"),
        name="nsa_selected",
    )(nq, bias, ks, vs, e_sel, gates, oc, ow, sz)


def _moba_kernel(q_ref, kmean_ref, k_ref, v_ref, e_ref, sz_ref, out_ref, lhs_ref, m_ref, l_ref, acc_ref, *, top):
    tq = q_ref.shape[0]
    tk = k_ref.shape[0]
    n_pairs = MOBA_HEADS // 2
    qi = pl.program_id(0)
    kt = pl.program_id(1)
    t0 = qi * tq
    last_kt = (t0 + tq - 1) // tk
    cur = t0 // MOBA_BLOCK
    lane = lax.broadcasted_iota(jnp.int32, (tq, LANES), 1)

    @pl.when(kt == 0)
    def _():
        q = q_ref[...]
        blk = lax.broadcasted_iota(jnp.int32, (2 * tq, LANES), 1)
        past = blk < cur
        for p in range(n_pairs):
            colv = q[:, p * LANES:(p + 1) * LANES]
            zero = jnp.zeros_like(colv)
            qp = jnp.concatenate([jnp.where(lane < HEAD_DIM, colv, zero), jnp.where(lane >= HEAD_DIM, colv, zero)],
                                 axis=0)
            gate = jnp.where(past, _dot_nt(qp, kmean_ref[:, p * LANES:(p + 1) * LANES]), NEG_INF)
            sel = _top_k_flags(gate, blk, top, LANES)
            bias = jnp.where((sel & past) | (blk == cur), 0.0, MASK_BIAS).astype(BF16)
            lhs_ref[p] = jnp.concatenate([qp, bias], axis=1)
        m_ref[...] = jnp.full(m_ref.shape, NEG_INF, F32)
        l_ref[...] = jnp.zeros(l_ref.shape, F32)
        acc_ref[...] = jnp.zeros(acc_ref.shape, F32)

    @pl.when(kt <= last_kt)
    def _():
        e = e_ref[...]
        kpos = kt * tk + lax.broadcasted_iota(jnp.int32, (1, tk), 1)
        t_rows = t0 + (lax.broadcasted_iota(jnp.int32, (2 * tq, 1), 0) & (tq - 1))
        causal = kpos <= t_rows
        for p in range(n_pairs):
            cs = slice(p * LANES, (p + 1) * LANES)
            s = _dot_nt(lhs_ref[p], jnp.concatenate([k_ref[:, cs], e], axis=1))
            s = jnp.where(causal, s, NEG_INF)
            _online_softmax_step(s, v_ref[:, cs], m_ref, l_ref, acc_ref, slice(p * 2 * tq, (p + 1) * 2 * tq))

    @pl.when(kt == pl.num_programs(1) - 1)
    def _():
        o = acc_ref[...] / l_ref[...]
        for p in range(n_pairs):
            cs = slice(p * LANES, (p + 1) * LANES)
            a = jnp.where(lane < HEAD_DIM, o[p * 2 * tq:p * 2 * tq + tq], o[p * 2 * tq + tq:(p + 1) * 2 * tq])
            out_ref[:, cs] = (a * sz_ref[:, cs].astype(F32)).astype(out_ref.dtype)


def _moba(mq, kmean, mk, mv, e_blk, sz, top):
    T = mq.shape[0]

    def kv_map(qi, kt):
        return (jnp.minimum(kt, (qi * Q_TILE + Q_TILE - 1) // K_TILE), 0)

    return pl.pallas_call(
        functools.partial(_moba_kernel, top=top),
        grid=(T // Q_TILE, T // K_TILE),
        in_specs=[
            pl.BlockSpec((Q_TILE, 512), lambda qi, kt: (qi, 0)),
            pl.BlockSpec((LANES, 512), lambda qi, kt: (0, 0)),
            pl.BlockSpec((K_TILE, 512), kv_map),
            pl.BlockSpec((K_TILE, 512), kv_map),
            pl.BlockSpec((K_TILE, LANES), kv_map),
            pl.BlockSpec((Q_TILE, 512), lambda qi, kt: (qi, 0)),
        ],
        out_specs=pl.BlockSpec((Q_TILE, 512), lambda qi, kt: (qi, 0)),
        out_shape=jax.ShapeDtypeStruct((T, 512), BF16),
        scratch_shapes=[
            pltpu.VMEM((MOBA_HEADS // 2, 2 * Q_TILE, 2 * LANES), BF16),
            pltpu.VMEM((MOBA_HEADS * Q_TILE, 1), F32),
            pltpu.VMEM((MOBA_HEADS * Q_TILE, 1), F32),
            pltpu.VMEM((MOBA_HEADS * Q_TILE, LANES), F32),
        ],
        compiler_params=_params("arbitrary", "arbitrary"),
        name="moba",
    )(mq, kmean, mk, mv, e_blk, sz)


def _mem_kv_kernel(mem_ref, g_ref, w0_ref, w1_ref, kv0_ref, kv1_ref):
    mem_n = _rms(mem_ref[...], g_ref[...]).astype(BF16)
    kv0_ref[...] = _dot(mem_n, w0_ref[...]).astype(kv0_ref.dtype)
    kv1_ref[...] = _dot(mem_n, w1_ref[...]).astype(kv1_ref.dtype)


def _mem_kv(mem, g, w0, w1):
    n = mem.shape[0]
    kv_w = 2 * MEM_HEADS * HEAD_DIM
    return pl.pallas_call(
        _mem_kv_kernel,
        grid=(1,),
        in_specs=[
            pl.BlockSpec((n, D_MODEL), lambda i: (0, 0)),
            pl.BlockSpec((1, D_MODEL), lambda i: (0, 0)),
            pl.BlockSpec((D_MODEL, kv_w), lambda i: (0, 0)),
            pl.BlockSpec((D_MODEL, kv_w), lambda i: (0, 0)),
        ],
        out_specs=[pl.BlockSpec((n, kv_w), lambda i: (0, 0)), pl.BlockSpec((n, kv_w), lambda i: (0, 0))],
        out_shape=[jax.ShapeDtypeStruct((n, kv_w), BF16), jax.ShapeDtypeStruct((n, kv_w), BF16)],
        compiler_params=_params("arbitrary"),
        name="mem_kv",
    )(mem, g, w0, w1)


def _mem_attention(eq, kv, sez, lane):
    rows = eq.shape[0]
    mem_w = MEM_HEADS * HEAD_DIM
    cols = []
    for p in range(MEM_HEADS // 2):
        cs = slice(p * LANES, (p + 1) * LANES)
        colv = eq[:, cs]
        zero = jnp.zeros_like(colv)
        qp = jnp.concatenate([jnp.where(lane < HEAD_DIM, colv, zero), jnp.where(lane >= HEAD_DIM, colv, zero)], axis=0)
        s = _dot_nt(qp, kv[:, cs])
        pe = jnp.exp(s - jnp.max(s, axis=-1, keepdims=True))
        o = _dot(pe.astype(BF16), kv[:, mem_w + p * LANES:mem_w + (p + 1) * LANES]) / jnp.sum(pe, axis=-1, keepdims=True)
        cols.append(jnp.where(lane < HEAD_DIM, o[:rows], o[rows:]) * sez[:, cs].astype(F32))
    return jnp.concatenate(cols, axis=1).astype(BF16)


def _out0_kernel(ag_ref, bg_ref, eq_ref, sez_ref, kv_ref, wa_ref, wb_ref, wm_ref, x_ref, out_ref):
    rows = x_ref.shape[0]
    lane = lax.broadcasted_iota(jnp.int32, (rows, LANES), 1)
    mg = _mem_attention(eq_ref[...], kv_ref[...], sez_ref[...], lane)
    y = _dot(ag_ref[...], wa_ref[...]) + _dot(bg_ref[...], wb_ref[...]) + _dot(mg, wm_ref[...])
    out_ref[...] = x_ref[...] + y


def _out0(ag, bg, eq, sez, kv, wa, wb, wm, x):
    T = x.shape[0]
    rt = PROJ_ROWS
    row = lambda w: pl.BlockSpec((rt, w), lambda i: (i, 0))
    full = lambda a: pl.BlockSpec(a.shape, lambda i: (0, 0))
    return pl.pallas_call(
        _out0_kernel,
        grid=(T // rt,),
        in_specs=[row(512), row(512), row(256), row(256), full(kv), full(wa), full(wb), full(wm), row(D_MODEL)],
        out_specs=row(D_MODEL),
        out_shape=jax.ShapeDtypeStruct((T, D_MODEL), F32),
        compiler_params=_params("arbitrary"),
        name="out_proj0",
    )(ag, bg, eq, sez, kv, wa, wb, wm, x)


def _out1_kernel(rg_ref, eq_ref, sez_ref, kv_ref, wr_ref, wm_ref, x_ref, g_ref, out_ref):
    rows = x_ref.shape[0]
    lane = lax.broadcasted_iota(jnp.int32, (rows, LANES), 1)
    mg = _mem_attention(eq_ref[...], kv_ref[...], sez_ref[...], lane)
    y = _dot(rg_ref[...], wr_ref[...]) + _dot(mg, wm_ref[...])
    out_ref[...] = _rms(x_ref[...] + y, g_ref[...])


def _out1(rg, eq, sez, kv, wr, wm, x, g):
    T = x.shape[0]
    rt = PROJ_ROWS
    row = lambda w: pl.BlockSpec((rt, w), lambda i: (i, 0))
    full = lambda a: pl.BlockSpec(a.shape, lambda i: (0, 0))
    return pl.pallas_call(
        _out1_kernel,
        grid=(T // rt,),
        in_specs=[row(RET_HEADS * RET_V_DIM), row(256), row(256), full(kv), full(wr), full(wm), row(D_MODEL), full(g)],
        out_specs=row(D_MODEL),
        out_shape=jax.ShapeDtypeStruct((T, D_MODEL), F32),
        compiler_params=_params("arbitrary"),
        name="out_proj1_norm",
    )(rg, eq, sez, kv, wr, wm, x, g)


def _retention_kernel(q_ref, k_ref, v_ref, sz_ref, decay_ref, qd_ref, kd_ref, cd_ref, out_ref, state_ref):
    @pl.when(pl.program_id(0) == 0)
    def _():
        state_ref[...] = jnp.zeros(state_ref.shape, F32)

    for h in range(RET_HEADS):
        qs = slice(h * RET_QK_DIM, (h + 1) * RET_QK_DIM)
        vsl = slice(h * RET_V_DIM, (h + 1) * RET_V_DIM)
        qh = q_ref[:, qs]
        kh = k_ref[:, qs]
        vh = v_ref[:, vsl]
        st = state_ref[h]
        inner = _dot_nt(qh, kh.astype(BF16)) * decay_ref[h]
        o = _dot(inner.astype(BF16), vh) + _dot(qh, st.astype(BF16)) * qd_ref[h]
        state_ref[h] = st * cd_ref[h] + _dot_tn((kh * kd_ref[h]).astype(BF16), vh)
        mu = jnp.mean(o, axis=-1, keepdims=True)
        d = o - mu
        on = d * lax.rsqrt(jnp.mean(d * d, axis=-1, keepdims=True) + NORM_EPS)
        out_ref[:, vsl] = (on * sz_ref[:, vsl].astype(F32)).astype(out_ref.dtype)


def _retention(rq, rk, rv, sz):
    T = rq.shape[0]
    C = RET_CHUNK
    log_g = jnp.log(1.0 - 2.0 ** (-5.0 - jnp.arange(RET_HEADS, dtype=F32)))
    i = jnp.arange(C, dtype=F32)
    diff = i[:, None] - i[None, :]
    decay = jnp.where(diff >= 0, jnp.exp(jnp.maximum(diff, 0.0)[None] * log_g[:, None, None]), 0.0)
    q_decay = jnp.exp((i + 1.0)[None, :] * log_g[:, None])[:, :, None]
    k_decay = jnp.exp((C - 1.0 - i)[None, :] * log_g[:, None])[:, :, None]
    chunk_decay = jnp.exp(C * log_g)[:, None, None]
    qk_w = RET_HEADS * RET_QK_DIM
    v_w = RET_HEADS * RET_V_DIM
    return pl.pallas_call(
        _retention_kernel,
        grid=(T // C,),
        in_specs=[
            pl.BlockSpec((C, qk_w), lambda c: (c, 0)),
            pl.BlockSpec((C, qk_w), lambda c: (c, 0)),
            pl.BlockSpec((C, v_w), lambda c: (c, 0)),
            pl.BlockSpec((C, v_w), lambda c: (c, 0)),
            pl.BlockSpec((RET_HEADS, C, C), lambda c: (0, 0, 0)),
            pl.BlockSpec((RET_HEADS, C, 1), lambda c: (0, 0, 0)),
            pl.BlockSpec((RET_HEADS, C, 1), lambda c: (0, 0, 0)),
            pl.BlockSpec((RET_HEADS, 1, 1), lambda c: (0, 0, 0)),
        ],
        out_specs=pl.BlockSpec((C, v_w), lambda c: (c, 0)),
        out_shape=jax.ShapeDtypeStruct((T, v_w), BF16),
        scratch_shapes=[pltpu.VMEM((RET_HEADS, RET_QK_DIM, RET_V_DIM), F32)],
        compiler_params=_params("arbitrary"),
        name="retention",
    )(rq, rk, rv, sz, decay, q_decay, k_decay, chunk_decay)


def _head_cols(order):
    return np.concatenate([np.arange(h * HEAD_DIM, (h + 1) * HEAD_DIM) for h in order])


def _one_hot_blocks(T, block):
    b = (np.arange(T) // block) % LANES
    return jnp.asarray(np.eye(LANES, dtype=np.float32)[b], dtype=BF16)


def kernel(x, mem, positions, l0_norm_g, l0_w_in, l0_nsa_gate_b, l0_cmp_pe_k, l0_cmp_w1_k, l0_cmp_w2_k,
           l0_cmp_pe_v, l0_cmp_w1_v, l0_cmp_w2_v, l0_w_mem_kv, l0_w_out,
           l1_norm_g, l1_w_in, l1_w_mem_kv, l1_w_out, mem_norm_g, final_norm_g):
    B, T, _ = x.shape
    assert B == 1 and T % K_TILE == 0 and T >= NSA_WINDOW + Q_TILE
    n_sel = T // NSA_SEL_LEN
    n_selp = -(-n_sel // LANES) * LANES
    k_sel = min(NSA_SEL_TOPK, n_sel)
    n_blk = T // MOBA_BLOCK
    assert n_blk <= LANES
    moba_top = max(1, min(MOBA_TOPK, n_blk - 1))
    x2 = x[0]
    pos = positions[0].astype(F32)

    attn_inv = 1.0 / (ROPE_THETA ** (jnp.arange(0, HEAD_DIM, 2, dtype=F32) / HEAD_DIM))
    ret_inv = 1.0 / (ROPE_THETA ** jnp.linspace(0.0, 1.0, RET_QK_DIM // 2, dtype=F32))
    ang = pos[:, None] * attn_inv
    cos64 = jnp.tile(jnp.cos(ang), (1, 4))
    sin64 = jnp.tile(jnp.concatenate([-jnp.sin(ang), jnp.sin(ang)], axis=1), (1, 2))
    rang = pos[:, None] * ret_inv
    cos256, sin256 = jnp.cos(rang), jnp.sin(rang)

    offs = np.cumsum([0, 512, 128, 128, 128, 128, 128, 128, 24, 512, 512, 512, 512, 512, 256, 256])
    nsa_cols = _head_cols(NSA_HEAD_ORDER)
    col_idx = np.concatenate([
        offs[0] + nsa_cols,
        np.arange(offs[1], offs[7]),
    ])
    w0 = jnp.concatenate([
        l0_w_in[:, col_idx],
        l0_w_in[:, offs[7]:offs[8]], jnp.zeros((D_MODEL, LANES - 24), F32),
        l0_w_in[:, offs[8] + nsa_cols],
        l0_w_in[:, offs[9]:],
    ], axis=1).astype(BF16)
    gate_b = jnp.concatenate([l0_nsa_gate_b, jnp.zeros((LANES - 24,), F32)])[None, :]
    groups0 = [
        (512, "rot64", 0.125), (128, "rot64", 1.0), (128, "plain", 1.0), (128, "rot64", 1.0), (128, "plain", 1.0),
        (128, "rot64", 1.0), (128, "plain", 1.0), (128, "gate", 1.0), (512, "silu", 1.0),
        (512, "rot64", 0.125), (512, "rot64_mean", 1.0), (512, "plain", 1.0), (512, "silu", 1.0),
        (256, "plain", 0.125), (256, "silu", 1.0),
    ]
    dts0 = [BF16] * 7 + [F32] + [BF16] * 7
    (nq, nkc, nvc, nks, nvs, nkw, nvw, gates, snz, mq, mk, kmean, mv, smz, eq0, sez0) = _proj(
        x2, l0_norm_g[None, :], w0, cos64, sin64, gate_b, groups0, dts0, "proj0")

    kv0, kv1 = _mem_kv(mem[0], mem_norm_g[None, :], l0_w_mem_kv.astype(BF16), l1_w_mem_kv.astype(BF16))

    kc = _compress(nkc, l0_cmp_pe_k, l0_cmp_w1_k, l0_cmp_w2_k, "compress_k")
    vc = _compress(nvc, l0_cmp_pe_v, l0_cmp_w1_v, l0_cmp_w2_v, "compress_v")
    n_cmp_rows = T // NSA_CMP_STRIDE
    cmp_start = np.arange(n_cmp_rows)[:, None] * NSA_CMP_STRIDE
    sel_start = np.arange(n_selp)[None, :] * NSA_SEL_LEN
    overlap = ((cmp_start < sel_start + NSA_SEL_LEN) & (cmp_start + NSA_CMP_LEN > sel_start)
               & (np.arange(n_selp)[None, :] < n_sel))
    ov = jnp.asarray(overlap.astype(np.float32), dtype=BF16)
    oc, sel_bias = _nsa_cmp(nq, kc, vc, ov, gates, k_sel)
    ow = _nsa_win(nq, nkw, nvw, gates)
    ag = _nsa_sel(nq, sel_bias, nks, nvs, _one_hot_blocks(T, NSA_SEL_LEN), gates, oc, ow, snz)

    kmean_p = jnp.concatenate([kmean[:, 0, :], jnp.zeros((LANES - n_blk, 512), F32)], axis=0).astype(BF16)
    bg = _moba(mq, kmean_p, mk, mv, _one_hot_blocks(T, MOBA_BLOCK), smz, moba_top)

    wa = l0_w_out[:512][nsa_cols].astype(BF16)
    wb = l0_w_out[512:1024].astype(BF16)
    wm0 = l0_w_out[1024:].astype(BF16)
    x1 = _out0(ag, bg, eq0, sez0, kv0, wa, wb, wm0, x2)

    groups1 = [
        (1024, "rot256", 1.0), (1024, "rot256", RET_QK_DIM ** -0.5), (2048, "plain", 1.0), (2048, "silu", 1.0),
        (256, "plain", 0.125), (256, "silu", 1.0),
    ]
    dts1 = [BF16, F32, BF16, BF16, BF16, BF16]
    rq, rk, rv, srz, eq1, sez1 = _proj(x1, l1_norm_g[None, :], l1_w_in.astype(BF16), cos256, sin256,
                                       jnp.zeros((1, LANES), F32), groups1, dts1, "proj1")
    rg = _retention(rq, rk, rv, srz)
    v_w = RET_HEADS * RET_V_DIM
    out = _out1(rg, eq1, sez1, kv1, l1_w_out[:v_w].astype(BF16), l1_w_out[v_w:].astype(BF16), x1,
                final_norm_g[None, :])
    return out[None]
```

```python
import functools

import numpy as np
import jax
import jax.numpy as jnp
from jax import lax
from jax.experimental import pallas as pl
from jax.experimental.pallas import tpu as pltpu

F32 = jnp.float32
BF16 = jnp.bfloat16

D_MODEL = 1024
HEAD_DIM = 64
ROPE_THETA = 10000.0
NORM_EPS = 1e-6
NEG_INF = -1e30
FORCE_SELECT = 1e9
MASK_BIAS = -1e9

NSA_HEADS = 8
NSA_GROUP_HEADS = 4
NSA_CMP_LEN = 32
NSA_CMP_STRIDE = 16
NSA_SEL_LEN = 64
NSA_SEL_TOPK = 16
NSA_WINDOW = 512
MOBA_HEADS = 8
MOBA_BLOCK = 256
MOBA_TOPK = 3
MEM_LEN = 256
MEM_HEADS = 4
RET_HEADS = 4
RET_QK_DIM = 256
RET_V_DIM = 512
RET_CHUNK = 128

LANES = 128
PROJ_ROWS = 256
PROJ_COLS = 512
Q_TILE = 128
K_TILE = 512
VMEM_LIMIT = 56 * 1024 * 1024

NSA_HEAD_ORDER = (0, 4, 1, 5, 2, 6, 3, 7)


def _dot(a, b):
    return jnp.dot(a, b, preferred_element_type=F32)


def _dot_nt(a, b):
    return lax.dot_general(a, b, (((1,), (1,)), ((), ())), preferred_element_type=F32)


def _dot_tn(a, b):
    return lax.dot_general(a, b, (((0,), (0,)), ((), ())), preferred_element_type=F32)


def _silu(y):
    return y * (1.0 / (1.0 + jnp.exp(-y)))


def _rms(x, g):
    return x * lax.rsqrt(jnp.mean(x * x, axis=-1, keepdims=True) + NORM_EPS) * g


def _params(*sem):
    return pltpu.CompilerParams(dimension_semantics=sem, vmem_limit_bytes=VMEM_LIMIT)


def _proj_kernel(x_ref, g_ref, w_ref, c_ref, s_ref, b_ref, *out_refs, groups):
    h = _rms(x_ref[...], g_ref[...]).astype(BF16)
    rows = x_ref.shape[0]
    lane = lax.broadcasted_iota(jnp.int32, (rows, LANES), 1)
    low_half = (lane & 32) == 0
    col = 0
    oi = 0
    for width, kind, scale in groups:
        out_ref = out_refs[oi]
        oi += 1
        mean_ref = None
        if kind == "rot64_mean":
            mean_ref = out_refs[oi]
            oi += 1
        for c0 in range(0, width, PROJ_COLS):
            cw = min(PROJ_COLS, width - c0)
            y = _dot(h, w_ref[:, col + c0:col + c0 + cw])
            if kind in ("rot64", "rot64_mean"):
                c = c_ref[...]
                s = s_ref[...]
                pieces = []
                for j in range(cw // LANES):
                    p = y[:, j * LANES:(j + 1) * LANES]
                    partner = jnp.where(low_half, pltpu.roll(p, LANES - 32, 1), pltpu.roll(p, 32, 1))
                    pieces.append(p * c + partner * s)
                y = pieces[0] if len(pieces) == 1 else jnp.concatenate(pieces, axis=1)
            elif kind == "rot256":
                c = c_ref[...]
                s = s_ref[...]
                pieces = []
                for j in range(cw // RET_QK_DIM):
                    x1 = y[:, j * RET_QK_DIM:j * RET_QK_DIM + LANES]
                    x2 = y[:, j * RET_QK_DIM + LANES:(j + 1) * RET_QK_DIM]
                    pieces.append(x1 * c - x2 * s)
                    pieces.append(x1 * s + x2 * c)
                y = jnp.concatenate(pieces, axis=1)
            elif kind == "silu":
                y = _silu(y)
            elif kind == "gate":
                y = 1.0 / (1.0 + jnp.exp(-(y + b_ref[...])))
            if scale != 1.0:
                y = y * scale
            if mean_ref is not None:
                mean_ref[0, :, c0:c0 + cw] = jnp.mean(y, axis=0, keepdims=True)
            out_ref[:, c0:c0 + cw] = y.astype(out_ref.dtype)
        col += width


def _proj(x, g, w, cos_t, sin_t, bias, groups, out_dtypes, name):
    T = x.shape[0]
    n_tiles = T // PROJ_ROWS
    out_shapes = []
    out_specs = []
    for (width, kind, _), dt in zip(groups, out_dtypes):
        out_shapes.append(jax.ShapeDtypeStruct((T, width), dt))
        out_specs.append(pl.BlockSpec((PROJ_ROWS, width), lambda i: (i, 0)))
        if kind == "rot64_mean":
            out_shapes.append(jax.ShapeDtypeStruct((n_tiles, 1, width), F32))
            out_specs.append(pl.BlockSpec((1, 1, width), lambda i: (i, 0, 0)))
    return pl.pallas_call(
        functools.partial(_proj_kernel, groups=tuple(groups)),
        grid=(n_tiles,),
        in_specs=[
            pl.BlockSpec((PROJ_ROWS, D_MODEL), lambda i: (i, 0)),
            pl.BlockSpec((1, D_MODEL), lambda i: (0, 0)),
            pl.BlockSpec(w.shape, lambda i: (0, 0)),
            pl.BlockSpec((PROJ_ROWS, LANES), lambda i: (i, 0)),
            pl.BlockSpec((PROJ_ROWS, LANES), lambda i: (i, 0)),
            pl.BlockSpec((1, LANES), lambda i: (0, 0)),
        ],
        out_specs=out_specs,
        out_shape=out_shapes,
        compiler_params=_params("arbitrary"),
        name=name,
    )(x, g, w, cos_t, sin_t, bias)


def _compress_kernel(a_ref, an_ref, pea_ref, peb_ref, w1a_ref, w1b_ref, w2_ref, out_ref):
    xa = (a_ref[...].astype(F32) + pea_ref[...]).astype(BF16)
    xb = (an_ref[...].astype(F32) + peb_ref[...]).astype(BF16)
    hid = _silu(_dot(xa, w1a_ref[...]) + _dot(xb, w1b_ref[...]))
    out_ref[...] = _dot(hid.astype(BF16), w2_ref[...]).astype(out_ref.dtype)


def _compress(a, pe, w1, w2, name):
    T = a.shape[0]
    n_rows = T // NSA_CMP_STRIDE
    row_w = NSA_CMP_STRIDE * LANES
    a2 = a.reshape(n_rows, row_w)
    a2n = jnp.concatenate([a2[1:], jnp.zeros((1, row_w), a2.dtype)], axis=0)
    eye2 = jnp.eye(2, dtype=F32)

    def expand(w1_half):
        return jnp.einsum("lde,gh->lgdhe", w1_half, eye2).reshape(row_w, LANES).astype(BF16)

    def pe_row(pe_half):
        return jnp.broadcast_to(pe_half[:, None, :], (NSA_CMP_STRIDE, 2, HEAD_DIM)).reshape(1, row_w)

    w2bd = jnp.einsum("ef,gh->gehf", w2, eye2).reshape(LANES, LANES).astype(BF16)
    tile = min(256, n_rows)
    return pl.pallas_call(
        _compress_kernel,
        grid=(n_rows // tile,),
        in_specs=[
            pl.BlockSpec((tile, row_w), lambda i: (i, 0)),
            pl.BlockSpec((tile, row_w), lambda i: (i, 0)),
            pl.BlockSpec((1, row_w), lambda i: (0, 0)),
            pl.BlockSpec((1, row_w), lambda i: (0, 0)),
            pl.BlockSpec((row_w, LANES), lambda i: (0, 0)),
            pl.BlockSpec((row_w, LANES), lambda i: (0, 0)),
            pl.BlockSpec((LANES, LANES), lambda i: (0, 0)),
        ],
        out_specs=pl.BlockSpec((tile, LANES), lambda i: (i, 0)),
        out_shape=jax.ShapeDtypeStruct((n_rows, LANES), BF16),
        compiler_params=_params("arbitrary"),
        name=name,
    )(a2, a2n, pe_row(pe[:NSA_CMP_STRIDE]), pe_row(pe[NSA_CMP_STRIDE:]),
      expand(w1[:NSA_CMP_STRIDE]), expand(w1[NSA_CMP_STRIDE:]), w2bd)


def _nsa_head_query(q, h, lane):
    colv = q[:, (h % 4) * LANES:(h % 4 + 1) * LANES]
    keep = (lane < HEAD_DIM) if h < NSA_GROUP_HEADS else (lane >= HEAD_DIM)
    return jnp.where(keep, colv, jnp.zeros_like(colv))


def _masked_softmax_parts(s, mask):
    s = jnp.where(mask, s, NEG_INF)
    m = jnp.max(s, axis=-1, keepdims=True)
    p = jnp.where(mask, jnp.exp(s - m), 0.0)
    l = jnp.sum(p, axis=-1, keepdims=True)
    return p, 1.0 / jnp.maximum(l, 1e-30)


def _top_k_flags(x, idx, k, n_lanes):
    sel = jnp.zeros(x.shape, jnp.bool_)
    for _ in range(k):
        mx = jnp.max(x, axis=-1, keepdims=True)
        first = jnp.min(jnp.where(x == mx, idx, n_lanes), axis=-1, keepdims=True)
        hit = idx == first
        sel = sel | hit
        x = jnp.where(hit, -jnp.inf, x)
    return sel


def _nsa_cmp_kernel(q_ref, kc_ref, vc_ref, ov_ref, gate_ref, oc_ref, bias_ref, *, k_sel):
    tq = q_ref.shape[0]
    n_cmp = kc_ref.shape[0]
    n_selp = ov_ref.shape[1]
    t0 = pl.program_id(0) * tq
    lane = lax.broadcasted_iota(jnp.int32, (tq, LANES), 1)
    t_col = t0 + lax.broadcasted_iota(jnp.int32, (tq, 1), 0)
    cmp_end = lax.broadcasted_iota(jnp.int32, (1, n_cmp), 1) * NSA_CMP_STRIDE + (NSA_CMP_LEN - 1)
    mask = cmp_end <= t_col
    q = q_ref[...]
    kc = kc_ref[...]
    vc = vc_ref[...]
    ov = ov_ref[...]
    gates = gate_ref[...]
    imp = [jnp.zeros((tq, n_selp), F32), jnp.zeros((tq, n_selp), F32)]
    outs = []
    for h in range(NSA_HEADS):
        p, rl = _masked_softmax_parts(_dot_nt(_nsa_head_query(q, h, lane), kc), mask)
        pb = p.astype(BF16)
        outs.append(_dot(pb, vc) * (rl * gates[:, h:h + 1]))
        imp[h // NSA_GROUP_HEADS] = imp[h // NSA_GROUP_HEADS] + _dot(pb, ov) * rl
    for j in range(4):
        oc_ref[:, j * LANES:(j + 1) * LANES] = jnp.where(lane < HEAD_DIM, outs[j], outs[4 + j])
    blk = lax.broadcasted_iota(jnp.int32, (tq, n_selp), 1)
    cur = t_col >> 6
    valid = blk <= cur
    forced = (blk == 0) | (blk == cur) | (blk == cur - 1)
    for g in range(2):
        x = jnp.where(forced, FORCE_SELECT, imp[g])
        x = jnp.where(valid, x, NEG_INF)
        sel = _top_k_flags(x, blk, k_sel, n_selp)
        bias_ref[g] = jnp.where(sel & valid, 0.0, MASK_BIAS).astype(BF16)


def _nsa_cmp(nq, kc, vc, ov, gates, k_sel):
    T = nq.shape[0]
    n_cmp = kc.shape[0]
    n_selp = ov.shape[1]
    return pl.pallas_call(
        functools.partial(_nsa_cmp_kernel, k_sel=k_sel),
        grid=(T // Q_TILE,),
        in_specs=[
            pl.BlockSpec((Q_TILE, 512), lambda i: (i, 0)),
            pl.BlockSpec((n_cmp, LANES), lambda i: (0, 0)),
            pl.BlockSpec((n_cmp, LANES), lambda i: (0, 0)),
            pl.BlockSpec((n_cmp, n_selp), lambda i: (0, 0)),
            pl.BlockSpec((Q_TILE, LANES), lambda i: (i, 0)),
        ],
        out_specs=[
            pl.BlockSpec((Q_TILE, 512), lambda i: (i, 0)),
            pl.BlockSpec((2, Q_TILE, n_selp), lambda i: (0, i, 0)),
        ],
        out_shape=[
            jax.ShapeDtypeStruct((T, 512), F32),
            jax.ShapeDtypeStruct((2, T, n_selp), BF16),
        ],
        compiler_params=_params("arbitrary"),
        name="nsa_cmp_select",
    )(nq, kc, vc, ov, gates)


def _nsa_win_kernel(q_ref, kw_ref, vw_ref, gate_ref, ow_ref):
    tq = q_ref.shape[0]
    span = NSA_WINDOW + tq
    t0 = pl.program_id(0) * tq
    start = pl.multiple_of(jnp.maximum(t0 - NSA_WINDOW, 0), tq)
    lane = lax.broadcasted_iota(jnp.int32, (tq, LANES), 1)
    t_col = t0 + lax.broadcasted_iota(jnp.int32, (tq, 1), 0)
    wpos = start + lax.broadcasted_iota(jnp.int32, (1, span), 1)
    mask = (wpos <= t_col) & (wpos > t_col - NSA_WINDOW)
    q = q_ref[...]
    kw = kw_ref[pl.ds(start, span), :]
    vw = vw_ref[pl.ds(start, span), :]
    gates = gate_ref[...]
    outs = []
    for h in range(NSA_HEADS):
        p, rl = _masked_softmax_parts(_dot_nt(_nsa_head_query(q, h, lane), kw), mask)
        outs.append(_dot(p.astype(BF16), vw) * (rl * gates[:, 2 * NSA_HEADS + h:2 * NSA_HEADS + h + 1]))
    for j in range(4):
        ow_ref[:, j * LANES:(j + 1) * LANES] = jnp.where(lane < HEAD_DIM, outs[j], outs[4 + j])


def _nsa_win(nq, kw, vw, gates):
    T = nq.shape[0]
    return pl.pallas_call(
        _nsa_win_kernel,
        grid=(T // Q_TILE,),
        in_specs=[
            pl.BlockSpec((Q_TILE, 512), lambda i: (i, 0)),
            pl.BlockSpec((T, LANES), lambda i: (0, 0)),
            pl.BlockSpec((T, LANES), lambda i: (0, 0)),
            pl.BlockSpec((Q_TILE, LANES), lambda i: (i, 0)),
        ],
        out_specs=pl.BlockSpec((Q_TILE, 512), lambda i: (i, 0)),
        out_shape=jax.ShapeDtypeStruct((T, 512), F32),
        compiler_params=_params("arbitrary"),
        name="nsa_window",
    )(nq, kw, vw, gates)


def _flash_scratch(rows):
    return [
        pltpu.VMEM((rows, K_TILE), F32), pltpu.VMEM((rows, K_TILE), F32),
        pltpu.VMEM((rows, K_TILE), BF16), pltpu.VMEM((rows, K_TILE), BF16),
        pltpu.VMEM((rows, 1), F32), pltpu.VMEM((rows, 1), F32),
        pltpu.VMEM((rows, 1), F32), pltpu.VMEM((rows, 1), F32),
        pltpu.VMEM((rows, LANES), F32),
    ]


def _flash_body(lhs_ref, k_ref, e_ref, v_ref, t_of_rows, n_full, tokens_per_half, scratch):
    s_a, s_b, p_a, p_b, al_a, al_b, m_ref, l_ref, acc_ref = scratch
    tk = K_TILE
    rows = acc_ref.shape[0]
    last_mem_tile = k_ref.shape[0] // tk - 1

    def tile_start(j):
        return pl.multiple_of(jnp.clip(j, 0, last_mem_tile) * tk, tk)

    def scores(j, s_out):
        k0 = tile_start(j)
        rhs = jnp.concatenate([k_ref[pl.ds(k0, tk), :], e_ref[pl.ds(k0, tk), :]], axis=1)
        s_out[...] = _dot_nt(lhs_ref[k0 // tokens_per_half], rhs)

    def softmax(j, s_in, p_out, al_out, causal):
        s = s_in[...]
        if causal:
            kpos = j * tk + lax.broadcasted_iota(jnp.int32, (1, tk), 1)
            s = jnp.where(kpos <= t_of_rows(0, rows), s, NEG_INF)
        m_old = m_ref[...]
        m_new = jnp.maximum(m_old, jnp.max(s, axis=-1, keepdims=True))
        alpha = jnp.exp(m_old - m_new)
        p = jnp.exp(s - m_new)
        l_ref[...] = alpha * l_ref[...] + jnp.sum(p, axis=-1, keepdims=True)
        m_ref[...] = m_new
        al_out[...] = alpha
        p_out[...] = p.astype(BF16)

    def values(j, p_in, al_in):
        acc_ref[...] = al_in[...] * acc_ref[...] + _dot(p_in[...], v_ref[pl.ds(tile_start(j), tk), :])

    m_ref[...] = jnp.full(m_ref.shape, NEG_INF, F32)
    l_ref[...] = jnp.zeros(l_ref.shape, F32)
    acc_ref[...] = jnp.zeros(acc_ref.shape, F32)
    p_b[...] = jnp.zeros(p_b.shape, BF16)
    al_b[...] = jnp.ones(al_b.shape, F32)
    scores(0, s_a)

    def two_full_tiles(i, carry):
        j = 2 * i
        values(j - 1, p_b, al_b)
        scores(j + 1, s_b)
        softmax(j, s_a, p_a, al_a, False)
        values(j, p_a, al_a)
        scores(j + 2, s_a)
        softmax(j + 1, s_b, p_b, al_b, False)
        return carry

    n_pairs = n_full // 2
    lax.fori_loop(0, n_pairs, two_full_tiles, 0)
    j = 2 * n_pairs
    odd = (n_full & 1) == 1
    values(j - 1, p_b, al_b)

    @pl.when(odd)
    def _():
        scores(j + 1, s_b)

    softmax(j, s_a, p_a, al_a, True)
    values(j, p_a, al_a)

    @pl.when(odd)
    def _():
        softmax(j + 1, s_b, p_b, al_b, True)
        values(j + 1, p_b, al_b)

    return acc_ref[...] / l_ref[...]


def _nsa_sel_kernel(q_ref, bias_ref, ks_ref, vs_ref, e_ref, gate_ref, oc_ref, ow_ref, sz_ref, out_ref,
                    lhs_ref, *scratch):
    tq = q_ref.shape[0]
    t0 = pl.program_id(0) * tq
    lane = lax.broadcasted_iota(jnp.int32, (tq, LANES), 1)
    q = q_ref[...]
    for half in range(lhs_ref.shape[0]):
        for h in range(NSA_HEADS):
            b = bias_ref[h // NSA_GROUP_HEADS, :, half * LANES:(half + 1) * LANES]
            lhs_ref[half, h * tq:(h + 1) * tq, :] = jnp.concatenate([_nsa_head_query(q, h, lane), b], axis=1)

    def t_of_rows(r0, n):
        return t0 + ((r0 + lax.broadcasted_iota(jnp.int32, (n, 1), 0)) & (tq - 1))

    o = _flash_body(lhs_ref, ks_ref, e_ref, vs_ref, t_of_rows, t0 // K_TILE, NSA_SEL_LEN * LANES, scratch)
    gates = gate_ref[...]
    outs = [o[h * tq:(h + 1) * tq] * gates[:, NSA_HEADS + h:NSA_HEADS + h + 1] for h in range(NSA_HEADS)]
    for j in range(4):
        cs = slice(j * LANES, (j + 1) * LANES)
        a = jnp.where(lane < HEAD_DIM, outs[j], outs[4 + j]) + oc_ref[:, cs] + ow_ref[:, cs]
        out_ref[:, cs] = (a * sz_ref[:, cs].astype(F32)).astype(out_ref.dtype)


def _nsa_sel(nq, bias, ks, vs, e_sel, gates, oc, ow, sz):
    T = nq.shape[0]
    n_selp = bias.shape[2]
    rows = NSA_HEADS * Q_TILE
    row = lambda w: pl.BlockSpec((Q_TILE, w), lambda qi: (qi, 0))
    full = pl.BlockSpec((T, LANES), lambda qi: (0, 0))
    return pl.pallas_call(
        _nsa_sel_kernel,
        grid=(T // Q_TILE,),
        in_specs=[row(512), pl.BlockSpec((2, Q_TILE, n_selp), lambda qi: (0, qi, 0)), full, full, full,
                  row(LANES), row(512), row(512), row(512)],
        out_specs=row(512),
        out_shape=jax.ShapeDtypeStruct((T, 512), BF16),
        scratch_shapes=[pltpu.VMEM((n_selp // LANES, rows, 2 * LANES), BF16)] + _flash_scratch(rows),
        compiler_params=_params("arbitrary"),
        name="nsa_selected",
    )(nq, bias, ks, vs, e_sel, gates, oc, ow, sz)


MOBA_Q_TILE = 512


def _moba_kernel(q_ref, kmean_ref, k_ref, v_ref, e_ref, sz_ref, out_ref, lhs_ref, *scratch, top):
    tq = q_ref.shape[0]
    t0 = pl.program_id(1) * tq
    lane = lax.broadcasted_iota(jnp.int32, (tq, LANES), 1)
    colv = q_ref[...]
    zero = jnp.zeros_like(colv)
    qp = jnp.concatenate([jnp.where(lane < HEAD_DIM, colv, zero), jnp.where(lane >= HEAD_DIM, colv, zero)], axis=0)

    def t_of_rows(r0, n):
        return t0 + ((r0 + lax.broadcasted_iota(jnp.int32, (n, 1), 0)) & (tq - 1))

    cur = t_of_rows(0, 2 * tq) >> 8
    blk = lax.broadcasted_iota(jnp.int32, (2 * tq, LANES), 1)
    past = blk < cur
    gate = jnp.where(past, _dot_nt(qp, kmean_ref[...]), NEG_INF)
    sel = _top_k_flags(gate, blk, top, LANES)
    bias = jnp.where((sel & past) | (blk == cur), 0.0, MASK_BIAS).astype(BF16)
    lhs_ref[0] = jnp.concatenate([qp, bias], axis=1)
    o = _flash_body(lhs_ref, k_ref, e_ref, v_ref, t_of_rows, t0 // K_TILE, MOBA_BLOCK * LANES, scratch)
    a = jnp.where(lane < HEAD_DIM, o[:tq], o[tq:])
    out_ref[...] = (a * sz_ref[...].astype(F32)).astype(out_ref.dtype)


def _moba(mq, kmean, mk, mv, e_blk, sz, top):
    T = mq.shape[0]
    tq = MOBA_Q_TILE
    assert MOBA_BLOCK == 256 and T * 1 <= MOBA_BLOCK * LANES
    rows = 2 * tq
    row = pl.BlockSpec((tq, LANES), lambda p, qi: (qi, p))
    col = pl.BlockSpec((T, LANES), lambda p, qi: (0, p))
    return pl.pallas_call(
        functools.partial(_moba_kernel, top=top),
        grid=(MOBA_HEADS // 2, T // tq),
        in_specs=[row, pl.BlockSpec((LANES, LANES), lambda p, qi: (0, p)), col, col,
                  pl.BlockSpec((T, LANES), lambda p, qi: (0, 0)), row],
        out_specs=row,
        out_shape=jax.ShapeDtypeStruct((T, 512), BF16),
        scratch_shapes=[pltpu.VMEM((1, rows, 2 * LANES), BF16)] + _flash_scratch(rows),
        compiler_params=_params("arbitrary", "arbitrary"),
        name="moba",
    )(mq, kmean, mk, mv, e_blk, sz)


def _mem_kv_kernel(mem_ref, g_ref, w0_ref, w1_ref, kv0_ref, kv1_ref):
    mem_n = _rms(mem_ref[...], g_ref[...]).astype(BF16)
    kv0_ref[...] = _dot(mem_n, w0_ref[...]).astype(kv0_ref.dtype)
    kv1_ref[...] = _dot(mem_n, w1_ref[...]).astype(kv1_ref.dtype)


def _mem_kv(mem, g, w0, w1):
    n = mem.shape[0]
    kv_w = 2 * MEM_HEADS * HEAD_DIM
    return pl.pallas_call(
        _mem_kv_kernel,
        grid=(1,),
        in_specs=[
            pl.BlockSpec((n, D_MODEL), lambda i: (0, 0)),
            pl.BlockSpec((1, D_MODEL), lambda i: (0, 0)),
            pl.BlockSpec((D_MODEL, kv_w), lambda i: (0, 0)),
            pl.BlockSpec((D_MODEL, kv_w), lambda i: (0, 0)),
        ],
        out_specs=[pl.BlockSpec((n, kv_w), lambda i: (0, 0)), pl.BlockSpec((n, kv_w), lambda i: (0, 0))],
        out_shape=[jax.ShapeDtypeStruct((n, kv_w), BF16), jax.ShapeDtypeStruct((n, kv_w), BF16)],
        compiler_params=_params("arbitrary"),
        name="mem_kv",
    )(mem, g, w0, w1)


def _mem_attention(eq, kv, sez, lane):
    rows = eq.shape[0]
    mem_w = MEM_HEADS * HEAD_DIM
    cols = []
    for p in range(MEM_HEADS // 2):
        cs = slice(p * LANES, (p + 1) * LANES)
        colv = eq[:, cs]
        zero = jnp.zeros_like(colv)
        qp = jnp.concatenate([jnp.where(lane < HEAD_DIM, colv, zero), jnp.where(lane >= HEAD_DIM, colv, zero)], axis=0)
        s = _dot_nt(qp, kv[:, cs])
        pe = jnp.exp(s - jnp.max(s, axis=-1, keepdims=True))
        o = _dot(pe.astype(BF16), kv[:, mem_w + p * LANES:mem_w + (p + 1) * LANES]) / jnp.sum(pe, axis=-1, keepdims=True)
        cols.append(jnp.where(lane < HEAD_DIM, o[:rows], o[rows:]) * sez[:, cs].astype(F32))
    return jnp.concatenate(cols, axis=1).astype(BF16)


def _out0_kernel(ag_ref, bg_ref, eq_ref, sez_ref, kv_ref, wa_ref, wb_ref, wm_ref, x_ref, out_ref):
    rows = x_ref.shape[0]
    lane = lax.broadcasted_iota(jnp.int32, (rows, LANES), 1)
    mg = _mem_attention(eq_ref[...], kv_ref[...], sez_ref[...], lane)
    y = _dot(ag_ref[...], wa_ref[...]) + _dot(bg_ref[...], wb_ref[...]) + _dot(mg, wm_ref[...])
    out_ref[...] = x_ref[...] + y


def _out0(ag, bg, eq, sez, kv, wa, wb, wm, x):
    T = x.shape[0]
    rt = PROJ_ROWS
    row = lambda w: pl.BlockSpec((rt, w), lambda i: (i, 0))
    full = lambda a: pl.BlockSpec(a.shape, lambda i: (0, 0))
    return pl.pallas_call(
        _out0_kernel,
        grid=(T // rt,),
        in_specs=[row(512), row(512), row(256), row(256), full(kv), full(wa), full(wb), full(wm), row(D_MODEL)],
        out_specs=row(D_MODEL),
        out_shape=jax.ShapeDtypeStruct((T, D_MODEL), F32),
        compiler_params=_params("arbitrary"),
        name="out_proj0",
    )(ag, bg, eq, sez, kv, wa, wb, wm, x)


def _out1_kernel(rg_ref, eq_ref, sez_ref, kv_ref, wr_ref, wm_ref, x_ref, g_ref, out_ref):
    rows = x_ref.shape[0]
    lane = lax.broadcasted_iota(jnp.int32, (rows, LANES), 1)
    mg = _mem_attention(eq_ref[...], kv_ref[...], sez_ref[...], lane)
    y = _dot(rg_ref[...], wr_ref[...]) + _dot(mg, wm_ref[...])
    out_ref[...] = _rms(x_ref[...] + y, g_ref[...])


def _out1(rg, eq, sez, kv, wr, wm, x, g):
    T = x.shape[0]
    rt = PROJ_ROWS
    row = lambda w: pl.BlockSpec((rt, w), lambda i: (i, 0))
    full = lambda a: pl.BlockSpec(a.shape, lambda i: (0, 0))
    return pl.pallas_call(
        _out1_kernel,
        grid=(T // rt,),
        in_specs=[row(RET_HEADS * RET_V_DIM), row(256), row(256), full(kv), full(wr), full(wm), row(D_MODEL), full(g)],
        out_specs=row(D_MODEL),
        out_shape=jax.ShapeDtypeStruct((T, D_MODEL), F32),
        compiler_params=_params("arbitrary"),
        name="out_proj1_norm",
    )(rg, eq, sez, kv, wr, wm, x, g)


def _retention_kernel(q_ref, k_ref, v_ref, sz_ref, decay_ref, qd_ref, kd_ref, cd_ref, out_ref, state_ref):
    @pl.when(pl.program_id(0) == 0)
    def _():
        state_ref[...] = jnp.zeros(state_ref.shape, F32)

    for h in range(RET_HEADS):
        qs = slice(h * RET_QK_DIM, (h + 1) * RET_QK_DIM)
        vsl = slice(h * RET_V_DIM, (h + 1) * RET_V_DIM)
        qh = q_ref[:, qs]
        kh = k_ref[:, qs]
        vh = v_ref[:, vsl]
        st = state_ref[h]
        inner = _dot_nt(qh, kh.astype(BF16)) * decay_ref[h]
        o = _dot(inner.astype(BF16), vh) + _dot(qh, st.astype(BF16)) * qd_ref[h]
        state_ref[h] = st * cd_ref[h] + _dot_tn((kh * kd_ref[h]).astype(BF16), vh)
        mu = jnp.mean(o, axis=-1, keepdims=True)
        d = o - mu
        on = d * lax.rsqrt(jnp.mean(d * d, axis=-1, keepdims=True) + NORM_EPS)
        out_ref[:, vsl] = (on * sz_ref[:, vsl].astype(F32)).astype(out_ref.dtype)


def _retention(rq, rk, rv, sz):
    T = rq.shape[0]
    C = RET_CHUNK
    log_g = jnp.log(1.0 - 2.0 ** (-5.0 - jnp.arange(RET_HEADS, dtype=F32)))
    i = jnp.arange(C, dtype=F32)
    diff = i[:, None] - i[None, :]
    decay = jnp.where(diff >= 0, jnp.exp(jnp.maximum(diff, 0.0)[None] * log_g[:, None, None]), 0.0)
    q_decay = jnp.exp((i + 1.0)[None, :] * log_g[:, None])[:, :, None]
    k_decay = jnp.exp((C - 1.0 - i)[None, :] * log_g[:, None])[:, :, None]
    chunk_decay = jnp.exp(C * log_g)[:, None, None]
    qk_w = RET_HEADS * RET_QK_DIM
    v_w = RET_HEADS * RET_V_DIM
    return pl.pallas_call(
        _retention_kernel,
        grid=(T // C,),
        in_specs=[
            pl.BlockSpec((C, qk_w), lambda c: (c, 0)),
            pl.BlockSpec((C, qk_w), lambda c: (c, 0)),
            pl.BlockSpec((C, v_w), lambda c: (c, 0)),
            pl.BlockSpec((C, v_w), lambda c: (c, 0)),
            pl.BlockSpec((RET_HEADS, C, C), lambda c: (0, 0, 0)),
            pl.BlockSpec((RET_HEADS, C, 1), lambda c: (0, 0, 0)),
            pl.BlockSpec((RET_HEADS, C, 1), lambda c: (0, 0, 0)),
            pl.BlockSpec((RET_HEADS, 1, 1), lambda c: (0, 0, 0)),
        ],
        out_specs=pl.BlockSpec((C, v_w), lambda c: (c, 0)),
        out_shape=jax.ShapeDtypeStruct((T, v_w), BF16),
        scratch_shapes=[pltpu.VMEM((RET_HEADS, RET_QK_DIM, RET_V_DIM), F32)],
        compiler_params=_params("arbitrary"),
        name="retention",
    )(rq, rk, rv, sz, decay, q_decay, k_decay, chunk_decay)


def _head_cols(order):
    return np.concatenate([np.arange(h * HEAD_DIM, (h + 1) * HEAD_DIM) for h in order])


def _one_hot_blocks(T, block):
    b = (np.arange(T) // block) % LANES
    return jnp.asarray(np.eye(LANES, dtype=np.float32)[b], dtype=BF16)


def kernel(x, mem, positions, l0_norm_g, l0_w_in, l0_nsa_gate_b, l0_cmp_pe_k, l0_cmp_w1_k, l0_cmp_w2_k,
           l0_cmp_pe_v, l0_cmp_w1_v, l0_cmp_w2_v, l0_w_mem_kv, l0_w_out,
           l1_norm_g, l1_w_in, l1_w_mem_kv, l1_w_out, mem_norm_g, final_norm_g):
    B, T, _ = x.shape
    assert B == 1 and T % K_TILE == 0 and T >= NSA_WINDOW + Q_TILE
    n_sel = T // NSA_SEL_LEN
    n_selp = -(-n_sel // LANES) * LANES
    k_sel = min(NSA_SEL_TOPK, n_sel)
    n_blk = T // MOBA_BLOCK
    assert n_blk <= LANES
    moba_top = max(1, min(MOBA_TOPK, n_blk - 1))
    x2 = x[0]
    pos = positions[0].astype(F32)

    attn_inv = 1.0 / (ROPE_THETA ** (jnp.arange(0, HEAD_DIM, 2, dtype=F32) / HEAD_DIM))
    ret_inv = 1.0 / (ROPE_THETA ** jnp.linspace(0.0, 1.0, RET_QK_DIM // 2, dtype=F32))
    ang = pos[:, None] * attn_inv
    cos64 = jnp.tile(jnp.cos(ang), (1, 4))
    sin64 = jnp.tile(jnp.concatenate([-jnp.sin(ang), jnp.sin(ang)], axis=1), (1, 2))
    rang = pos[:, None] * ret_inv
    cos256, sin256 = jnp.cos(rang), jnp.sin(rang)

    offs = np.cumsum([0, 512, 128, 128, 128, 128, 128, 128, 24, 512, 512, 512, 512, 512, 256, 256])
    nsa_cols = _head_cols(NSA_HEAD_ORDER)
    col_idx = np.concatenate([
        offs[0] + nsa_cols,
        np.arange(offs[1], offs[7]),
    ])
    w0 = jnp.concatenate([
        l0_w_in[:, col_idx],
        l0_w_in[:, offs[7]:offs[8]], jnp.zeros((D_MODEL, LANES - 24), F32),
        l0_w_in[:, offs[8] + nsa_cols],
        l0_w_in[:, offs[9]:],
    ], axis=1).astype(BF16)
    gate_b = jnp.concatenate([l0_nsa_gate_b, jnp.zeros((LANES - 24,), F32)])[None, :]
    groups0 = [
        (512, "rot64", 0.125), (128, "rot64", 1.0), (128, "plain", 1.0), (128, "rot64", 1.0), (128, "plain", 1.0),
        (128, "rot64", 1.0), (128, "plain", 1.0), (128, "gate", 1.0), (512, "silu", 1.0),
        (512, "rot64", 0.125), (512, "rot64_mean", 1.0), (512, "plain", 1.0), (512, "silu", 1.0),
        (256, "plain", 0.125), (256, "silu", 1.0),
    ]
    dts0 = [BF16] * 7 + [F32] + [BF16] * 7
    (nq, nkc, nvc, nks, nvs, nkw, nvw, gates, snz, mq, mk, kmean, mv, smz, eq0, sez0) = _proj(
        x2, l0_norm_g[None, :], w0, cos64, sin64, gate_b, groups0, dts0, "proj0")

    kv0, kv1 = _mem_kv(mem[0], mem_norm_g[None, :], l0_w_mem_kv.astype(BF16), l1_w_mem_kv.astype(BF16))

    kc = _compress(nkc, l0_cmp_pe_k, l0_cmp_w1_k, l0_cmp_w2_k, "compress_k")
    vc = _compress(nvc, l0_cmp_pe_v, l0_cmp_w1_v, l0_cmp_w2_v, "compress_v")
    n_cmp_rows = T // NSA_CMP_STRIDE
    cmp_start = np.arange(n_cmp_rows)[:, None] * NSA_CMP_STRIDE
    sel_start = np.arange(n_selp)[None, :] * NSA_SEL_LEN
    overlap = ((cmp_start < sel_start + NSA_SEL_LEN) & (cmp_start + NSA_CMP_LEN > sel_start)
               & (np.arange(n_selp)[None, :] < n_sel))
    ov = jnp.asarray(overlap.astype(np.float32), dtype=BF16)
    oc, sel_bias = _nsa_cmp(nq, kc, vc, ov, gates, k_sel)
    ow = _nsa_win(nq, nkw, nvw, gates)
    ag = _nsa_sel(nq, sel_bias, nks, nvs, _one_hot_blocks(T, NSA_SEL_LEN), gates, oc, ow, snz)

    kmean_p = jnp.concatenate([kmean[:, 0, :], jnp.zeros((LANES - n_blk, 512), F32)], axis=0).astype(BF16)
    bg = _moba(mq, kmean_p, mk, mv, _one_hot_blocks(T, MOBA_BLOCK), smz, moba_top)

    wa = l0_w_out[:512][nsa_cols].astype(BF16)
    wb = l0_w_out[512:1024].astype(BF16)
    wm0 = l0_w_out[1024:].astype(BF16)
    x1 = _out0(ag, bg, eq0, sez0, kv0, wa, wb, wm0, x2)

    groups1 = [
        (1024, "rot256", 1.0), (1024, "rot256", RET_QK_DIM ** -0.5), (2048, "plain", 1.0), (2048, "silu", 1.0),
        (256, "plain", 0.125), (256, "silu", 1.0),
    ]
    dts1 = [BF16, F32, BF16, BF16, BF16, BF16]
    rq, rk, rv, srz, eq1, sez1 = _proj(x1, l1_norm_g[None, :], l1_w_in.astype(BF16), cos256, sin256,
                                       jnp.zeros((1, LANES), F32), groups1, dts1, "proj1")
    rg = _retention(rq, rk, rv, srz)
    v_w = RET_HEADS * RET_V_DIM
    out = _out1(rg, eq1, sez1, kv1, l1_w_out[:v_w].astype(BF16), l1_w_out[v_w:].astype(BF16), x1,
                final_norm_g[None, :])
    return out[None]
```

```python
import functools

import numpy as np
import jax
import jax.numpy as jnp
from jax import lax
from jax.experimental import pallas as pl
from jax.experimental.pallas import tpu as pltpu

F32 = jnp.float32
BF16 = jnp.bfloat16

D_MODEL = 1024
HEAD_DIM = 64
ROPE_THETA = 10000.0
NORM_EPS = 1e-6
NEG_INF = -1e30
FORCE_SELECT = 1e9
MASK_BIAS = -1e9
LOG2_SCALE = HEAD_DIM ** -0.5 * float(np.log2(np.e))

NSA_HEADS = 8
NSA_GROUP_HEADS = 4
NSA_CMP_LEN = 32
NSA_CMP_STRIDE = 16
NSA_SEL_LEN = 64
NSA_SEL_TOPK = 16
NSA_WINDOW = 512
MOBA_HEADS = 8
MOBA_BLOCK = 256
MOBA_TOPK = 3
MEM_LEN = 256
MEM_HEADS = 4
RET_HEADS = 4
RET_QK_DIM = 256
RET_V_DIM = 512
RET_CHUNK = 128

LANES = 128
PROJ_ROWS = 256
PROJ_COLS = 512
Q_TILE = 128
K_TILE = 512
VMEM_LIMIT = 56 * 1024 * 1024

NSA_HEAD_ORDER = (0, 4, 1, 5, 2, 6, 3, 7)


def _dot(a, b):
    return jnp.dot(a, b, preferred_element_type=F32)


def _dot_nt(a, b):
    return lax.dot_general(a, b, (((1,), (1,)), ((), ())), preferred_element_type=F32)


def _dot_tn(a, b):
    return lax.dot_general(a, b, (((0,), (0,)), ((), ())), preferred_element_type=F32)


def _silu(y):
    return y * (1.0 / (1.0 + jnp.exp(-y)))


def _rms(x, g):
    return x * lax.rsqrt(jnp.mean(x * x, axis=-1, keepdims=True) + NORM_EPS) * g


def _params(*sem):
    return pltpu.CompilerParams(dimension_semantics=sem, vmem_limit_bytes=VMEM_LIMIT)


def _proj_kernel(x_ref, g_ref, w_ref, c_ref, s_ref, b_ref, *out_refs, groups):
    h = _rms(x_ref[...], g_ref[...]).astype(BF16)
    rows = x_ref.shape[0]
    lane = lax.broadcasted_iota(jnp.int32, (rows, LANES), 1)
    low_half = (lane & 32) == 0
    col = 0
    oi = 0
    for width, kind, scale in groups:
        out_ref = out_refs[oi]
        oi += 1
        mean_ref = None
        if kind == "rot64_mean":
            mean_ref = out_refs[oi]
            oi += 1
        for c0 in range(0, width, PROJ_COLS):
            cw = min(PROJ_COLS, width - c0)
            y = _dot(h, w_ref[:, col + c0:col + c0 + cw])
            if kind in ("rot64", "rot64_mean"):
                c = c_ref[...]
                s = s_ref[...]
                pieces = []
                for j in range(cw // LANES):
                    p = y[:, j * LANES:(j + 1) * LANES]
                    partner = jnp.where(low_half, pltpu.roll(p, LANES - 32, 1), pltpu.roll(p, 32, 1))
                    pieces.append(p * c + partner * s)
                y = pieces[0] if len(pieces) == 1 else jnp.concatenate(pieces, axis=1)
            elif kind == "rot256":
                c = c_ref[...]
                s = s_ref[...]
                pieces = []
                for j in range(cw // RET_QK_DIM):
                    x1 = y[:, j * RET_QK_DIM:j * RET_QK_DIM + LANES]
                    x2 = y[:, j * RET_QK_DIM + LANES:(j + 1) * RET_QK_DIM]
                    pieces.append(x1 * c - x2 * s)
                    pieces.append(x1 * s + x2 * c)
                y = jnp.concatenate(pieces, axis=1)
            elif kind == "silu":
                y = _silu(y)
            elif kind == "gate":
                y = 1.0 / (1.0 + jnp.exp(-(y + b_ref[...])))
            if scale != 1.0:
                y = y * scale
            if mean_ref is not None:
                mean_ref[0, :, c0:c0 + cw] = jnp.mean(y, axis=0, keepdims=True)
            out_ref[:, c0:c0 + cw] = y.astype(out_ref.dtype)
        col += width


def _proj(x, g, w, cos_t, sin_t, bias, groups, out_dtypes, name):
    T = x.shape[0]
    n_tiles = T // PROJ_ROWS
    out_shapes = []
    out_specs = []
    for (width, kind, _), dt in zip(groups, out_dtypes):
        out_shapes.append(jax.ShapeDtypeStruct((T, width), dt))
        out_specs.append(pl.BlockSpec((PROJ_ROWS, width), lambda i: (i, 0)))
        if kind == "rot64_mean":
            out_shapes.append(jax.ShapeDtypeStruct((n_tiles, 1, width), F32))
            out_specs.append(pl.BlockSpec((1, 1, width), lambda i: (i, 0, 0)))
    return pl.pallas_call(
        functools.partial(_proj_kernel, groups=tuple(groups)),
        grid=(n_tiles,),
        in_specs=[
            pl.BlockSpec((PROJ_ROWS, D_MODEL), lambda i: (i, 0)),
            pl.BlockSpec((1, D_MODEL), lambda i: (0, 0)),
            pl.BlockSpec(w.shape, lambda i: (0, 0)),
            pl.BlockSpec((PROJ_ROWS, LANES), lambda i: (i, 0)),
            pl.BlockSpec((PROJ_ROWS, LANES), lambda i: (i, 0)),
            pl.BlockSpec((1, LANES), lambda i: (0, 0)),
        ],
        out_specs=out_specs,
        out_shape=out_shapes,
        compiler_params=_params("arbitrary"),
        name=name,
    )(x, g, w, cos_t, sin_t, bias)


def _compress_kernel(a_ref, an_ref, pea_ref, peb_ref, w1a_ref, w1b_ref, w2_ref, out_ref):
    xa = (a_ref[...].astype(F32) + pea_ref[...]).astype(BF16)
    xb = (an_ref[...].astype(F32) + peb_ref[...]).astype(BF16)
    hid = _silu(_dot(xa, w1a_ref[...]) + _dot(xb, w1b_ref[...]))
    out_ref[...] = _dot(hid.astype(BF16), w2_ref[...]).astype(out_ref.dtype)


def _compress(a, pe, w1, w2, name):
    T = a.shape[0]
    n_rows = T // NSA_CMP_STRIDE
    row_w = NSA_CMP_STRIDE * LANES
    a2 = a.reshape(n_rows, row_w)
    a2n = jnp.concatenate([a2[1:], jnp.zeros((1, row_w), a2.dtype)], axis=0)
    eye2 = jnp.eye(2, dtype=F32)

    def expand(w1_half):
        return jnp.einsum("lde,gh->lgdhe", w1_half, eye2).reshape(row_w, LANES).astype(BF16)

    def pe_row(pe_half):
        return jnp.broadcast_to(pe_half[:, None, :], (NSA_CMP_STRIDE, 2, HEAD_DIM)).reshape(1, row_w)

    w2bd = jnp.einsum("ef,gh->gehf", w2, eye2).reshape(LANES, LANES).astype(BF16)
    tile = min(256, n_rows)
    return pl.pallas_call(
        _compress_kernel,
        grid=(n_rows // tile,),
        in_specs=[
            pl.BlockSpec((tile, row_w), lambda i: (i, 0)),
            pl.BlockSpec((tile, row_w), lambda i: (i, 0)),
            pl.BlockSpec((1, row_w), lambda i: (0, 0)),
            pl.BlockSpec((1, row_w), lambda i: (0, 0)),
            pl.BlockSpec((row_w, LANES), lambda i: (0, 0)),
            pl.BlockSpec((row_w, LANES), lambda i: (0, 0)),
            pl.BlockSpec((LANES, LANES), lambda i: (0, 0)),
        ],
        out_specs=pl.BlockSpec((tile, LANES), lambda i: (i, 0)),
        out_shape=jax.ShapeDtypeStruct((n_rows, LANES), BF16),
        compiler_params=_params("arbitrary"),
        name=name,
    )(a2, a2n, pe_row(pe[:NSA_CMP_STRIDE]), pe_row(pe[NSA_CMP_STRIDE:]),
      expand(w1[:NSA_CMP_STRIDE]), expand(w1[NSA_CMP_STRIDE:]), w2bd)


def _nsa_head_query(q, h, lane):
    colv = q[:, (h % 4) * LANES:(h % 4 + 1) * LANES]
    keep = (lane < HEAD_DIM) if h < NSA_GROUP_HEADS else (lane >= HEAD_DIM)
    return jnp.where(keep, colv, jnp.zeros_like(colv))


def _masked_softmax_parts(s, mask):
    s = jnp.where(mask, s, NEG_INF)
    m = jnp.max(s, axis=-1, keepdims=True)
    p = jnp.where(mask, jnp.exp2(s - m), 0.0)
    l = jnp.sum(p, axis=-1, keepdims=True)
    return p, 1.0 / jnp.maximum(l, 1e-30)


def _top_k_flags(x, idx, k):
    sel = jnp.zeros(x.shape, jnp.bool_)
    for _ in range(k):
        mx = jnp.max(x, axis=0, keepdims=True)
        first = jnp.min(jnp.where(x == mx, idx, x.shape[0]), axis=0, keepdims=True)
        hit = idx == first
        sel = sel | hit
        x = jnp.where(hit, -jnp.inf, x)
    return sel


def _nsa_cmp_kernel(q_ref, kc_ref, vc_ref, ovt_ref, gate_ref, oc_ref, bias_ref, *, k_sel):
    tq = q_ref.shape[0]
    n_cmp = kc_ref.shape[0]
    n_selp = ovt_ref.shape[0]
    t0 = pl.program_id(0) * tq
    lane = lax.broadcasted_iota(jnp.int32, (tq, LANES), 1)
    t_col = t0 + lax.broadcasted_iota(jnp.int32, (tq, 1), 0)
    cmp_end = lax.broadcasted_iota(jnp.int32, (1, n_cmp), 1) * NSA_CMP_STRIDE + (NSA_CMP_LEN - 1)
    mask = cmp_end <= t_col
    q = q_ref[...]
    kc = kc_ref[...]
    vc = vc_ref[...]
    gates = gate_ref[...]
    pg = [jnp.zeros((tq, n_cmp), F32), jnp.zeros((tq, n_cmp), F32)]
    outs = []
    for h in range(NSA_HEADS):
        p, rl = _masked_softmax_parts(_dot_nt(_nsa_head_query(q, h, lane), kc), mask)
        outs.append(_dot(p.astype(BF16), vc) * (rl * gates[:, h:h + 1]))
        pg[h // NSA_GROUP_HEADS] = pg[h // NSA_GROUP_HEADS] + p * rl
    for j in range(4):
        oc_ref[:, j * LANES:(j + 1) * LANES] = jnp.where(lane < HEAD_DIM, outs[j], outs[4 + j])
    blk = lax.broadcasted_iota(jnp.int32, (n_selp, tq), 0)
    cur = (t0 + lax.broadcasted_iota(jnp.int32, (1, tq), 1)) >> 6
    valid = blk <= cur
    forced = (blk == 0) | (blk == cur) | (blk == cur - 1)
    for g in range(2):
        imp_t = _dot_nt(ovt_ref[...], pg[g].astype(BF16))
        x = jnp.where(forced, FORCE_SELECT, imp_t)
        x = jnp.where(valid, x, NEG_INF)
        sel = _top_k_flags(x, blk, k_sel)
        bias_ref[g] = jnp.where(sel & valid, 0.0, MASK_BIAS).T.astype(BF16)


def _nsa_cmp(nq, kc, vc, ov, gates, k_sel):
    T = nq.shape[0]
    n_cmp = kc.shape[0]
    n_selp = ov.shape[0]
    return pl.pallas_call(
        functools.partial(_nsa_cmp_kernel, k_sel=k_sel),
        grid=(T // Q_TILE,),
        in_specs=[
            pl.BlockSpec((Q_TILE, 512), lambda i: (i, 0)),
            pl.BlockSpec((n_cmp, LANES), lambda i: (0, 0)),
            pl.BlockSpec((n_cmp, LANES), lambda i: (0, 0)),
            pl.BlockSpec((n_selp, n_cmp), lambda i: (0, 0)),
            pl.BlockSpec((Q_TILE, LANES), lambda i: (i, 0)),
        ],
        out_specs=[
            pl.BlockSpec((Q_TILE, 512), lambda i: (i, 0)),
            pl.BlockSpec((2, Q_TILE, n_selp), lambda i: (0, i, 0)),
        ],
        out_shape=[
            jax.ShapeDtypeStruct((T, 512), F32),
            jax.ShapeDtypeStruct((2, T, n_selp), BF16),
        ],
        compiler_params=_params("arbitrary"),
        name="nsa_cmp_select",
    )(nq, kc, vc, ov, gates)


def _nsa_win_kernel(q_ref, kw_ref, vw_ref, gate_ref, ow_ref):
    tq = q_ref.shape[0]
    span = NSA_WINDOW + tq
    t0 = pl.program_id(0) * tq
    start = pl.multiple_of(jnp.maximum(t0 - NSA_WINDOW, 0), tq)
    lane = lax.broadcasted_iota(jnp.int32, (tq, LANES), 1)
    t_col = t0 + lax.broadcasted_iota(jnp.int32, (tq, 1), 0)
    wpos = start + lax.broadcasted_iota(jnp.int32, (1, span), 1)
    mask = (wpos <= t_col) & (wpos > t_col - NSA_WINDOW)
    q = q_ref[...]
    kw = kw_ref[pl.ds(start, span), :]
    vw = vw_ref[pl.ds(start, span), :]
    gates = gate_ref[...]
    outs = []
    for h in range(NSA_HEADS):
        p, rl = _masked_softmax_parts(_dot_nt(_nsa_head_query(q, h, lane), kw), mask)
        outs.append(_dot(p.astype(BF16), vw) * (rl * gates[:, 2 * NSA_HEADS + h:2 * NSA_HEADS + h + 1]))
    for j in range(4):
        ow_ref[:, j * LANES:(j + 1) * LANES] = jnp.where(lane < HEAD_DIM, outs[j], outs[4 + j])


def _nsa_win(nq, kw, vw, gates):
    T = nq.shape[0]
    return pl.pallas_call(
        _nsa_win_kernel,
        grid=(T // Q_TILE,),
        in_specs=[
            pl.BlockSpec((Q_TILE, 512), lambda i: (i, 0)),
            pl.BlockSpec((T, LANES), lambda i: (0, 0)),
            pl.BlockSpec((T, LANES), lambda i: (0, 0)),
            pl.BlockSpec((Q_TILE, LANES), lambda i: (i, 0)),
        ],
        out_specs=pl.BlockSpec((Q_TILE, 512), lambda i: (i, 0)),
        out_shape=jax.ShapeDtypeStruct((T, 512), F32),
        compiler_params=_params("arbitrary"),
        name="nsa_window",
    )(nq, kw, vw, gates)


def _flash_scratch(rows):
    return [
        pltpu.VMEM((rows, K_TILE), F32), pltpu.VMEM((rows, K_TILE), F32),
        pltpu.VMEM((rows, K_TILE), BF16), pltpu.VMEM((rows, K_TILE), BF16),
        pltpu.VMEM((rows, 1), F32), pltpu.VMEM((rows, 1), F32),
        pltpu.VMEM((rows, 1), F32),
        pltpu.VMEM((rows, LANES), F32),
    ]


def _flash_body(lhs_ref, k_ref, e_ref, va_ref, vb_ref, t_of_rows, n_full, tokens_per_half, scratch):
    s_a, s_b, p_a, p_b, al_a, al_b, m_ref, acc_ref = scratch
    tk = K_TILE
    rows = acc_ref.shape[0]
    half_rows = rows // 2
    last_mem_tile = k_ref.shape[0] // tk - 1

    def tile_start(j):
        return pl.multiple_of(jnp.clip(j, 0, last_mem_tile) * tk, tk)

    def scores(j, s_out):
        k0 = tile_start(j)
        rhs = jnp.concatenate([k_ref[pl.ds(k0, tk), :], e_ref[pl.ds(k0, tk), :]], axis=1)
        s_out[...] = _dot_nt(lhs_ref[k0 // tokens_per_half], rhs)

    def softmax(j, s_in, p_out, al_out, causal):
        s = s_in[...]
        if causal:
            kpos = j * tk + lax.broadcasted_iota(jnp.int32, (1, tk), 1)
            s = jnp.where(kpos <= t_of_rows(0, rows), s, NEG_INF)
        m_old = m_ref[...]
        m_new = jnp.maximum(m_old, jnp.max(s, axis=-1, keepdims=True))
        m_ref[...] = m_new
        al_out[...] = jnp.exp2(m_old - m_new)
        p_out[...] = jnp.exp2(s - m_new).astype(BF16)

    def values(j, p_in, al_in):
        k0 = tile_start(j)
        for rs, v_ref in ((slice(0, half_rows), va_ref), (slice(half_rows, rows), vb_ref)):
            acc_ref[rs, :] = al_in[rs, :] * acc_ref[rs, :] + _dot(p_in[rs, :], v_ref[pl.ds(k0, tk), :])

    m_ref[...] = jnp.full(m_ref.shape, NEG_INF, F32)
    acc_ref[...] = jnp.zeros(acc_ref.shape, F32)
    p_b[...] = jnp.zeros(p_b.shape, BF16)
    al_b[...] = jnp.ones(al_b.shape, F32)
    scores(0, s_a)

    def two_full_tiles(i, carry):
        j = 2 * i
        values(j - 1, p_b, al_b)
        scores(j + 1, s_b)
        softmax(j, s_a, p_a, al_a, False)
        values(j, p_a, al_a)
        scores(j + 2, s_a)
        softmax(j + 1, s_b, p_b, al_b, False)
        return carry

    n_pairs = n_full // 2
    lax.fori_loop(0, n_pairs, two_full_tiles, 0)
    j = 2 * n_pairs
    odd = (n_full & 1) == 1
    values(j - 1, p_b, al_b)

    @pl.when(odd)
    def _():
        scores(j + 1, s_b)

    softmax(j, s_a, p_a, al_a, True)
    values(j, p_a, al_a)

    @pl.when(odd)
    def _():
        softmax(j + 1, s_b, p_b, al_b, True)
        values(j + 1, p_b, al_b)

    acc = acc_ref[...]
    return acc / pltpu.roll(acc, HEAD_DIM, 1)


def _nsa_sel_kernel(q_ref, bias_ref, ks_ref, va_ref, vb_ref, e_ref, gate_ref, oc_ref, ow_ref, sz_ref, out_ref,
                    lhs_ref, *scratch):
    tq = q_ref.shape[0]
    t0 = pl.program_id(0) * tq
    lane = lax.broadcasted_iota(jnp.int32, (tq, LANES), 1)
    q = q_ref[...]
    for half in range(lhs_ref.shape[0]):
        for h in range(NSA_HEADS):
            b = bias_ref[h // NSA_GROUP_HEADS, :, half * LANES:(half + 1) * LANES]
            lhs_ref[half, h * tq:(h + 1) * tq, :] = jnp.concatenate([_nsa_head_query(q, h, lane), b], axis=1)

    def t_of_rows(r0, n):
        return t0 + ((r0 + lax.broadcasted_iota(jnp.int32, (n, 1), 0)) & (tq - 1))

    o = _flash_body(lhs_ref, ks_ref, e_ref, va_ref, vb_ref, t_of_rows, t0 // K_TILE, NSA_SEL_LEN * LANES, scratch)
    gates = gate_ref[...]
    outs = [o[h * tq:(h + 1) * tq] * gates[:, NSA_HEADS + h:NSA_HEADS + h + 1] for h in range(NSA_HEADS)]
    for j in range(4):
        cs = slice(j * LANES, (j + 1) * LANES)
        a = jnp.where(lane < HEAD_DIM, outs[j], outs[4 + j]) + oc_ref[:, cs] + ow_ref[:, cs]
        out_ref[:, cs] = (a * sz_ref[:, cs].astype(F32)).astype(out_ref.dtype)


def _nsa_sel(nq, bias, ks, vs, e_sel, gates, oc, ow, sz):
    T = nq.shape[0]
    n_selp = bias.shape[2]
    rows = NSA_HEADS * Q_TILE
    row = lambda w: pl.BlockSpec((Q_TILE, w), lambda qi: (qi, 0))
    full = pl.BlockSpec((T, LANES), lambda qi: (0, 0))
    ones = jnp.ones((T, HEAD_DIM), vs.dtype)
    va = jnp.concatenate([vs[:, :HEAD_DIM], ones], axis=1)
    vb = jnp.concatenate([ones, vs[:, HEAD_DIM:]], axis=1)
    return pl.pallas_call(
        _nsa_sel_kernel,
        grid=(T // Q_TILE,),
        in_specs=[row(512), pl.BlockSpec((2, Q_TILE, n_selp), lambda qi: (0, qi, 0)), full, full, full, full,
                  row(LANES), row(512), row(512), row(512)],
        out_specs=row(512),
        out_shape=jax.ShapeDtypeStruct((T, 512), BF16),
        scratch_shapes=[pltpu.VMEM((n_selp // LANES, rows, 2 * LANES), BF16)] + _flash_scratch(rows),
        compiler_params=_params("arbitrary"),
        name="nsa_selected",
    )(nq, bias, ks, va, vb, e_sel, gates, oc, ow, sz)


MOBA_Q_TILE = 512


def _moba_kernel(q_ref, kmean_ref, k_ref, va_ref, vb_ref, e_ref, sz_ref, out_ref, lhs_ref, *scratch, top):
    tq = q_ref.shape[0]
    t0 = pl.program_id(1) * tq
    lane = lax.broadcasted_iota(jnp.int32, (tq, LANES), 1)
    colv = q_ref[...]
    zero = jnp.zeros_like(colv)
    qp = jnp.concatenate([jnp.where(lane < HEAD_DIM, colv, zero), jnp.where(lane >= HEAD_DIM, colv, zero)], axis=0)

    def t_of_rows(r0, n):
        return t0 + ((r0 + lax.broadcasted_iota(jnp.int32, (n, 1), 0)) & (tq - 1))

    cur = (t0 + (lax.broadcasted_iota(jnp.int32, (1, 2 * tq), 1) & (tq - 1))) >> 8
    blk = lax.broadcasted_iota(jnp.int32, (LANES, 2 * tq), 0)
    past = blk < cur
    gate = jnp.where(past, _dot_nt(kmean_ref[...], qp), NEG_INF)
    sel = _top_k_flags(gate, blk, top)
    bias = jnp.where((sel & past) | (blk == cur), 0.0, MASK_BIAS).T.astype(BF16)
    lhs_ref[0] = jnp.concatenate([qp, bias], axis=1)
    o = _flash_body(lhs_ref, k_ref, e_ref, va_ref, vb_ref, t_of_rows, t0 // K_TILE, MOBA_BLOCK * LANES, scratch)
    a = jnp.where(lane < HEAD_DIM, o[:tq], o[tq:])
    out_ref[...] = (a * sz_ref[...].astype(F32)).astype(out_ref.dtype)


def _moba(mq, kmean, mk, mv, e_blk, sz, top):
    T = mq.shape[0]
    tq = MOBA_Q_TILE
    assert MOBA_BLOCK == 256 and T * 1 <= MOBA_BLOCK * LANES
    rows = 2 * tq
    row = pl.BlockSpec((tq, LANES), lambda p, qi: (qi, p))
    col = pl.BlockSpec((T, LANES), lambda p, qi: (0, p))
    v4 = mv.reshape(T, MOBA_HEADS // 2, 2, HEAD_DIM)
    ones = jnp.ones((T, MOBA_HEADS // 2, HEAD_DIM), mv.dtype)
    va = jnp.stack([v4[:, :, 0], ones], axis=2).reshape(T, 512)
    vb = jnp.stack([ones, v4[:, :, 1]], axis=2).reshape(T, 512)
    return pl.pallas_call(
        functools.partial(_moba_kernel, top=top),
        grid=(MOBA_HEADS // 2, T // tq),
        in_specs=[row, pl.BlockSpec((LANES, LANES), lambda p, qi: (0, p)), col, col, col,
                  pl.BlockSpec((T, LANES), lambda p, qi: (0, 0)), row],
        out_specs=row,
        out_shape=jax.ShapeDtypeStruct((T, 512), BF16),
        scratch_shapes=[pltpu.VMEM((1, rows, 2 * LANES), BF16)] + _flash_scratch(rows),
        compiler_params=_params("arbitrary", "arbitrary"),
        name="moba",
    )(mq, kmean, mk, va, vb, e_blk, sz)


def _mem_kv_kernel(mem_ref, g_ref, w0_ref, w1_ref, kv0_ref, kv1_ref):
    mem_n = _rms(mem_ref[...], g_ref[...]).astype(BF16)
    kv0_ref[...] = _dot(mem_n, w0_ref[...]).astype(kv0_ref.dtype)
    kv1_ref[...] = _dot(mem_n, w1_ref[...]).astype(kv1_ref.dtype)


def _mem_kv(mem, g, w0, w1):
    n = mem.shape[0]
    kv_w = 2 * MEM_HEADS * HEAD_DIM
    return pl.pallas_call(
        _mem_kv_kernel,
        grid=(1,),
        in_specs=[
            pl.BlockSpec((n, D_MODEL), lambda i: (0, 0)),
            pl.BlockSpec((1, D_MODEL), lambda i: (0, 0)),
            pl.BlockSpec((D_MODEL, kv_w), lambda i: (0, 0)),
            pl.BlockSpec((D_MODEL, kv_w), lambda i: (0, 0)),
        ],
        out_specs=[pl.BlockSpec((n, kv_w), lambda i: (0, 0)), pl.BlockSpec((n, kv_w), lambda i: (0, 0))],
        out_shape=[jax.ShapeDtypeStruct((n, kv_w), BF16), jax.ShapeDtypeStruct((n, kv_w), BF16)],
        compiler_params=_params("arbitrary"),
        name="mem_kv",
    )(mem, g, w0, w1)


def _mem_attention(eq, kv, sez, lane):
    rows = eq.shape[0]
    mem_w = MEM_HEADS * HEAD_DIM
    cols = []
    for p in range(MEM_HEADS // 2):
        cs = slice(p * LANES, (p + 1) * LANES)
        colv = eq[:, cs]
        zero = jnp.zeros_like(colv)
        qp = jnp.concatenate([jnp.where(lane < HEAD_DIM, colv, zero), jnp.where(lane >= HEAD_DIM, colv, zero)], axis=0)
        s = _dot_nt(qp, kv[:, cs])
        pe = jnp.exp(s - jnp.max(s, axis=-1, keepdims=True))
        o = _dot(pe.astype(BF16), kv[:, mem_w + p * LANES:mem_w + (p + 1) * LANES]) / jnp.sum(pe, axis=-1, keepdims=True)
        cols.append(jnp.where(lane < HEAD_DIM, o[:rows], o[rows:]) * sez[:, cs].astype(F32))
    return jnp.concatenate(cols, axis=1).astype(BF16)


def _out0_kernel(ag_ref, bg_ref, eq_ref, sez_ref, kv_ref, wa_ref, wb_ref, wm_ref, x_ref, out_ref):
    rows = x_ref.shape[0]
    lane = lax.broadcasted_iota(jnp.int32, (rows, LANES), 1)
    mg = _mem_attention(eq_ref[...], kv_ref[...], sez_ref[...], lane)
    y = _dot(ag_ref[...], wa_ref[...]) + _dot(bg_ref[...], wb_ref[...]) + _dot(mg, wm_ref[...])
    out_ref[...] = x_ref[...] + y


def _out0(ag, bg, eq, sez, kv, wa, wb, wm, x):
    T = x.shape[0]
    rt = PROJ_ROWS
    row = lambda w: pl.BlockSpec((rt, w), lambda i: (i, 0))
    full = lambda a: pl.BlockSpec(a.shape, lambda i: (0, 0))
    return pl.pallas_call(
        _out0_kernel,
        grid=(T // rt,),
        in_specs=[row(512), row(512), row(256), row(256), full(kv), full(wa), full(wb), full(wm), row(D_MODEL)],
        out_specs=row(D_MODEL),
        out_shape=jax.ShapeDtypeStruct((T, D_MODEL), F32),
        compiler_params=_params("arbitrary"),
        name="out_proj0",
    )(ag, bg, eq, sez, kv, wa, wb, wm, x)


def _out1_kernel(rg_ref, eq_ref, sez_ref, kv_ref, wr_ref, wm_ref, x_ref, g_ref, out_ref):
    rows = x_ref.shape[0]
    lane = lax.broadcasted_iota(jnp.int32, (rows, LANES), 1)
    mg = _mem_attention(eq_ref[...], kv_ref[...], sez_ref[...], lane)
    y = _dot(rg_ref[...], wr_ref[...]) + _dot(mg, wm_ref[...])
    out_ref[...] = _rms(x_ref[...] + y, g_ref[...])


def _out1(rg, eq, sez, kv, wr, wm, x, g):
    T = x.shape[0]
    rt = PROJ_ROWS
    row = lambda w: pl.BlockSpec((rt, w), lambda i: (i, 0))
    full = lambda a: pl.BlockSpec(a.shape, lambda i: (0, 0))
    return pl.pallas_call(
        _out1_kernel,
        grid=(T // rt,),
        in_specs=[row(RET_HEADS * RET_V_DIM), row(256), row(256), full(kv), full(wr), full(wm), row(D_MODEL), full(g)],
        out_specs=row(D_MODEL),
        out_shape=jax.ShapeDtypeStruct((T, D_MODEL), F32),
        compiler_params=_params("arbitrary"),
        name="out_proj1_norm",
    )(rg, eq, sez, kv, wr, wm, x, g)


def _retention_kernel(q_ref, k_ref, v_ref, sz_ref, decay_ref, qd_ref, kd_ref, cd_ref, out_ref, state_ref):
    @pl.when(pl.program_id(0) == 0)
    def _():
        state_ref[...] = jnp.zeros(state_ref.shape, F32)

    for h in range(RET_HEADS):
        qs = slice(h * RET_QK_DIM, (h + 1) * RET_QK_DIM)
        vsl = slice(h * RET_V_DIM, (h + 1) * RET_V_DIM)
        qh = q_ref[:, qs]
        kh = k_ref[:, qs]
        vh = v_ref[:, vsl]
        st = state_ref[h]
        inner = _dot_nt(qh, kh.astype(BF16)) * decay_ref[h]
        o = _dot(inner.astype(BF16), vh) + _dot(qh, st.astype(BF16)) * qd_ref[h]
        state_ref[h] = st * cd_ref[h] + _dot_tn((kh * kd_ref[h]).astype(BF16), vh)
        mu = jnp.mean(o, axis=-1, keepdims=True)
        d = o - mu
        on = d * lax.rsqrt(jnp.mean(d * d, axis=-1, keepdims=True) + NORM_EPS)
        out_ref[:, vsl] = (on * sz_ref[:, vsl].astype(F32)).astype(out_ref.dtype)


def _retention(rq, rk, rv, sz):
    T = rq.shape[0]
    C = RET_CHUNK
    log_g = jnp.log(1.0 - 2.0 ** (-5.0 - jnp.arange(RET_HEADS, dtype=F32)))
    i = jnp.arange(C, dtype=F32)
    diff = i[:, None] - i[None, :]
    decay = jnp.where(diff >= 0, jnp.exp(jnp.maximum(diff, 0.0)[None] * log_g[:, None, None]), 0.0)
    q_decay = jnp.exp((i + 1.0)[None, :] * log_g[:, None])[:, :, None]
    k_decay = jnp.exp((C - 1.0 - i)[None, :] * log_g[:, None])[:, :, None]
    chunk_decay = jnp.exp(C * log_g)[:, None, None]
    qk_w = RET_HEADS * RET_QK_DIM
    v_w = RET_HEADS * RET_V_DIM
    return pl.pallas_call(
        _retention_kernel,
        grid=(T // C,),
        in_specs=[
            pl.BlockSpec((C, qk_w), lambda c: (c, 0)),
            pl.BlockSpec((C, qk_w), lambda c: (c, 0)),
            pl.BlockSpec((C, v_w), lambda c: (c, 0)),
            pl.BlockSpec((C, v_w), lambda c: (c, 0)),
            pl.BlockSpec((RET_HEADS, C, C), lambda c: (0, 0, 0)),
            pl.BlockSpec((RET_HEADS, C, 1), lambda c: (0, 0, 0)),
            pl.BlockSpec((RET_HEADS, C, 1), lambda c: (0, 0, 0)),
            pl.BlockSpec((RET_HEADS, 1, 1), lambda c: (0, 0, 0)),
        ],
        out_specs=pl.BlockSpec((C, v_w), lambda c: (c, 0)),
        out_shape=jax.ShapeDtypeStruct((T, v_w), BF16),
        scratch_shapes=[pltpu.VMEM((RET_HEADS, RET_QK_DIM, RET_V_DIM), F32)],
        compiler_params=_params("arbitrary"),
        name="retention",
    )(rq, rk, rv, sz, decay, q_decay, k_decay, chunk_decay)


def _head_cols(order):
    return np.concatenate([np.arange(h * HEAD_DIM, (h + 1) * HEAD_DIM) for h in order])


def _one_hot_blocks(T, block):
    b = (np.arange(T) // block) % LANES
    return jnp.asarray(np.eye(LANES, dtype=np.float32)[b], dtype=BF16)


def kernel(x, mem, positions, l0_norm_g, l0_w_in, l0_nsa_gate_b, l0_cmp_pe_k, l0_cmp_w1_k, l0_cmp_w2_k,
           l0_cmp_pe_v, l0_cmp_w1_v, l0_cmp_w2_v, l0_w_mem_kv, l0_w_out,
           l1_norm_g, l1_w_in, l1_w_mem_kv, l1_w_out, mem_norm_g, final_norm_g):
    B, T, _ = x.shape
    assert B == 1 and T % K_TILE == 0 and T >= NSA_WINDOW + Q_TILE
    n_sel = T // NSA_SEL_LEN
    n_selp = -(-n_sel // LANES) * LANES
    k_sel = min(NSA_SEL_TOPK, n_sel)
    n_blk = T // MOBA_BLOCK
    assert n_blk <= LANES
    moba_top = max(1, min(MOBA_TOPK, n_blk - 1))
    x2 = x[0]
    pos = positions[0].astype(F32)

    attn_inv = 1.0 / (ROPE_THETA ** (jnp.arange(0, HEAD_DIM, 2, dtype=F32) / HEAD_DIM))
    ret_inv = 1.0 / (ROPE_THETA ** jnp.linspace(0.0, 1.0, RET_QK_DIM // 2, dtype=F32))
    ang = pos[:, None] * attn_inv
    cos64 = jnp.tile(jnp.cos(ang), (1, 4))
    sin64 = jnp.tile(jnp.concatenate([-jnp.sin(ang), jnp.sin(ang)], axis=1), (1, 2))
    rang = pos[:, None] * ret_inv
    cos256, sin256 = jnp.cos(rang), jnp.sin(rang)

    offs = np.cumsum([0, 512, 128, 128, 128, 128, 128, 128, 24, 512, 512, 512, 512, 512, 256, 256])
    nsa_cols = _head_cols(NSA_HEAD_ORDER)
    col_idx = np.concatenate([
        offs[0] + nsa_cols,
        np.arange(offs[1], offs[7]),
    ])
    w0 = jnp.concatenate([
        l0_w_in[:, col_idx],
        l0_w_in[:, offs[7]:offs[8]], jnp.zeros((D_MODEL, LANES - 24), F32),
        l0_w_in[:, offs[8] + nsa_cols],
        l0_w_in[:, offs[9]:],
    ], axis=1).astype(BF16)
    gate_b = jnp.concatenate([l0_nsa_gate_b, jnp.zeros((LANES - 24,), F32)])[None, :]
    groups0 = [
        (512, "rot64", LOG2_SCALE), (128, "rot64", 1.0), (128, "plain", 1.0), (128, "rot64", 1.0), (128, "plain", 1.0),
        (128, "rot64", 1.0), (128, "plain", 1.0), (128, "gate", 1.0), (512, "silu", 1.0),
        (512, "rot64", LOG2_SCALE), (512, "rot64_mean", 1.0), (512, "plain", 1.0), (512, "silu", 1.0),
        (256, "plain", 0.125), (256, "silu", 1.0),
    ]
    dts0 = [BF16] * 7 + [F32] + [BF16] * 7
    (nq, nkc, nvc, nks, nvs, nkw, nvw, gates, snz, mq, mk, kmean, mv, smz, eq0, sez0) = _proj(
        x2, l0_norm_g[None, :], w0, cos64, sin64, gate_b, groups0, dts0, "proj0")

    kv0, kv1 = _mem_kv(mem[0], mem_norm_g[None, :], l0_w_mem_kv.astype(BF16), l1_w_mem_kv.astype(BF16))

    kc = _compress(nkc, l0_cmp_pe_k, l0_cmp_w1_k, l0_cmp_w2_k, "compress_k")
    vc = _compress(nvc, l0_cmp_pe_v, l0_cmp_w1_v, l0_cmp_w2_v, "compress_v")
    n_cmp_rows = T // NSA_CMP_STRIDE
    cmp_start = np.arange(n_cmp_rows)[:, None] * NSA_CMP_STRIDE
    sel_start = np.arange(n_selp)[None, :] * NSA_SEL_LEN
    overlap = ((cmp_start < sel_start + NSA_SEL_LEN) & (cmp_start + NSA_CMP_LEN > sel_start)
               & (np.arange(n_selp)[None, :] < n_sel))
    ov = jnp.asarray(overlap.T.astype(np.float32), dtype=BF16)
    oc, sel_bias = _nsa_cmp(nq, kc, vc, ov, gates, k_sel)
    ow = _nsa_win(nq, nkw, nvw, gates)
    ag = _nsa_sel(nq, sel_bias, nks, nvs, _one_hot_blocks(T, NSA_SEL_LEN), gates, oc, ow, snz)

    kmean_p = jnp.concatenate([kmean[:, 0, :], jnp.zeros((LANES - n_blk, 512), F32)], axis=0).astype(BF16)
    bg = _moba(mq, kmean_p, mk, mv, _one_hot_blocks(T, MOBA_BLOCK), smz, moba_top)

    wa = l0_w_out[:512][nsa_cols].astype(BF16)
    wb = l0_w_out[512:1024].astype(BF16)
    wm0 = l0_w_out[1024:].astype(BF16)
    x1 = _out0(ag, bg, eq0, sez0, kv0, wa, wb, wm0, x2)

    groups1 = [
        (1024, "rot256", 1.0), (1024, "rot256", RET_QK_DIM ** -0.5), (2048, "plain", 1.0), (2048, "silu", 1.0),
        (256, "plain", 0.125), (256, "silu", 1.0),
    ]
    dts1 = [BF16, F32, BF16, BF16, BF16, BF16]
    rq, rk, rv, srz, eq1, sez1 = _proj(x1, l1_norm_g[None, :], l1_w_in.astype(BF16), cos256, sin256,
                                       jnp.zeros((1, LANES), F32), groups1, dts1, "proj1")
    rg = _retention(rq, rk, rv, srz)
    v_w = RET_HEADS * RET_V_DIM
    out = _out1(rg, eq1, sez1, kv1, l1_w_out[:v_w].astype(BF16), l1_w_out[v_w:].astype(BF16), x1,
                final_norm_g[None, :])
    return out[None]
```

```python
import functools

import numpy as np
import jax
import jax.numpy as jnp
from jax import lax
from jax.experimental import pallas as pl
from jax.experimental.pallas import tpu as pltpu

F32 = jnp.float32
BF16 = jnp.bfloat16

D_MODEL = 1024
HEAD_DIM = 64
ROPE_THETA = 10000.0
NORM_EPS = 1e-6
NEG_INF = -1e30
FORCE_SELECT = 1e9
MASK_BIAS = -1e9
LOG2_SCALE = HEAD_DIM ** -0.5 * float(np.log2(np.e))

NSA_HEADS = 8
NSA_GROUP_HEADS = 4
NSA_CMP_LEN = 32
NSA_CMP_STRIDE = 16
NSA_SEL_LEN = 64
NSA_SEL_TOPK = 16
NSA_WINDOW = 512
MOBA_HEADS = 8
MOBA_BLOCK = 256
MOBA_TOPK = 3
MEM_LEN = 256
MEM_HEADS = 4
RET_HEADS = 4
RET_QK_DIM = 256
RET_V_DIM = 512
RET_CHUNK = 128

LANES = 128
PROJ_ROWS = 256
PROJ_COLS = 512
Q_TILE = 128
K_TILE = 512
VMEM_LIMIT = 56 * 1024 * 1024

NSA_HEAD_ORDER = (0, 4, 1, 5, 2, 6, 3, 7)


def _dot(a, b):
    return jnp.dot(a, b, preferred_element_type=F32)


def _dot_nt(a, b):
    return lax.dot_general(a, b, (((1,), (1,)), ((), ())), preferred_element_type=F32)


def _dot_tn(a, b):
    return lax.dot_general(a, b, (((0,), (0,)), ((), ())), preferred_element_type=F32)


def _silu(y):
    return y * (1.0 / (1.0 + jnp.exp(-y)))


def _rms(x, g):
    return x * lax.rsqrt(jnp.mean(x * x, axis=-1, keepdims=True) + NORM_EPS) * g


def _params(*sem):
    return pltpu.CompilerParams(dimension_semantics=sem, vmem_limit_bytes=VMEM_LIMIT)


def _proj_kernel(x_ref, g_ref, w_ref, c_ref, s_ref, b_ref, *out_refs, groups):
    h = _rms(x_ref[...], g_ref[...]).astype(BF16)
    rows = x_ref.shape[0]
    lane = lax.broadcasted_iota(jnp.int32, (rows, LANES), 1)
    low_half = (lane & 32) == 0
    col = 0
    oi = 0
    for width, kind, scale in groups:
        out_ref = out_refs[oi]
        oi += 1
        mean_ref = None
        if kind == "rot64_mean":
            mean_ref = out_refs[oi]
            oi += 1
        for c0 in range(0, width, PROJ_COLS):
            cw = min(PROJ_COLS, width - c0)
            y = _dot(h, w_ref[:, col + c0:col + c0 + cw])
            if kind in ("rot64", "rot64_mean"):
                c = c_ref[...]
                s = s_ref[...]
                pieces = []
                for j in range(cw // LANES):
                    p = y[:, j * LANES:(j + 1) * LANES]
                    partner = jnp.where(low_half, pltpu.roll(p, LANES - 32, 1), pltpu.roll(p, 32, 1))
                    pieces.append(p * c + partner * s)
                y = pieces[0] if len(pieces) == 1 else jnp.concatenate(pieces, axis=1)
            elif kind == "rot256":
                c = c_ref[...]
                s = s_ref[...]
                pieces = []
                for j in range(cw // RET_QK_DIM):
                    x1 = y[:, j * RET_QK_DIM:j * RET_QK_DIM + LANES]
                    x2 = y[:, j * RET_QK_DIM + LANES:(j + 1) * RET_QK_DIM]
                    pieces.append(x1 * c - x2 * s)
                    pieces.append(x1 * s + x2 * c)
                y = jnp.concatenate(pieces, axis=1)
            elif kind == "silu":
                y = _silu(y)
            elif kind == "gate":
                y = 1.0 / (1.0 + jnp.exp(-(y + b_ref[...])))
            if scale != 1.0:
                y = y * scale
            if mean_ref is not None:
                mean_ref[0, :, c0:c0 + cw] = jnp.mean(y, axis=0, keepdims=True)
            out_ref[:, c0:c0 + cw] = y.astype(out_ref.dtype)
        col += width


def _proj(x, g, w, cos_t, sin_t, bias, groups, out_dtypes, name):
    T = x.shape[0]
    n_tiles = T // PROJ_ROWS
    out_shapes = []
    out_specs = []
    for (width, kind, _), dt in zip(groups, out_dtypes):
        out_shapes.append(jax.ShapeDtypeStruct((T, width), dt))
        out_specs.append(pl.BlockSpec((PROJ_ROWS, width), lambda i: (i, 0)))
        if kind == "rot64_mean":
            out_shapes.append(jax.ShapeDtypeStruct((n_tiles, 1, width), F32))
            out_specs.append(pl.BlockSpec((1, 1, width), lambda i: (i, 0, 0)))
    return pl.pallas_call(
        functools.partial(_proj_kernel, groups=tuple(groups)),
        grid=(n_tiles,),
        in_specs=[
            pl.BlockSpec((PROJ_ROWS, D_MODEL), lambda i: (i, 0)),
            pl.BlockSpec((1, D_MODEL), lambda i: (0, 0)),
            pl.BlockSpec(w.shape, lambda i: (0, 0)),
            pl.BlockSpec((PROJ_ROWS, LANES), lambda i: (i, 0)),
            pl.BlockSpec((PROJ_ROWS, LANES), lambda i: (i, 0)),
            pl.BlockSpec((1, LANES), lambda i: (0, 0)),
        ],
        out_specs=out_specs,
        out_shape=out_shapes,
        compiler_params=_params("arbitrary"),
        name=name,
    )(x, g, w, cos_t, sin_t, bias)


def _compress_kernel(a_ref, an_ref, pea_ref, peb_ref, w1a_ref, w1b_ref, w2_ref, out_ref):
    xa = (a_ref[...].astype(F32) + pea_ref[...]).astype(BF16)
    xb = (an_ref[...].astype(F32) + peb_ref[...]).astype(BF16)
    hid = _silu(_dot(xa, w1a_ref[...]) + _dot(xb, w1b_ref[...]))
    out_ref[...] = _dot(hid.astype(BF16), w2_ref[...]).astype(out_ref.dtype)


def _compress(a, pe, w1, w2, name):
    T = a.shape[0]
    n_rows = T // NSA_CMP_STRIDE
    row_w = NSA_CMP_STRIDE * LANES
    a2 = a.reshape(n_rows, row_w)
    a2n = jnp.concatenate([a2[1:], jnp.zeros((1, row_w), a2.dtype)], axis=0)
    eye2 = jnp.eye(2, dtype=F32)

    def expand(w1_half):
        return jnp.einsum("lde,gh->lgdhe", w1_half, eye2).reshape(row_w, LANES).astype(BF16)

    def pe_row(pe_half):
        return jnp.broadcast_to(pe_half[:, None, :], (NSA_CMP_STRIDE, 2, HEAD_DIM)).reshape(1, row_w)

    w2bd = jnp.einsum("ef,gh->gehf", w2, eye2).reshape(LANES, LANES).astype(BF16)
    tile = min(256, n_rows)
    return pl.pallas_call(
        _compress_kernel,
        grid=(n_rows // tile,),
        in_specs=[
            pl.BlockSpec((tile, row_w), lambda i: (i, 0)),
            pl.BlockSpec((tile, row_w), lambda i: (i, 0)),
            pl.BlockSpec((1, row_w), lambda i: (0, 0)),
            pl.BlockSpec((1, row_w), lambda i: (0, 0)),
            pl.BlockSpec((row_w, LANES), lambda i: (0, 0)),
            pl.BlockSpec((row_w, LANES), lambda i: (0, 0)),
            pl.BlockSpec((LANES, LANES), lambda i: (0, 0)),
        ],
        out_specs=pl.BlockSpec((tile, LANES), lambda i: (i, 0)),
        out_shape=jax.ShapeDtypeStruct((n_rows, LANES), BF16),
        compiler_params=_params("arbitrary"),
        name=name,
    )(a2, a2n, pe_row(pe[:NSA_CMP_STRIDE]), pe_row(pe[NSA_CMP_STRIDE:]),
      expand(w1[:NSA_CMP_STRIDE]), expand(w1[NSA_CMP_STRIDE:]), w2bd)


def _nsa_head_query(q, h, lane):
    colv = q[:, (h % 4) * LANES:(h % 4 + 1) * LANES]
    keep = (lane < HEAD_DIM) if h < NSA_GROUP_HEADS else (lane >= HEAD_DIM)
    return jnp.where(keep, colv, jnp.zeros_like(colv))


def _nsa_stacked_queries(q, lane):
    return jnp.concatenate([_nsa_head_query(q, h, lane) for h in range(NSA_HEADS)], axis=0)


def _nsa_interleave_heads(o, gates, gate0, lane, out_ref):
    tq = out_ref.shape[0]
    outs = [o[h * tq:(h + 1) * tq] * gates[:, gate0 + h:gate0 + h + 1] for h in range(NSA_HEADS)]
    for j in range(4):
        out_ref[:, j * LANES:(j + 1) * LANES] = jnp.where(lane < HEAD_DIM, outs[j], outs[4 + j])


def _values_with_ones(v):
    ones = jnp.ones((v.shape[0], HEAD_DIM), v.dtype)
    return jnp.concatenate([v[:, :HEAD_DIM], ones], axis=1), jnp.concatenate([ones, v[:, HEAD_DIM:]], axis=1)


def _top_k_flags(x, idx, k):
    sel = jnp.zeros(x.shape, jnp.bool_)
    for _ in range(k):
        mx = jnp.max(x, axis=0, keepdims=True)
        first = jnp.min(jnp.where(x == mx, idx, x.shape[0]), axis=0, keepdims=True)
        hit = idx == first
        sel = sel | hit
        x = jnp.where(hit, -jnp.inf, x)
    return sel


CMP_KEY_BUCKET = 256


def _nsa_cmp_kernel(q_ref, kc_ref, vca_ref, vcb_ref, ovt_ref, gate_ref, oc_ref, bias_ref, imp_ref, *, k_sel):
    tq = q_ref.shape[0]
    n_cmp = kc_ref.shape[0]
    n_selp = ovt_ref.shape[0]
    rows = NSA_HEADS * tq
    half = rows // 2
    t0 = pl.program_id(0) * tq
    lane = lax.broadcasted_iota(jnp.int32, (tq, LANES), 1)

    def attend(nk):
        t_rows = t0 + (lax.broadcasted_iota(jnp.int32, (rows, 1), 0) & (tq - 1))
        cmp_end = lax.broadcasted_iota(jnp.int32, (1, nk), 1) * NSA_CMP_STRIDE + (NSA_CMP_LEN - 1)
        s = _dot_nt(_nsa_stacked_queries(q_ref[...], lane), kc_ref[0:nk, :])
        s = jnp.where(cmp_end <= t_rows, s, NEG_INF)
        m = jnp.max(s, axis=-1, keepdims=True)
        p = jnp.exp2(s - m)
        pb = p.astype(BF16)
        acc_a = _dot(pb[:half], vca_ref[0:nk, :])
        acc_b = _dot(pb[half:], vcb_ref[0:nk, :])
        l = jnp.concatenate([acc_a[:, HEAD_DIM:HEAD_DIM + 1], acc_b[:, 0:1]], axis=0)
        rl = jnp.where(m > 0.5 * NEG_INF, 1.0 / jnp.maximum(l, 1e-30), 0.0)
        _nsa_interleave_heads(jnp.concatenate([acc_a, acc_b], axis=0) * rl, gate_ref[...], 0, lane, oc_ref)
        pr = p * rl
        for g in range(2):
            r0 = g * half
            pg = pr[r0:r0 + tq] + pr[r0 + tq:r0 + 2 * tq] + pr[r0 + 2 * tq:r0 + 3 * tq] + pr[r0 + 3 * tq:r0 + 4 * tq]
            imp_ref[g] = _dot_nt(ovt_ref[:, 0:nk], pg.astype(BF16))

    needed = (t0 + tq - NSA_CMP_LEN) // NSA_CMP_STRIDE + 1
    buckets = sorted({min(b, n_cmp) for b in range(CMP_KEY_BUCKET, n_cmp + CMP_KEY_BUCKET, CMP_KEY_BUCKET)})
    for i, nk in enumerate(buckets):
        lo = buckets[i - 1] if i else -(1 << 30)
        cond = (needed > lo) if i == len(buckets) - 1 else ((needed > lo) & (needed <= nk))
        pl.when(cond)(functools.partial(attend, nk))

    blk = lax.broadcasted_iota(jnp.int32, (n_selp, tq), 0)
    cur = (t0 + lax.broadcasted_iota(jnp.int32, (1, tq), 1)) >> 6
    valid = blk <= cur
    forced = (blk == 0) | (blk == cur) | (blk == cur - 1)
    for g in range(2):
        x = jnp.where(forced, FORCE_SELECT, imp_ref[g])
        x = jnp.where(valid, x, NEG_INF)
        sel = _top_k_flags(x, blk, k_sel)
        bias_ref[g] = jnp.where(sel & valid, 0.0, MASK_BIAS).T.astype(BF16)


def _nsa_cmp(nq, kc, vc, ov, gates, k_sel):
    T = nq.shape[0]
    n_cmp = kc.shape[0]
    n_selp = ov.shape[0]
    vca, vcb = _values_with_ones(vc)
    return pl.pallas_call(
        functools.partial(_nsa_cmp_kernel, k_sel=k_sel),
        grid=(T // Q_TILE,),
        in_specs=[
            pl.BlockSpec((Q_TILE, 512), lambda i: (i, 0)),
            pl.BlockSpec((n_cmp, LANES), lambda i: (0, 0)),
            pl.BlockSpec((n_cmp, LANES), lambda i: (0, 0)),
            pl.BlockSpec((n_cmp, LANES), lambda i: (0, 0)),
            pl.BlockSpec((n_selp, n_cmp), lambda i: (0, 0)),
            pl.BlockSpec((Q_TILE, LANES), lambda i: (i, 0)),
        ],
        out_specs=[
            pl.BlockSpec((Q_TILE, 512), lambda i: (i, 0)),
            pl.BlockSpec((2, Q_TILE, n_selp), lambda i: (0, i, 0)),
        ],
        out_shape=[
            jax.ShapeDtypeStruct((T, 512), F32),
            jax.ShapeDtypeStruct((2, T, n_selp), BF16),
        ],
        scratch_shapes=[pltpu.VMEM((2, n_selp, Q_TILE), F32)],
        compiler_params=_params("arbitrary"),
        name="nsa_cmp_select",
    )(nq, kc, vca, vcb, ov, gates)


def _nsa_win_kernel(q_ref, kw_ref, va_ref, vb_ref, gate_ref, ow_ref):
    tq = q_ref.shape[0]
    rows = NSA_HEADS * tq
    half = rows // 2
    span = NSA_WINDOW + tq
    t0 = pl.program_id(0) * tq
    start = pl.multiple_of(jnp.maximum(t0 - NSA_WINDOW, 0), tq)
    lane = lax.broadcasted_iota(jnp.int32, (tq, LANES), 1)
    t_rows = t0 + (lax.broadcasted_iota(jnp.int32, (rows, 1), 0) & (tq - 1))
    wpos = start + lax.broadcasted_iota(jnp.int32, (1, span), 1)
    s = _dot_nt(_nsa_stacked_queries(q_ref[...], lane), kw_ref[pl.ds(start, span), :])
    s = jnp.where((wpos <= t_rows) & (wpos > t_rows - NSA_WINDOW), s, NEG_INF)
    pb = jnp.exp2(s - jnp.max(s, axis=-1, keepdims=True)).astype(BF16)
    acc = jnp.concatenate([_dot(pb[:half], va_ref[pl.ds(start, span), :]),
                           _dot(pb[half:], vb_ref[pl.ds(start, span), :])], axis=0)
    _nsa_interleave_heads(acc / pltpu.roll(acc, HEAD_DIM, 1), gate_ref[...], 2 * NSA_HEADS, lane, ow_ref)


def _nsa_win(nq, kw, vw, gates):
    T = nq.shape[0]
    va, vb = _values_with_ones(vw)
    full = pl.BlockSpec((T, LANES), lambda i: (0, 0))
    return pl.pallas_call(
        _nsa_win_kernel,
        grid=(T // Q_TILE,),
        in_specs=[pl.BlockSpec((Q_TILE, 512), lambda i: (i, 0)), full, full, full,
                  pl.BlockSpec((Q_TILE, LANES), lambda i: (i, 0))],
        out_specs=pl.BlockSpec((Q_TILE, 512), lambda i: (i, 0)),
        out_shape=jax.ShapeDtypeStruct((T, 512), F32),
        compiler_params=_params("arbitrary"),
        name="nsa_window",
    )(nq, kw, va, vb, gates)


def _flash_scratch(rows):
    return [
        pltpu.VMEM((rows, K_TILE), F32), pltpu.VMEM((rows, K_TILE), F32),
        pltpu.VMEM((rows, K_TILE), BF16), pltpu.VMEM((rows, K_TILE), BF16),
        pltpu.VMEM((rows, 1), F32), pltpu.VMEM((rows, 1), F32),
        pltpu.VMEM((rows, 1), F32),
        pltpu.VMEM((rows, LANES), F32),
    ]


def _flash_body(lhs_ref, k_ref, e_ref, va_ref, vb_ref, t_of_rows, n_full, tokens_per_half, scratch):
    s_a, s_b, p_a, p_b, al_a, al_b, m_ref, acc_ref = scratch
    tk = K_TILE
    rows = acc_ref.shape[0]
    half_rows = rows // 2
    last_mem_tile = k_ref.shape[0] // tk - 1

    def tile_start(j):
        return pl.multiple_of(jnp.clip(j, 0, last_mem_tile) * tk, tk)

    def scores(j, s_out):
        k0 = tile_start(j)
        rhs = jnp.concatenate([k_ref[pl.ds(k0, tk), :], e_ref[pl.ds(k0, tk), :]], axis=1)
        s_out[...] = _dot_nt(lhs_ref[k0 // tokens_per_half], rhs)

    def softmax(j, s_in, p_out, al_out, causal):
        s = s_in[...]
        if causal:
            kpos = j * tk + lax.broadcasted_iota(jnp.int32, (1, tk), 1)
            s = jnp.where(kpos <= t_of_rows(0, rows), s, NEG_INF)
        m_old = m_ref[...]
        m_new = jnp.maximum(m_old, jnp.max(s, axis=-1, keepdims=True))
        m_ref[...] = m_new
        al_out[...] = jnp.exp2(m_old - m_new)
        p_out[...] = jnp.exp2(s - m_new).astype(BF16)

    def values(j, p_in, al_in):
        k0 = tile_start(j)
        for rs, v_ref in ((slice(0, half_rows), va_ref), (slice(half_rows, rows), vb_ref)):
            acc_ref[rs, :] = al_in[rs, :] * acc_ref[rs, :] + _dot(p_in[rs, :], v_ref[pl.ds(k0, tk), :])

    m_ref[...] = jnp.full(m_ref.shape, NEG_INF, F32)
    acc_ref[...] = jnp.zeros(acc_ref.shape, F32)
    p_b[...] = jnp.zeros(p_b.shape, BF16)
    al_b[...] = jnp.ones(al_b.shape, F32)
    scores(0, s_a)

    def two_full_tiles(i, carry):
        j = 2 * i
        values(j - 1, p_b, al_b)
        scores(j + 1, s_b)
        softmax(j, s_a, p_a, al_a, False)
        values(j, p_a, al_a)
        scores(j + 2, s_a)
        softmax(j + 1, s_b, p_b, al_b, False)
        return carry

    n_pairs = n_full // 2
    lax.fori_loop(0, n_pairs, two_full_tiles, 0)
    j = 2 * n_pairs
    odd = (n_full & 1) == 1
    values(j - 1, p_b, al_b)

    @pl.when(odd)
    def _():
        scores(j + 1, s_b)

    softmax(j, s_a, p_a, al_a, True)
    values(j, p_a, al_a)

    @pl.when(odd)
    def _():
        softmax(j + 1, s_b, p_b, al_b, True)
        values(j + 1, p_b, al_b)

    acc = acc_ref[...]
    return acc / pltpu.roll(acc, HEAD_DIM, 1)


def _nsa_sel_kernel(q_ref, bias_ref, ks_ref, va_ref, vb_ref, e_ref, gate_ref, oc_ref, ow_ref, sz_ref, out_ref,
                    lhs_ref, *scratch):
    tq = q_ref.shape[0]
    t0 = pl.program_id(0) * tq
    lane = lax.broadcasted_iota(jnp.int32, (tq, LANES), 1)
    q = q_ref[...]
    for half in range(lhs_ref.shape[0]):
        for h in range(NSA_HEADS):
            b = bias_ref[h // NSA_GROUP_HEADS, :, half * LANES:(half + 1) * LANES]
            lhs_ref[half, h * tq:(h + 1) * tq, :] = jnp.concatenate([_nsa_head_query(q, h, lane), b], axis=1)

    def t_of_rows(r0, n):
        return t0 + ((r0 + lax.broadcasted_iota(jnp.int32, (n, 1), 0)) & (tq - 1))

    o = _flash_body(lhs_ref, ks_ref, e_ref, va_ref, vb_ref, t_of_rows, t0 // K_TILE, NSA_SEL_LEN * LANES, scratch)
    gates = gate_ref[...]
    outs = [o[h * tq:(h + 1) * tq] * gates[:, NSA_HEADS + h:NSA_HEADS + h + 1] for h in range(NSA_HEADS)]
    for j in range(4):
        cs = slice(j * LANES, (j + 1) * LANES)
        a = jnp.where(lane < HEAD_DIM, outs[j], outs[4 + j]) + oc_ref[:, cs] + ow_ref[:, cs]
        out_ref[:, cs] = (a * sz_ref[:, cs].astype(F32)).astype(out_ref.dtype)


def _nsa_sel(nq, bias, ks, vs, e_sel, gates, oc, ow, sz):
    T = nq.shape[0]
    n_selp = bias.shape[2]
    rows = NSA_HEADS * Q_TILE
    row = lambda w: pl.BlockSpec((Q_TILE, w), lambda qi: (qi, 0))
    full = pl.BlockSpec((T, LANES), lambda qi: (0, 0))
    va, vb = _values_with_ones(vs)
    return pl.pallas_call(
        _nsa_sel_kernel,
        grid=(T // Q_TILE,),
        in_specs=[row(512), pl.BlockSpec((2, Q_TILE, n_selp), lambda qi: (0, qi, 0)), full, full, full, full,
                  row(LANES), row(512), row(512), row(512)],
        out_specs=row(512),
        out_shape=jax.ShapeDtypeStruct((T, 512), BF16),
        scratch_shapes=[pltpu.VMEM((n_selp // LANES, rows, 2 * LANES), BF16)] + _flash_scratch(rows),
        compiler_params=_params("arbitrary"),
        name="nsa_selected",
    )(nq, bias, ks, va, vb, e_sel, gates, oc, ow, sz)


MOBA_Q_TILE = 512


def _moba_kernel(q_ref, kmean_ref, k_ref, va_ref, vb_ref, e_ref, sz_ref, out_ref, lhs_ref, *scratch, top):
    tq = q_ref.shape[0]
    t0 = pl.program_id(1) * tq
    lane = lax.broadcasted_iota(jnp.int32, (tq, LANES), 1)
    colv = q_ref[...]
    zero = jnp.zeros_like(colv)
    qp = jnp.concatenate([jnp.where(lane < HEAD_DIM, colv, zero), jnp.where(lane >= HEAD_DIM, colv, zero)], axis=0)

    def t_of_rows(r0, n):
        return t0 + ((r0 + lax.broadcasted_iota(jnp.int32, (n, 1), 0)) & (tq - 1))

    cur = (t0 + (lax.broadcasted_iota(jnp.int32, (1, 2 * tq), 1) & (tq - 1))) >> 8
    blk = lax.broadcasted_iota(jnp.int32, (LANES, 2 * tq), 0)
    past = blk < cur
    gate = jnp.where(past, _dot_nt(kmean_ref[...], qp), NEG_INF)
    sel = _top_k_flags(gate, blk, top)
    bias = jnp.where((sel & past) | (blk == cur), 0.0, MASK_BIAS).T.astype(BF16)
    lhs_ref[0] = jnp.concatenate([qp, bias], axis=1)
    o = _flash_body(lhs_ref, k_ref, e_ref, va_ref, vb_ref, t_of_rows, t0 // K_TILE, MOBA_BLOCK * LANES, scratch)
    a = jnp.where(lane < HEAD_DIM, o[:tq], o[tq:])
    out_ref[...] = (a * sz_ref[...].astype(F32)).astype(out_ref.dtype)


def _moba(mq, kmean, mk, mv, e_blk, sz, top):
    T = mq.shape[0]
    tq = MOBA_Q_TILE
    assert MOBA_BLOCK == 256 and T * 1 <= MOBA_BLOCK * LANES
    rows = 2 * tq
    row = pl.BlockSpec((tq, LANES), lambda p, qi: (qi, p))
    col = pl.BlockSpec((T, LANES), lambda p, qi: (0, p))
    v4 = mv.reshape(T, MOBA_HEADS // 2, 2, HEAD_DIM)
    ones = jnp.ones((T, MOBA_HEADS // 2, HEAD_DIM), mv.dtype)
    va = jnp.stack([v4[:, :, 0], ones], axis=2).reshape(T, 512)
    vb = jnp.stack([ones, v4[:, :, 1]], axis=2).reshape(T, 512)
    return pl.pallas_call(
        functools.partial(_moba_kernel, top=top),
        grid=(MOBA_HEADS // 2, T // tq),
        in_specs=[row, pl.BlockSpec((LANES, LANES), lambda p, qi: (0, p)), col, col, col,
                  pl.BlockSpec((T, LANES), lambda p, qi: (0, 0)), row],
        out_specs=row,
        out_shape=jax.ShapeDtypeStruct((T, 512), BF16),
        scratch_shapes=[pltpu.VMEM((1, rows, 2 * LANES), BF16)] + _flash_scratch(rows),
        compiler_params=_params("arbitrary", "arbitrary"),
        name="moba",
    )(mq, kmean, mk, va, vb, e_blk, sz)


def _mem_kv_kernel(mem_ref, g_ref, w0_ref, w1_ref, kv0_ref, kv1_ref):
    mem_n = _rms(mem_ref[...], g_ref[...]).astype(BF16)
    kv0_ref[...] = _dot(mem_n, w0_ref[...]).astype(kv0_ref.dtype)
    kv1_ref[...] = _dot(mem_n, w1_ref[...]).astype(kv1_ref.dtype)


def _mem_kv(mem, g, w0, w1):
    n = mem.shape[0]
    kv_w = 2 * MEM_HEADS * HEAD_DIM
    return pl.pallas_call(
        _mem_kv_kernel,
        grid=(1,),
        in_specs=[
            pl.BlockSpec((n, D_MODEL), lambda i: (0, 0)),
            pl.BlockSpec((1, D_MODEL), lambda i: (0, 0)),
            pl.BlockSpec((D_MODEL, kv_w), lambda i: (0, 0)),
            pl.BlockSpec((D_MODEL, kv_w), lambda i: (0, 0)),
        ],
        out_specs=[pl.BlockSpec((n, kv_w), lambda i: (0, 0)), pl.BlockSpec((n, kv_w), lambda i: (0, 0))],
        out_shape=[jax.ShapeDtypeStruct((n, kv_w), BF16), jax.ShapeDtypeStruct((n, kv_w), BF16)],
        compiler_params=_params("arbitrary"),
        name="mem_kv",
    )(mem, g, w0, w1)


def _mem_attention(eq, kv, sez, lane):
    rows = eq.shape[0]
    mem_w = MEM_HEADS * HEAD_DIM
    cols = []
    for p in range(MEM_HEADS // 2):
        cs = slice(p * LANES, (p + 1) * LANES)
        colv = eq[:, cs]
        zero = jnp.zeros_like(colv)
        qp = jnp.concatenate([jnp.where(lane < HEAD_DIM, colv, zero), jnp.where(lane >= HEAD_DIM, colv, zero)], axis=0)
        s = _dot_nt(qp, kv[:, cs])
        pe = jnp.exp(s - jnp.max(s, axis=-1, keepdims=True))
        o = _dot(pe.astype(BF16), kv[:, mem_w + p * LANES:mem_w + (p + 1) * LANES]) / jnp.sum(pe, axis=-1, keepdims=True)
        cols.append(jnp.where(lane < HEAD_DIM, o[:rows], o[rows:]) * sez[:, cs].astype(F32))
    return jnp.concatenate(cols, axis=1).astype(BF16)


def _out0_kernel(ag_ref, bg_ref, eq_ref, sez_ref, kv_ref, wa_ref, wb_ref, wm_ref, x_ref, out_ref):
    rows = x_ref.shape[0]
    lane = lax.broadcasted_iota(jnp.int32, (rows, LANES), 1)
    mg = _mem_attention(eq_ref[...], kv_ref[...], sez_ref[...], lane)
    y = _dot(ag_ref[...], wa_ref[...]) + _dot(bg_ref[...], wb_ref[...]) + _dot(mg, wm_ref[...])
    out_ref[...] = x_ref[...] + y


def _out0(ag, bg, eq, sez, kv, wa, wb, wm, x):
    T = x.shape[0]
    rt = PROJ_ROWS
    row = lambda w: pl.BlockSpec((rt, w), lambda i: (i, 0))
    full = lambda a: pl.BlockSpec(a.shape, lambda i: (0, 0))
    return pl.pallas_call(
        _out0_kernel,
        grid=(T // rt,),
        in_specs=[row(512), row(512), row(256), row(256), full(kv), full(wa), full(wb), full(wm), row(D_MODEL)],
        out_specs=row(D_MODEL),
        out_shape=jax.ShapeDtypeStruct((T, D_MODEL), F32),
        compiler_params=_params("arbitrary"),
        name="out_proj0",
    )(ag, bg, eq, sez, kv, wa, wb, wm, x)


def _out1_kernel(rg_ref, eq_ref, sez_ref, kv_ref, wr_ref, wm_ref, x_ref, g_ref, out_ref):
    rows = x_ref.shape[0]
    lane = lax.broadcasted_iota(jnp.int32, (rows, LANES), 1)
    mg = _mem_attention(eq_ref[...], kv_ref[...], sez_ref[...], lane)
    y = _dot(rg_ref[...], wr_ref[...]) + _dot(mg, wm_ref[...])
    out_ref[...] = _rms(x_ref[...] + y, g_ref[...])


def _out1(rg, eq, sez, kv, wr, wm, x, g):
    T = x.shape[0]
    rt = PROJ_ROWS
    row = lambda w: pl.BlockSpec((rt, w), lambda i: (i, 0))
    full = lambda a: pl.BlockSpec(a.shape, lambda i: (0, 0))
    return pl.pallas_call(
        _out1_kernel,
        grid=(T // rt,),
        in_specs=[row(RET_HEADS * RET_V_DIM), row(256), row(256), full(kv), full(wr), full(wm), row(D_MODEL), full(g)],
        out_specs=row(D_MODEL),
        out_shape=jax.ShapeDtypeStruct((T, D_MODEL), F32),
        compiler_params=_params("arbitrary"),
        name="out_proj1_norm",
    )(rg, eq, sez, kv, wr, wm, x, g)


def _retention_kernel(q_ref, k_ref, v_ref, sz_ref, decay_ref, qd_ref, kd_ref, cd_ref, out_ref, state_ref):
    @pl.when(pl.program_id(0) == 0)
    def _():
        state_ref[...] = jnp.zeros(state_ref.shape, F32)

    for h in range(RET_HEADS):
        qs = slice(h * RET_QK_DIM, (h + 1) * RET_QK_DIM)
        vsl = slice(h * RET_V_DIM, (h + 1) * RET_V_DIM)
        qh = q_ref[:, qs]
        kh = k_ref[:, qs]
        vh = v_ref[:, vsl]
        st = state_ref[h]
        inner = _dot_nt(qh, kh.astype(BF16)) * decay_ref[h]
        o = _dot(inner.astype(BF16), vh) + _dot(qh, st.astype(BF16)) * qd_ref[h]
        state_ref[h] = st * cd_ref[h] + _dot_tn((kh * kd_ref[h]).astype(BF16), vh)
        mu = jnp.mean(o, axis=-1, keepdims=True)
        d = o - mu
        on = d * lax.rsqrt(jnp.mean(d * d, axis=-1, keepdims=True) + NORM_EPS)
        out_ref[:, vsl] = (on * sz_ref[:, vsl].astype(F32)).astype(out_ref.dtype)


def _retention(rq, rk, rv, sz):
    T = rq.shape[0]
    C = RET_CHUNK
    log_g = jnp.log(1.0 - 2.0 ** (-5.0 - jnp.arange(RET_HEADS, dtype=F32)))
    i = jnp.arange(C, dtype=F32)
    diff = i[:, None] - i[None, :]
    decay = jnp.where(diff >= 0, jnp.exp(jnp.maximum(diff, 0.0)[None] * log_g[:, None, None]), 0.0)
    q_decay = jnp.exp((i + 1.0)[None, :] * log_g[:, None])[:, :, None]
    k_decay = jnp.exp((C - 1.0 - i)[None, :] * log_g[:, None])[:, :, None]
    chunk_decay = jnp.exp(C * log_g)[:, None, None]
    qk_w = RET_HEADS * RET_QK_DIM
    v_w = RET_HEADS * RET_V_DIM
    return pl.pallas_call(
        _retention_kernel,
        grid=(T // C,),
        in_specs=[
            pl.BlockSpec((C, qk_w), lambda c: (c, 0)),
            pl.BlockSpec((C, qk_w), lambda c: (c, 0)),
            pl.BlockSpec((C, v_w), lambda c: (c, 0)),
            pl.BlockSpec((C, v_w), lambda c: (c, 0)),
            pl.BlockSpec((RET_HEADS, C, C), lambda c: (0, 0, 0)),
            pl.BlockSpec((RET_HEADS, C, 1), lambda c: (0, 0, 0)),
            pl.BlockSpec((RET_HEADS, C, 1), lambda c: (0, 0, 0)),
            pl.BlockSpec((RET_HEADS, 1, 1), lambda c: (0, 0, 0)),
        ],
        out_specs=pl.BlockSpec((C, v_w), lambda c: (c, 0)),
        out_shape=jax.ShapeDtypeStruct((T, v_w), BF16),
        scratch_shapes=[pltpu.VMEM((RET_HEADS, RET_QK_DIM, RET_V_DIM), F32)],
        compiler_params=_params("arbitrary"),
        name="retention",
    )(rq, rk, rv, sz, decay, q_decay, k_decay, chunk_decay)


def _head_cols(order):
    return np.concatenate([np.arange(h * HEAD_DIM, (h + 1) * HEAD_DIM) for h in order])


def _one_hot_blocks(T, block):
    b = (np.arange(T) // block) % LANES
    return jnp.asarray(np.eye(LANES, dtype=np.float32)[b], dtype=BF16)


def kernel(x, mem, positions, l0_norm_g, l0_w_in, l0_nsa_gate_b, l0_cmp_pe_k, l0_cmp_w1_k, l0_cmp_w2_k,
           l0_cmp_pe_v, l0_cmp_w1_v, l0_cmp_w2_v, l0_w_mem_kv, l0_w_out,
           l1_norm_g, l1_w_in, l1_w_mem_kv, l1_w_out, mem_norm_g, final_norm_g):
    B, T, _ = x.shape
    assert B == 1 and T % K_TILE == 0 and T >= NSA_WINDOW + Q_TILE
    n_sel = T // NSA_SEL_LEN
    n_selp = -(-n_sel // LANES) * LANES
    k_sel = min(NSA_SEL_TOPK, n_sel)
    n_blk = T // MOBA_BLOCK
    assert n_blk <= LANES
    moba_top = max(1, min(MOBA_TOPK, n_blk - 1))
    x2 = x[0]
    pos = positions[0].astype(F32)

    attn_inv = 1.0 / (ROPE_THETA ** (jnp.arange(0, HEAD_DIM, 2, dtype=F32) / HEAD_DIM))
    ret_inv = 1.0 / (ROPE_THETA ** jnp.linspace(0.0, 1.0, RET_QK_DIM // 2, dtype=F32))
    ang = pos[:, None] * attn_inv
    cos64 = jnp.tile(jnp.cos(ang), (1, 4))
    sin64 = jnp.tile(jnp.concatenate([-jnp.sin(ang), jnp.sin(ang)], axis=1), (1, 2))
    rang = pos[:, None] * ret_inv
    cos256, sin256 = jnp.cos(rang), jnp.sin(rang)

    offs = np.cumsum([0, 512, 128, 128, 128, 128, 128, 128, 24, 512, 512, 512, 512, 512, 256, 256])
    nsa_cols = _head_cols(NSA_HEAD_ORDER)
    col_idx = np.concatenate([
        offs[0] + nsa_cols,
        np.arange(offs[1], offs[7]),
    ])
    w0 = jnp.concatenate([
        l0_w_in[:, col_idx],
        l0_w_in[:, offs[7]:offs[8]], jnp.zeros((D_MODEL, LANES - 24), F32),
        l0_w_in[:, offs[8] + nsa_cols],
        l0_w_in[:, offs[9]:],
    ], axis=1).astype(BF16)
    gate_b = jnp.concatenate([l0_nsa_gate_b, jnp.zeros((LANES - 24,), F32)])[None, :]
    groups0 = [
        (512, "rot64", LOG2_SCALE), (128, "rot64", 1.0), (128, "plain", 1.0), (128, "rot64", 1.0), (128, "plain", 1.0),
        (128, "rot64", 1.0), (128, "plain", 1.0), (128, "gate", 1.0), (512, "silu", 1.0),
        (512, "rot64", LOG2_SCALE), (512, "rot64_mean", 1.0), (512, "plain", 1.0), (512, "silu", 1.0),
        (256, "plain", 0.125), (256, "silu", 1.0),
    ]
    dts0 = [BF16] * 7 + [F32] + [BF16] * 7
    (nq, nkc, nvc, nks, nvs, nkw, nvw, gates, snz, mq, mk, kmean, mv, smz, eq0, sez0) = _proj(
        x2, l0_norm_g[None, :], w0, cos64, sin64, gate_b, groups0, dts0, "proj0")

    kv0, kv1 = _mem_kv(mem[0], mem_norm_g[None, :], l0_w_mem_kv.astype(BF16), l1_w_mem_kv.astype(BF16))

    kc = _compress(nkc, l0_cmp_pe_k, l0_cmp_w1_k, l0_cmp_w2_k, "compress_k")
    vc = _compress(nvc, l0_cmp_pe_v, l0_cmp_w1_v, l0_cmp_w2_v, "compress_v")
    n_cmp_rows = T // NSA_CMP_STRIDE
    cmp_start = np.arange(n_cmp_rows)[:, None] * NSA_CMP_STRIDE
    sel_start = np.arange(n_selp)[None, :] * NSA_SEL_LEN
    overlap = ((cmp_start < sel_start + NSA_SEL_LEN) & (cmp_start + NSA_CMP_LEN > sel_start)
               & (np.arange(n_selp)[None, :] < n_sel))
    ov = jnp.asarray(overlap.T.astype(np.float32), dtype=BF16)
    oc, sel_bias = _nsa_cmp(nq, kc, vc, ov, gates, k_sel)
    ow = _nsa_win(nq, nkw, nvw, gates)
    ag = _nsa_sel(nq, sel_bias, nks, nvs, _one_hot_blocks(T, NSA_SEL_LEN), gates, oc, ow, snz)

    kmean_p = jnp.concatenate([kmean[:, 0, :], jnp.zeros((LANES - n_blk, 512), F32)], axis=0).astype(BF16)
    bg = _moba(mq, kmean_p, mk, mv, _one_hot_blocks(T, MOBA_BLOCK), smz, moba_top)

    wa = l0_w_out[:512][nsa_cols].astype(BF16)
    wb = l0_w_out[512:1024].astype(BF16)
    wm0 = l0_w_out[1024:].astype(BF16)
    x1 = _out0(ag, bg, eq0, sez0, kv0, wa, wb, wm0, x2)

    groups1 = [
        (1024, "rot256", 1.0), (1024, "rot256", RET_QK_DIM ** -0.5), (2048, "plain", 1.0), (2048, "silu", 1.0),
        (256, "plain", 0.125), (256, "silu", 1.0),
    ]
    dts1 = [BF16, F32, BF16, BF16, BF16, BF16]
    rq, rk, rv, srz, eq1, sez1 = _proj(x1, l1_norm_g[None, :], l1_w_in.astype(BF16), cos256, sin256,
                                       jnp.zeros((1, LANES), F32), groups1, dts1, "proj1")
    rg = _retention(rq, rk, rv, srz)
    v_w = RET_HEADS * RET_V_DIM
    out = _out1(rg, eq1, sez1, kv1, l1_w_out[:v_w].astype(BF16), l1_w_out[v_w:].astype(BF16), x1,
                final_norm_g[None, :])
    return out[None]
```

```python
import functools

import numpy as np
import jax
import jax.numpy as jnp
from jax import lax
from jax.experimental import pallas as pl
from jax.experimental.pallas import tpu as pltpu

F32 = jnp.float32
BF16 = jnp.bfloat16

D_MODEL = 1024
HEAD_DIM = 64
ROPE_THETA = 10000.0
NORM_EPS = 1e-6
NEG_INF = -1e30
FORCE_SELECT = 1e9
MASK_BIAS = -1e9
LOG2_SCALE = HEAD_DIM ** -0.5 * float(np.log2(np.e))

NSA_HEADS = 8
NSA_GROUP_HEADS = 4
NSA_CMP_LEN = 32
NSA_CMP_STRIDE = 16
NSA_SEL_LEN = 64
NSA_SEL_TOPK = 16
NSA_WINDOW = 512
MOBA_HEADS = 8
MOBA_BLOCK = 256
MOBA_TOPK = 3
MEM_LEN = 256
MEM_HEADS = 4
RET_HEADS = 4
RET_QK_DIM = 256
RET_V_DIM = 512
RET_CHUNK = 128

LANES = 128
PROJ_ROWS = 256
PROJ_COLS = 512
Q_TILE = 128
K_TILE = 512
VMEM_LIMIT = 56 * 1024 * 1024

NSA_HEAD_ORDER = (0, 4, 1, 5, 2, 6, 3, 7)


def _dot(a, b):
    return jnp.dot(a, b, preferred_element_type=F32)


def _dot_nt(a, b):
    return lax.dot_general(a, b, (((1,), (1,)), ((), ())), preferred_element_type=F32)


def _dot_tn(a, b):
    return lax.dot_general(a, b, (((0,), (0,)), ((), ())), preferred_element_type=F32)


def _silu(y):
    return y * (1.0 / (1.0 + jnp.exp(-y)))


def _rms(x, g):
    return x * lax.rsqrt(jnp.mean(x * x, axis=-1, keepdims=True) + NORM_EPS) * g


def _params(*sem):
    return pltpu.CompilerParams(dimension_semantics=sem, vmem_limit_bytes=VMEM_LIMIT)


def _proj_kernel(x_ref, g_ref, w_ref, c_ref, s_ref, b_ref, *out_refs, groups):
    h = _rms(x_ref[...], g_ref[...]).astype(BF16)
    rows = x_ref.shape[0]
    lane = lax.broadcasted_iota(jnp.int32, (rows, LANES), 1)
    low_half = (lane & 32) == 0
    pieces = []
    col = 0
    oi = 0
    for width, kind, scale in groups:
        out_ref = out_refs[oi]
        oi += 1
        mean_ref = None
        if kind in ("rot64_mean", "ones_pair"):
            mean_ref = out_refs[oi]
            oi += 1
        pw = RET_QK_DIM if kind == "rot256" else LANES
        for c0 in range(0, width, pw):
            pieces.append((col + c0, pw, kind, scale, out_ref, c0, mean_ref))
        col += width

    def epilogue(z, kind, scale):
        if kind in ("rot64", "rot64_mean"):
            partner = jnp.where(low_half, pltpu.roll(z, LANES - 32, 1), pltpu.roll(z, 32, 1))
            z = z * c_ref[...] + partner * s_ref[...]
        elif kind == "rot256":
            x1, x2 = z[:, :LANES], z[:, LANES:]
            z = jnp.concatenate([x1 * c_ref[...] - x2 * s_ref[...], x1 * s_ref[...] + x2 * c_ref[...]], axis=1)
        elif kind == "silu":
            z = _silu(z)
        elif kind == "gate":
            z = 1.0 / (1.0 + jnp.exp(-(z + b_ref[...])))
        return z if scale == 1.0 else z * scale

    i = 0
    while i < len(pieces):
        j, run_w = i, 0
        while j < len(pieces) and run_w + pieces[j][1] <= PROJ_COLS:
            run_w += pieces[j][1]
            j += 1
        c_start = pieces[i][0]
        y = _dot(h, w_ref[:, c_start:c_start + run_w])
        for pc, pw, kind, scale, out_ref, oc, mean_ref in pieces[i:j]:
            z = epilogue(y[:, pc - c_start:pc - c_start + pw], kind, scale)
            if kind == "ones_pair":
                out_ref[:, oc:oc + pw] = jnp.where(lane < HEAD_DIM, z, 1.0).astype(out_ref.dtype)
                mean_ref[:, oc:oc + pw] = jnp.where(lane >= HEAD_DIM, z, 1.0).astype(mean_ref.dtype)
                continue
            if mean_ref is not None:
                mean_ref[0, :, oc:oc + pw] = jnp.mean(z, axis=0, keepdims=True)
            out_ref[:, oc:oc + pw] = z.astype(out_ref.dtype)
        i = j


def _proj(x, g, w, cos_t, sin_t, bias, groups, out_dtypes, name):
    T = x.shape[0]
    n_tiles = T // PROJ_ROWS
    out_shapes = []
    out_specs = []
    for (width, kind, _), dt in zip(groups, out_dtypes):
        out_shapes.append(jax.ShapeDtypeStruct((T, width), dt))
        out_specs.append(pl.BlockSpec((PROJ_ROWS, width), lambda i: (i, 0)))
        if kind == "rot64_mean":
            out_shapes.append(jax.ShapeDtypeStruct((n_tiles, 1, width), F32))
            out_specs.append(pl.BlockSpec((1, 1, width), lambda i: (i, 0, 0)))
        if kind == "ones_pair":
            out_shapes.append(jax.ShapeDtypeStruct((T, width), dt))
            out_specs.append(pl.BlockSpec((PROJ_ROWS, width), lambda i: (i, 0)))
    return pl.pallas_call(
        functools.partial(_proj_kernel, groups=tuple(groups)),
        grid=(n_tiles,),
        in_specs=[
            pl.BlockSpec((PROJ_ROWS, D_MODEL), lambda i: (i, 0)),
            pl.BlockSpec((1, D_MODEL), lambda i: (0, 0)),
            pl.BlockSpec(w.shape, lambda i: (0, 0)),
            pl.BlockSpec((PROJ_ROWS, LANES), lambda i: (i, 0)),
            pl.BlockSpec((PROJ_ROWS, LANES), lambda i: (i, 0)),
            pl.BlockSpec((1, LANES), lambda i: (0, 0)),
        ],
        out_specs=out_specs,
        out_shape=out_shapes,
        compiler_params=_params("arbitrary"),
        name=name,
    )(x, g, w, cos_t, sin_t, bias)


def _compress_kernel(a_ref, an_ref, pea_ref, peb_ref, w1a_ref, w1b_ref, w2_ref, out_ref):
    xa = (a_ref[...].astype(F32) + pea_ref[...]).astype(BF16)
    xb = (an_ref[...].astype(F32) + peb_ref[...]).astype(BF16)
    hid = _silu(_dot(xa, w1a_ref[...]) + _dot(xb, w1b_ref[...]))
    out_ref[...] = _dot(hid.astype(BF16), w2_ref[...]).astype(out_ref.dtype)


def _compress(a, pe, w1, w2, name):
    T = a.shape[0]
    n_rows = T // NSA_CMP_STRIDE
    row_w = NSA_CMP_STRIDE * LANES
    a2 = a.reshape(n_rows, row_w)
    a2n = jnp.concatenate([a2[1:], jnp.zeros((1, row_w), a2.dtype)], axis=0)
    eye2 = jnp.eye(2, dtype=F32)

    def expand(w1_half):
        return jnp.einsum("lde,gh->lgdhe", w1_half, eye2).reshape(row_w, LANES).astype(BF16)

    def pe_row(pe_half):
        return jnp.broadcast_to(pe_half[:, None, :], (NSA_CMP_STRIDE, 2, HEAD_DIM)).reshape(1, row_w)

    w2bd = jnp.einsum("ef,gh->gehf", w2, eye2).reshape(LANES, LANES).astype(BF16)
    tile = min(256, n_rows)
    return pl.pallas_call(
        _compress_kernel,
        grid=(n_rows // tile,),
        in_specs=[
            pl.BlockSpec((tile, row_w), lambda i: (i, 0)),
            pl.BlockSpec((tile, row_w), lambda i: (i, 0)),
            pl.BlockSpec((1, row_w), lambda i: (0, 0)),
            pl.BlockSpec((1, row_w), lambda i: (0, 0)),
            pl.BlockSpec((row_w, LANES), lambda i: (0, 0)),
            pl.BlockSpec((row_w, LANES), lambda i: (0, 0)),
            pl.BlockSpec((LANES, LANES), lambda i: (0, 0)),
        ],
        out_specs=pl.BlockSpec((tile, LANES), lambda i: (i, 0)),
        out_shape=jax.ShapeDtypeStruct((n_rows, LANES), BF16),
        compiler_params=_params("arbitrary"),
        name=name,
    )(a2, a2n, pe_row(pe[:NSA_CMP_STRIDE]), pe_row(pe[NSA_CMP_STRIDE:]),
      expand(w1[:NSA_CMP_STRIDE]), expand(w1[NSA_CMP_STRIDE:]), w2bd)


def _nsa_head_query(q, h, lane):
    colv = q[:, (h % 4) * LANES:(h % 4 + 1) * LANES]
    keep = (lane < HEAD_DIM) if h < NSA_GROUP_HEADS else (lane >= HEAD_DIM)
    return jnp.where(keep, colv, jnp.zeros_like(colv))


def _nsa_stacked_queries(q, lane):
    return jnp.concatenate([_nsa_head_query(q, h, lane) for h in range(NSA_HEADS)], axis=0)


def _nsa_interleave_heads(o, gates, gate0, lane, out_ref):
    tq = out_ref.shape[0]
    outs = [o[h * tq:(h + 1) * tq] * gates[:, gate0 + h:gate0 + h + 1] for h in range(NSA_HEADS)]
    for j in range(4):
        out_ref[:, j * LANES:(j + 1) * LANES] = jnp.where(lane < HEAD_DIM, outs[j], outs[4 + j])


def _values_with_ones(v):
    ones = jnp.ones((v.shape[0], HEAD_DIM), v.dtype)
    return jnp.concatenate([v[:, :HEAD_DIM], ones], axis=1), jnp.concatenate([ones, v[:, HEAD_DIM:]], axis=1)


def _top_k_flags(x, idx, k):
    sel = jnp.zeros(x.shape, jnp.bool_)
    for _ in range(k):
        mx = jnp.max(x, axis=0, keepdims=True)
        first = jnp.min(jnp.where(x == mx, idx, x.shape[0]), axis=0, keepdims=True)
        hit = idx == first
        sel = sel | hit
        x = jnp.where(hit, -jnp.inf, x)
    return sel


CMP_KEY_BUCKET = 256


def _nsa_cmp_kernel(q_ref, kc_ref, vca_ref, vcb_ref, ovt_ref, gate_ref, oc_ref, bias_ref, *, k_sel):
    tq = q_ref.shape[0]
    n_cmp = kc_ref.shape[0]
    n_selp = ovt_ref.shape[0]
    rows = NSA_HEADS * tq
    half = rows // 2
    t0 = pl.program_id(0) * tq
    lane = lax.broadcasted_iota(jnp.int32, (tq, LANES), 1)

    def attend(nk):
        t_rows = t0 + (lax.broadcasted_iota(jnp.int32, (rows, 1), 0) & (tq - 1))
        cmp_end = lax.broadcasted_iota(jnp.int32, (1, nk), 1) * NSA_CMP_STRIDE + (NSA_CMP_LEN - 1)
        s = _dot_nt(_nsa_stacked_queries(q_ref[...], lane), kc_ref[0:nk, :])
        s = jnp.where(cmp_end <= t_rows, s, NEG_INF)
        m = jnp.max(s, axis=-1, keepdims=True)
        p = jnp.exp2(s - m)
        pb = p.astype(BF16)
        acc_a = _dot(pb[:half], vca_ref[0:nk, :])
        acc_b = _dot(pb[half:], vcb_ref[0:nk, :])
        l = jnp.concatenate([acc_a[:, HEAD_DIM:HEAD_DIM + 1], acc_b[:, 0:1]], axis=0)
        rl = jnp.where(m > 0.5 * NEG_INF, 1.0 / jnp.maximum(l, 1e-30), 0.0)
        _nsa_interleave_heads(jnp.concatenate([acc_a, acc_b], axis=0) * rl, gate_ref[...], 0, lane, oc_ref)
        pr = p * rl
        nb = min(n_selp, nk * NSA_CMP_STRIDE // NSA_SEL_LEN)
        blk = lax.broadcasted_iota(jnp.int32, (nb, tq), 0)
        cur = (t0 + lax.broadcasted_iota(jnp.int32, (1, tq), 1)) >> 6
        valid = blk <= cur
        forced = (blk == 0) | (blk == cur) | (blk == cur - 1)
        for g in range(2):
            r0 = g * half
            pg = pr[r0:r0 + tq] + pr[r0 + tq:r0 + 2 * tq] + pr[r0 + 2 * tq:r0 + 3 * tq] + pr[r0 + 3 * tq:r0 + 4 * tq]
            imp_t = _dot_nt(ovt_ref[0:nb, 0:nk], pg.astype(BF16))
            x = jnp.where(valid & jnp.logical_not(forced), imp_t, NEG_INF)
            sel = _top_k_flags(x, blk, k_sel - 3) | forced
            bias_t = jnp.where(sel & valid, 0.0, MASK_BIAS)
            if nb < n_selp:
                bias_t = jnp.concatenate([bias_t, jnp.full((n_selp - nb, tq), MASK_BIAS, F32)], axis=0)
            bias_ref[g] = bias_t.T.astype(BF16)

    needed = (t0 + tq - NSA_CMP_LEN) // NSA_CMP_STRIDE + 1
    buckets = sorted({min(b, n_cmp) for b in range(CMP_KEY_BUCKET, n_cmp + CMP_KEY_BUCKET, CMP_KEY_BUCKET)})
    for i, nk in enumerate(buckets):
        lo = buckets[i - 1] if i else -(1 << 30)
        cond = (needed > lo) if i == len(buckets) - 1 else ((needed > lo) & (needed <= nk))
        pl.when(cond)(functools.partial(attend, nk))


def _nsa_cmp(nq, kc, vc, ov, gates, k_sel):
    T = nq.shape[0]
    n_cmp = kc.shape[0]
    n_selp = ov.shape[0]
    vca, vcb = _values_with_ones(vc)
    return pl.pallas_call(
        functools.partial(_nsa_cmp_kernel, k_sel=k_sel),
        grid=(T // Q_TILE,),
        in_specs=[
            pl.BlockSpec((Q_TILE, 512), lambda i: (i, 0)),
            pl.BlockSpec((n_cmp, LANES), lambda i: (0, 0)),
            pl.BlockSpec((n_cmp, LANES), lambda i: (0, 0)),
            pl.BlockSpec((n_cmp, LANES), lambda i: (0, 0)),
            pl.BlockSpec((n_selp, n_cmp), lambda i: (0, 0)),
            pl.BlockSpec((Q_TILE, LANES), lambda i: (i, 0)),
        ],
        out_specs=[
            pl.BlockSpec((Q_TILE, 512), lambda i: (i, 0)),
            pl.BlockSpec((2, Q_TILE, n_selp), lambda i: (0, i, 0)),
        ],
        out_shape=[
            jax.ShapeDtypeStruct((T, 512), F32),
            jax.ShapeDtypeStruct((2, T, n_selp), BF16),
        ],
        compiler_params=_params("arbitrary"),
        name="nsa_cmp_select",
    )(nq, kc, vca, vcb, ov, gates)


def _nsa_win_kernel(q_ref, kw_ref, va_ref, vb_ref, gate_ref, ow_ref):
    tq = q_ref.shape[0]
    rows = NSA_HEADS * tq
    half = rows // 2
    span = NSA_WINDOW + tq
    t0 = pl.program_id(0) * tq
    start = pl.multiple_of(jnp.maximum(t0 - NSA_WINDOW, 0), tq)
    lane = lax.broadcasted_iota(jnp.int32, (tq, LANES), 1)
    t_rows = t0 + (lax.broadcasted_iota(jnp.int32, (rows, 1), 0) & (tq - 1))
    wpos = start + lax.broadcasted_iota(jnp.int32, (1, span), 1)
    s = _dot_nt(_nsa_stacked_queries(q_ref[...], lane), kw_ref[pl.ds(start, span), :])
    s = jnp.where((wpos <= t_rows) & (wpos > t_rows - NSA_WINDOW), s, NEG_INF)
    pb = jnp.exp2(s - jnp.max(s, axis=-1, keepdims=True)).astype(BF16)
    acc = jnp.concatenate([_dot(pb[:half], va_ref[pl.ds(start, span), :]),
                           _dot(pb[half:], vb_ref[pl.ds(start, span), :])], axis=0)
    _nsa_interleave_heads(acc / pltpu.roll(acc, HEAD_DIM, 1), gate_ref[...], 2 * NSA_HEADS, lane, ow_ref)


def _nsa_win(nq, kw, vw, gates):
    T = nq.shape[0]
    va, vb = vw if isinstance(vw, tuple) else _values_with_ones(vw)
    full = pl.BlockSpec((T, LANES), lambda i: (0, 0))
    return pl.pallas_call(
        _nsa_win_kernel,
        grid=(T // Q_TILE,),
        in_specs=[pl.BlockSpec((Q_TILE, 512), lambda i: (i, 0)), full, full, full,
                  pl.BlockSpec((Q_TILE, LANES), lambda i: (i, 0))],
        out_specs=pl.BlockSpec((Q_TILE, 512), lambda i: (i, 0)),
        out_shape=jax.ShapeDtypeStruct((T, 512), F32),
        compiler_params=_params("arbitrary"),
        name="nsa_window",
    )(nq, kw, va, vb, gates)


def _flash_scratch(rows):
    return [
        pltpu.VMEM((rows, K_TILE), F32), pltpu.VMEM((rows, K_TILE), F32),
        pltpu.VMEM((rows, K_TILE), BF16), pltpu.VMEM((rows, K_TILE), BF16),
        pltpu.VMEM((rows, 1), F32), pltpu.VMEM((rows, 1), F32),
        pltpu.VMEM((rows, 1), F32),
        pltpu.VMEM((rows, LANES), F32),
    ]


def _flash_body(lhs_ref, k_ref, e_ref, va_ref, vb_ref, t_of_rows, n_full, tokens_per_half, scratch):
    s_a, s_b, p_a, p_b, al_a, al_b, m_ref, acc_ref = scratch
    tk = K_TILE
    rows = acc_ref.shape[0]
    half_rows = rows // 2
    last_mem_tile = k_ref.shape[0] // tk - 1

    def tile_start(j):
        return pl.multiple_of(jnp.clip(j, 0, last_mem_tile) * tk, tk)

    def scores(j, s_out):
        k0 = tile_start(j)
        rhs = jnp.concatenate([k_ref[pl.ds(k0, tk), :], e_ref[pl.ds(k0, tk), :]], axis=1)
        s_out[...] = _dot_nt(lhs_ref[k0 // tokens_per_half], rhs)

    def softmax(j, s_in, p_out, al_out, causal):
        s = s_in[...]
        if causal:
            kpos = j * tk + lax.broadcasted_iota(jnp.int32, (1, tk), 1)
            s = jnp.where(kpos <= t_of_rows(0, rows), s, NEG_INF)
        m_old = m_ref[...]
        m_new = jnp.maximum(m_old, jnp.max(s, axis=-1, keepdims=True))
        m_ref[...] = m_new
        al_out[...] = jnp.exp2(m_old - m_new)
        p_out[...] = jnp.exp2(s - m_new).astype(BF16)

    def values(j, p_in, al_in):
        k0 = tile_start(j)
        for rs, v_ref in ((slice(0, half_rows), va_ref), (slice(half_rows, rows), vb_ref)):
            acc_ref[rs, :] = al_in[rs, :] * acc_ref[rs, :] + _dot(p_in[rs, :], v_ref[pl.ds(k0, tk), :])

    m_ref[...] = jnp.full(m_ref.shape, NEG_INF, F32)
    acc_ref[...] = jnp.zeros(acc_ref.shape, F32)
    p_b[...] = jnp.zeros(p_b.shape, BF16)
    al_b[...] = jnp.ones(al_b.shape, F32)
    scores(0, s_a)

    def two_full_tiles(i, carry):
        j = 2 * i
        values(j - 1, p_b, al_b)
        scores(j + 1, s_b)
        softmax(j, s_a, p_a, al_a, False)
        values(j, p_a, al_a)
        scores(j + 2, s_a)
        softmax(j + 1, s_b, p_b, al_b, False)
        return carry

    n_pairs = n_full // 2
    lax.fori_loop(0, n_pairs, two_full_tiles, 0)
    j = 2 * n_pairs
    odd = (n_full & 1) == 1

    @pl.when(odd)
    def _():
        values(j - 1, p_b, al_b)
        scores(j + 1, s_b)
        softmax(j, s_a, p_a, al_a, True)
        values(j, p_a, al_a)
        softmax(j + 1, s_b, p_b, al_b, True)
        values(j + 1, p_b, al_b)

    @pl.when(jnp.logical_not(odd))
    def _():
        values(j - 1, p_b, al_b)
        softmax(j, s_a, p_a, al_a, True)
        values(j, p_a, al_a)

    acc = acc_ref[...]
    return acc / pltpu.roll(acc, HEAD_DIM, 1)


def _nsa_sel_kernel(q_ref, bias_ref, ks_ref, va_ref, vb_ref, e_ref, gate_ref, oc_ref, ow_ref, sz_ref, out_ref,
                    lhs_ref, *scratch):
    tq = q_ref.shape[0]
    t0 = pl.program_id(0) * tq
    lane = lax.broadcasted_iota(jnp.int32, (tq, LANES), 1)
    q = q_ref[...]
    for half in range(lhs_ref.shape[0]):
        for h in range(NSA_HEADS):
            b = bias_ref[h // NSA_GROUP_HEADS, :, half * LANES:(half + 1) * LANES]
            lhs_ref[half, h * tq:(h + 1) * tq, :] = jnp.concatenate([_nsa_head_query(q, h, lane), b], axis=1)

    def t_of_rows(r0, n):
        return t0 + ((r0 + lax.broadcasted_iota(jnp.int32, (n, 1), 0)) & (tq - 1))

    o = _flash_body(lhs_ref, ks_ref, e_ref, va_ref, vb_ref, t_of_rows, t0 // K_TILE, NSA_SEL_LEN * LANES, scratch)
    gates = gate_ref[...]
    outs = [o[h * tq:(h + 1) * tq] * gates[:, NSA_HEADS + h:NSA_HEADS + h + 1] for h in range(NSA_HEADS)]
    for j in range(4):
        cs = slice(j * LANES, (j + 1) * LANES)
        a = jnp.where(lane < HEAD_DIM, outs[j], outs[4 + j]) + oc_ref[:, cs] + ow_ref[:, cs]
        out_ref[:, cs] = (a * sz_ref[:, cs].astype(F32)).astype(out_ref.dtype)


def _nsa_sel(nq, bias, ks, vs, e_sel, gates, oc, ow, sz):
    T = nq.shape[0]
    n_selp = bias.shape[2]
    rows = NSA_HEADS * Q_TILE
    row = lambda w: pl.BlockSpec((Q_TILE, w), lambda qi: (qi, 0))
    full = pl.BlockSpec((T, LANES), lambda qi: (0, 0))
    va, vb = vs if isinstance(vs, tuple) else _values_with_ones(vs)
    return pl.pallas_call(
        _nsa_sel_kernel,
        grid=(T // Q_TILE,),
        in_specs=[row(512), pl.BlockSpec((2, Q_TILE, n_selp), lambda qi: (0, qi, 0)), full, full, full, full,
                  row(LANES), row(512), row(512), row(512)],
        out_specs=row(512),
        out_shape=jax.ShapeDtypeStruct((T, 512), BF16),
        scratch_shapes=[pltpu.VMEM((n_selp // LANES, rows, 2 * LANES), BF16)] + _flash_scratch(rows),
        compiler_params=_params("arbitrary"),
        name="nsa_selected",
    )(nq, bias, ks, va, vb, e_sel, gates, oc, ow, sz)


MOBA_Q_TILE = 512


def _moba_kernel(q_ref, kmean_ref, k_ref, va_ref, vb_ref, e_ref, sz_ref, out_ref, lhs_ref, *scratch, top):
    tq = q_ref.shape[0]
    t0 = pl.program_id(1) * tq
    lane = lax.broadcasted_iota(jnp.int32, (tq, LANES), 1)
    colv = q_ref[...]
    zero = jnp.zeros_like(colv)
    qp = jnp.concatenate([jnp.where(lane < HEAD_DIM, colv, zero), jnp.where(lane >= HEAD_DIM, colv, zero)], axis=0)

    def t_of_rows(r0, n):
        return t0 + ((r0 + lax.broadcasted_iota(jnp.int32, (n, 1), 0)) & (tq - 1))

    n_blk_rows = kmean_ref.shape[0]
    cur = (t0 + (lax.broadcasted_iota(jnp.int32, (1, 2 * tq), 1) & (tq - 1))) >> 8
    blk = lax.broadcasted_iota(jnp.int32, (n_blk_rows, 2 * tq), 0)
    past = blk < cur
    gate = jnp.where(past, _dot_nt(kmean_ref[...], qp), NEG_INF)
    sel = _top_k_flags(gate, blk, top)
    bias_t = jnp.where((sel & past) | (blk == cur), 0.0, MASK_BIAS)
    if n_blk_rows < LANES:
        bias_t = jnp.concatenate([bias_t, jnp.full((LANES - n_blk_rows, 2 * tq), MASK_BIAS, F32)], axis=0)
    lhs_ref[0] = jnp.concatenate([qp, bias_t.T.astype(BF16)], axis=1)
    o = _flash_body(lhs_ref, k_ref, e_ref, va_ref, vb_ref, t_of_rows, t0 // K_TILE, MOBA_BLOCK * LANES, scratch)
    a = jnp.where(lane < HEAD_DIM, o[:tq], o[tq:])
    out_ref[...] = (a * sz_ref[...].astype(F32)).astype(out_ref.dtype)


def _moba(mq, kmean, mk, mv, e_blk, sz, top):
    T = mq.shape[0]
    tq = MOBA_Q_TILE
    assert MOBA_BLOCK == 256 and T * 1 <= MOBA_BLOCK * LANES
    rows = 2 * tq
    row = pl.BlockSpec((tq, LANES), lambda p, qi: (qi, p))
    col = pl.BlockSpec((T, LANES), lambda p, qi: (0, p))
    if isinstance(mv, tuple):
        va, vb = mv
    else:
        v4 = mv.reshape(T, MOBA_HEADS // 2, 2, HEAD_DIM)
        ones = jnp.ones((T, MOBA_HEADS // 2, HEAD_DIM), mv.dtype)
        va = jnp.stack([v4[:, :, 0], ones], axis=2).reshape(T, 512)
        vb = jnp.stack([ones, v4[:, :, 1]], axis=2).reshape(T, 512)
    return pl.pallas_call(
        functools.partial(_moba_kernel, top=top),
        grid=(MOBA_HEADS // 2, T // tq),
        in_specs=[row, pl.BlockSpec((kmean.shape[0], LANES), lambda p, qi: (0, p)), col, col, col,
                  pl.BlockSpec((T, LANES), lambda p, qi: (0, 0)), row],
        out_specs=row,
        out_shape=jax.ShapeDtypeStruct((T, 512), BF16),
        scratch_shapes=[pltpu.VMEM((1, rows, 2 * LANES), BF16)] + _flash_scratch(rows),
        compiler_params=_params("arbitrary", "arbitrary"),
        name="moba",
    )(mq, kmean, mk, va, vb, e_blk, sz)


def _mem_kv_kernel(mem_ref, g_ref, w0_ref, w1_ref, kv0_ref, kv1_ref):
    mem_n = _rms(mem_ref[...], g_ref[...]).astype(BF16)
    kv0_ref[...] = _dot(mem_n, w0_ref[...]).astype(kv0_ref.dtype)
    kv1_ref[...] = _dot(mem_n, w1_ref[...]).astype(kv1_ref.dtype)


def _mem_kv(mem, g, w0, w1):
    n = mem.shape[0]
    kv_w = 2 * MEM_HEADS * HEAD_DIM
    return pl.pallas_call(
        _mem_kv_kernel,
        grid=(1,),
        in_specs=[
            pl.BlockSpec((n, D_MODEL), lambda i: (0, 0)),
            pl.BlockSpec((1, D_MODEL), lambda i: (0, 0)),
            pl.BlockSpec((D_MODEL, kv_w), lambda i: (0, 0)),
            pl.BlockSpec((D_MODEL, kv_w), lambda i: (0, 0)),
        ],
        out_specs=[pl.BlockSpec((n, kv_w), lambda i: (0, 0)), pl.BlockSpec((n, kv_w), lambda i: (0, 0))],
        out_shape=[jax.ShapeDtypeStruct((n, kv_w), BF16), jax.ShapeDtypeStruct((n, kv_w), BF16)],
        compiler_params=_params("arbitrary"),
        name="mem_kv",
    )(mem, g, w0, w1)


def _mem_attention(eq, kv, sez, lane):
    rows = eq.shape[0]
    mem_w = MEM_HEADS * HEAD_DIM
    cols = []
    for p in range(MEM_HEADS // 2):
        cs = slice(p * LANES, (p + 1) * LANES)
        colv = eq[:, cs]
        zero = jnp.zeros_like(colv)
        qp = jnp.concatenate([jnp.where(lane < HEAD_DIM, colv, zero), jnp.where(lane >= HEAD_DIM, colv, zero)], axis=0)
        s = _dot_nt(qp, kv[:, cs])
        pe = jnp.exp(s - jnp.max(s, axis=-1, keepdims=True))
        o = _dot(pe.astype(BF16), kv[:, mem_w + p * LANES:mem_w + (p + 1) * LANES]) / jnp.sum(pe, axis=-1, keepdims=True)
        cols.append(jnp.where(lane < HEAD_DIM, o[:rows], o[rows:]) * sez[:, cs].astype(F32))
    return jnp.concatenate(cols, axis=1).astype(BF16)


def _out0_kernel(ag_ref, bg_ref, eq_ref, sez_ref, kv_ref, wa_ref, wb_ref, wm_ref, x_ref, out_ref):
    rows = x_ref.shape[0]
    lane = lax.broadcasted_iota(jnp.int32, (rows, LANES), 1)
    mg = _mem_attention(eq_ref[...], kv_ref[...], sez_ref[...], lane)
    y = _dot(ag_ref[...], wa_ref[...]) + _dot(bg_ref[...], wb_ref[...]) + _dot(mg, wm_ref[...])
    out_ref[...] = x_ref[...] + y


def _out0(ag, bg, eq, sez, kv, wa, wb, wm, x):
    T = x.shape[0]
    rt = PROJ_ROWS
    row = lambda w: pl.BlockSpec((rt, w), lambda i: (i, 0))
    full = lambda a: pl.BlockSpec(a.shape, lambda i: (0, 0))
    return pl.pallas_call(
        _out0_kernel,
        grid=(T // rt,),
        in_specs=[row(512), row(512), row(256), row(256), full(kv), full(wa), full(wb), full(wm), row(D_MODEL)],
        out_specs=row(D_MODEL),
        out_shape=jax.ShapeDtypeStruct((T, D_MODEL), F32),
        compiler_params=_params("arbitrary"),
        name="out_proj0",
    )(ag, bg, eq, sez, kv, wa, wb, wm, x)


def _out1_kernel(rg_ref, eq_ref, sez_ref, kv_ref, wr_ref, wm_ref, x_ref, g_ref, out_ref):
    rows = x_ref.shape[0]
    lane = lax.broadcasted_iota(jnp.int32, (rows, LANES), 1)
    mg = _mem_attention(eq_ref[...], kv_ref[...], sez_ref[...], lane)
    y = _dot(rg_ref[...], wr_ref[...]) + _dot(mg, wm_ref[...])
    out_ref[...] = _rms(x_ref[...] + y, g_ref[...])


def _out1(rg, eq, sez, kv, wr, wm, x, g):
    T = x.shape[0]
    rt = PROJ_ROWS
    row = lambda w: pl.BlockSpec((rt, w), lambda i: (i, 0))
    full = lambda a: pl.BlockSpec(a.shape, lambda i: (0, 0))
    return pl.pallas_call(
        _out1_kernel,
        grid=(T // rt,),
        in_specs=[row(RET_HEADS * RET_V_DIM), row(256), row(256), full(kv), full(wr), full(wm), row(D_MODEL), full(g)],
        out_specs=row(D_MODEL),
        out_shape=jax.ShapeDtypeStruct((T, D_MODEL), F32),
        compiler_params=_params("arbitrary"),
        name="out_proj1_norm",
    )(rg, eq, sez, kv, wr, wm, x, g)


def _retention_kernel(q_ref, k_ref, v_ref, sz_ref, decay_ref, qd_ref, kd_ref, cd_ref, out_ref, state_ref):
    @pl.when(pl.program_id(0) == 0)
    def _():
        state_ref[...] = jnp.zeros(state_ref.shape, F32)

    for h in range(RET_HEADS):
        qs = slice(h * RET_QK_DIM, (h + 1) * RET_QK_DIM)
        vsl = slice(h * RET_V_DIM, (h + 1) * RET_V_DIM)
        qh = q_ref[:, qs]
        kh = k_ref[:, qs]
        vh = v_ref[:, vsl]
        st = state_ref[h]
        inner = _dot_nt(qh, kh.astype(BF16)) * decay_ref[h]
        o = _dot(inner.astype(BF16), vh) + _dot(qh, st.astype(BF16)) * qd_ref[h]
        state_ref[h] = st * cd_ref[h] + _dot_tn((kh * kd_ref[h]).astype(BF16), vh)
        mu = jnp.mean(o, axis=-1, keepdims=True)
        d = o - mu
        on = d * lax.rsqrt(jnp.mean(d * d, axis=-1, keepdims=True) + NORM_EPS)
        out_ref[:, vsl] = (on * sz_ref[:, vsl].astype(F32)).astype(out_ref.dtype)


def _retention(rq, rk, rv, sz):
    T = rq.shape[0]
    C = RET_CHUNK
    log_g = jnp.log(1.0 - 2.0 ** (-5.0 - jnp.arange(RET_HEADS, dtype=F32)))
    i = jnp.arange(C, dtype=F32)
    diff = i[:, None] - i[None, :]
    decay = jnp.where(diff >= 0, jnp.exp(jnp.maximum(diff, 0.0)[None] * log_g[:, None, None]), 0.0)
    q_decay = jnp.exp((i + 1.0)[None, :] * log_g[:, None])[:, :, None]
    k_decay = jnp.exp((C - 1.0 - i)[None, :] * log_g[:, None])[:, :, None]
    chunk_decay = jnp.exp(C * log_g)[:, None, None]
    qk_w = RET_HEADS * RET_QK_DIM
    v_w = RET_HEADS * RET_V_DIM
    return pl.pallas_call(
        _retention_kernel,
        grid=(T // C,),
        in_specs=[
            pl.BlockSpec((C, qk_w), lambda c: (c, 0)),
            pl.BlockSpec((C, qk_w), lambda c: (c, 0)),
            pl.BlockSpec((C, v_w), lambda c: (c, 0)),
            pl.BlockSpec((C, v_w), lambda c: (c, 0)),
            pl.BlockSpec((RET_HEADS, C, C), lambda c: (0, 0, 0)),
            pl.BlockSpec((RET_HEADS, C, 1), lambda c: (0, 0, 0)),
            pl.BlockSpec((RET_HEADS, C, 1), lambda c: (0, 0, 0)),
            pl.BlockSpec((RET_HEADS, 1, 1), lambda c: (0, 0, 0)),
        ],
        out_specs=pl.BlockSpec((C, v_w), lambda c: (c, 0)),
        out_shape=jax.ShapeDtypeStruct((T, v_w), BF16),
        scratch_shapes=[pltpu.VMEM((RET_HEADS, RET_QK_DIM, RET_V_DIM), F32)],
        compiler_params=_params("arbitrary"),
        name="retention",
    )(rq, rk, rv, sz, decay, q_decay, k_decay, chunk_decay)


def _head_cols(order):
    return np.concatenate([np.arange(h * HEAD_DIM, (h + 1) * HEAD_DIM) for h in order])


def _one_hot_blocks(T, block):
    b = (np.arange(T) // block) % LANES
    return jnp.asarray(np.eye(LANES, dtype=np.float32)[b], dtype=BF16)


def kernel(x, mem, positions, l0_norm_g, l0_w_in, l0_nsa_gate_b, l0_cmp_pe_k, l0_cmp_w1_k, l0_cmp_w2_k,
           l0_cmp_pe_v, l0_cmp_w1_v, l0_cmp_w2_v, l0_w_mem_kv, l0_w_out,
           l1_norm_g, l1_w_in, l1_w_mem_kv, l1_w_out, mem_norm_g, final_norm_g):
    B, T, _ = x.shape
    assert B == 1 and T % K_TILE == 0 and T >= NSA_WINDOW + Q_TILE
    n_sel = T // NSA_SEL_LEN
    n_selp = -(-n_sel // LANES) * LANES
    k_sel = min(NSA_SEL_TOPK, n_sel)
    assert k_sel >= 3
    n_blk = T // MOBA_BLOCK
    assert n_blk <= LANES
    moba_top = max(1, min(MOBA_TOPK, n_blk - 1))
    x2 = x[0]
    pos = positions[0].astype(F32)

    attn_inv = 1.0 / (ROPE_THETA ** (jnp.arange(0, HEAD_DIM, 2, dtype=F32) / HEAD_DIM))
    ret_inv = 1.0 / (ROPE_THETA ** jnp.linspace(0.0, 1.0, RET_QK_DIM // 2, dtype=F32))
    ang = pos[:, None] * attn_inv
    cos64 = jnp.tile(jnp.cos(ang), (1, 4))
    sin64 = jnp.tile(jnp.concatenate([-jnp.sin(ang), jnp.sin(ang)], axis=1), (1, 2))
    rang = pos[:, None] * ret_inv
    cos256, sin256 = jnp.cos(rang), jnp.sin(rang)

    offs = np.cumsum([0, 512, 128, 128, 128, 128, 128, 128, 24, 512, 512, 512, 512, 512, 256, 256])
    nsa_cols = _head_cols(NSA_HEAD_ORDER)
    col_idx = np.concatenate([
        offs[0] + nsa_cols,
        np.arange(offs[1], offs[7]),
    ])
    w0 = jnp.concatenate([
        l0_w_in[:, col_idx],
        l0_w_in[:, offs[7]:offs[8]], jnp.zeros((D_MODEL, LANES - 24), F32),
        l0_w_in[:, offs[8] + nsa_cols],
        l0_w_in[:, offs[9]:],
    ], axis=1).astype(BF16)
    gate_b = jnp.concatenate([l0_nsa_gate_b, jnp.zeros((LANES - 24,), F32)])[None, :]
    groups0 = [
        (512, "rot64", LOG2_SCALE), (128, "rot64", 1.0), (128, "plain", 1.0), (128, "rot64", 1.0),
        (128, "ones_pair", 1.0), (128, "rot64", 1.0), (128, "ones_pair", 1.0), (128, "gate", 1.0), (512, "silu", 1.0),
        (512, "rot64", LOG2_SCALE), (512, "rot64_mean", 1.0), (512, "ones_pair", 1.0), (512, "silu", 1.0),
        (256, "plain", 0.125), (256, "silu", 1.0),
    ]
    dts0 = [BF16] * 7 + [F32] + [BF16] * 7
    (nq, nkc, nvc, nks, nvs_a, nvs_b, nkw, nvw_a, nvw_b, gates, snz, mq, mk, kmean, mv_a, mv_b, smz, eq0, sez0) = _proj(
        x2, l0_norm_g[None, :], w0, cos64, sin64, gate_b, groups0, dts0, "proj0")

    kv0, kv1 = _mem_kv(mem[0], mem_norm_g[None, :], l0_w_mem_kv.astype(BF16), l1_w_mem_kv.astype(BF16))

    kc = _compress(nkc, l0_cmp_pe_k, l0_cmp_w1_k, l0_cmp_w2_k, "compress_k")
    vc = _compress(nvc, l0_cmp_pe_v, l0_cmp_w1_v, l0_cmp_w2_v, "compress_v")
    n_cmp_rows = T // NSA_CMP_STRIDE
    cmp_start = np.arange(n_cmp_rows)[:, None] * NSA_CMP_STRIDE
    sel_start = np.arange(n_selp)[None, :] * NSA_SEL_LEN
    overlap = ((cmp_start < sel_start + NSA_SEL_LEN) & (cmp_start + NSA_CMP_LEN > sel_start)
               & (np.arange(n_selp)[None, :] < n_sel))
    ov = jnp.asarray(overlap.T.astype(np.float32), dtype=BF16)
    oc, sel_bias = _nsa_cmp(nq, kc, vc, ov, gates, k_sel)
    ow = _nsa_win(nq, nkw, (nvw_a, nvw_b), gates)
    ag = _nsa_sel(nq, sel_bias, nks, (nvs_a, nvs_b), _one_hot_blocks(T, NSA_SEL_LEN), gates, oc, ow, snz)

    n_blk_rows = -(-n_blk // 16) * 16
    kmean_p = jnp.concatenate([kmean[:, 0, :], jnp.zeros((n_blk_rows - n_blk, 512), F32)], axis=0).astype(BF16)
    bg = _moba(mq, kmean_p, mk, (mv_a, mv_b), _one_hot_blocks(T, MOBA_BLOCK), smz, moba_top)

    wa = l0_w_out[:512][nsa_cols].astype(BF16)
    wb = l0_w_out[512:1024].astype(BF16)
    wm0 = l0_w_out[1024:].astype(BF16)
    x1 = _out0(ag, bg, eq0, sez0, kv0, wa, wb, wm0, x2)

    groups1 = [
        (1024, "rot256", 1.0), (1024, "rot256", RET_QK_DIM ** -0.5), (2048, "plain", 1.0), (2048, "silu", 1.0),
        (256, "plain", 0.125), (256, "silu", 1.0),
    ]
    dts1 = [BF16, F32, BF16, BF16, BF16, BF16]
    rq, rk, rv, srz, eq1, sez1 = _proj(x1, l1_norm_g[None, :], l1_w_in.astype(BF16), cos256, sin256,
                                       jnp.zeros((1, LANES), F32), groups1, dts1, "proj1")
    rg = _retention(rq, rk, rv, srz)
    v_w = RET_HEADS * RET_V_DIM
    out = _out1(rg, eq1, sez1, kv1, l1_w_out[:v_w].astype(BF16), l1_w_out[v_w:].astype(BF16), x1,
                final_norm_g[None, :])
    return out[None]
```

```python
import functools

import numpy as np
import jax
import jax.numpy as jnp
from jax import lax
from jax.experimental import pallas as pl
from jax.experimental.pallas import tpu as pltpu

F32 = jnp.float32
BF16 = jnp.bfloat16

D_MODEL = 1024
HEAD_DIM = 64
ROPE_THETA = 10000.0
NORM_EPS = 1e-6
NEG_INF = -1e30
MASK_BIAS = NEG_INF
LOG2_SCALE = HEAD_DIM ** -0.5 * float(np.log2(np.e))

NSA_HEADS = 8
NSA_GROUP_HEADS = 4
NSA_CMP_LEN = 32
NSA_CMP_STRIDE = 16
NSA_SEL_LEN = 64
NSA_SEL_TOPK = 16
NSA_WINDOW = 512
MOBA_HEADS = 8
MOBA_BLOCK = 256
MOBA_TOPK = 3
MEM_LEN = 256
MEM_HEADS = 4
RET_HEADS = 4
RET_QK_DIM = 256
RET_V_DIM = 512
RET_CHUNK = 128

LANES = 128
PROJ_ROWS = 256
PROJ_COLS = 512
Q_TILE = 128
K_TILE = 512
VMEM_LIMIT = 56 * 1024 * 1024

NSA_HEAD_ORDER = (0, 4, 1, 5, 2, 6, 3, 7)


def _dot(a, b):
    return jnp.dot(a, b, preferred_element_type=F32)


def _dot_nt(a, b):
    return lax.dot_general(a, b, (((1,), (1,)), ((), ())), preferred_element_type=F32)


def _dot_tn(a, b):
    return lax.dot_general(a, b, (((0,), (0,)), ((), ())), preferred_element_type=F32)


def _silu(y):
    return y * (1.0 / (1.0 + jnp.exp(-y)))


def _rms(x, g):
    return x * lax.rsqrt(jnp.mean(x * x, axis=-1, keepdims=True) + NORM_EPS) * g


def _params(*sem):
    return pltpu.CompilerParams(dimension_semantics=sem, vmem_limit_bytes=VMEM_LIMIT)


def _proj_kernel(x_ref, g_ref, w_ref, c_ref, s_ref, b_ref, *out_refs, groups):
    h = _rms(x_ref[...], g_ref[...]).astype(BF16)
    rows = x_ref.shape[0]
    lane = lax.broadcasted_iota(jnp.int32, (rows, LANES), 1)
    low_half = (lane & 32) == 0
    pieces = []
    col = 0
    oi = 0
    for width, kind, scale in groups:
        out_ref = out_refs[oi]
        oi += 1
        mean_ref = None
        if kind in ("rot64_mean", "ones_pair"):
            mean_ref = out_refs[oi]
            oi += 1
        pw = RET_QK_DIM if kind == "rot256" else LANES
        for c0 in range(0, width, pw):
            pieces.append((col + c0, pw, kind, scale, out_ref, c0, mean_ref))
        col += width

    def epilogue(z, kind, scale):
        if kind in ("rot64", "rot64_mean"):
            partner = jnp.where(low_half, pltpu.roll(z, LANES - 32, 1), pltpu.roll(z, 32, 1))
            z = z * c_ref[...] + partner * s_ref[...]
        elif kind == "rot256":
            x1, x2 = z[:, :LANES], z[:, LANES:]
            z = jnp.concatenate([x1 * c_ref[...] - x2 * s_ref[...], x1 * s_ref[...] + x2 * c_ref[...]], axis=1)
        elif kind == "silu":
            z = _silu(z)
        elif kind == "gate":
            z = 1.0 / (1.0 + jnp.exp(-(z + b_ref[...])))
        return z if scale == 1.0 else z * scale

    i = 0
    while i < len(pieces):
        j, run_w = i, 0
        while j < len(pieces) and run_w + pieces[j][1] <= PROJ_COLS:
            run_w += pieces[j][1]
            j += 1
        c_start = pieces[i][0]
        y = _dot(h, w_ref[:, c_start:c_start + run_w])
        for pc, pw, kind, scale, out_ref, oc, mean_ref in pieces[i:j]:
            z = epilogue(y[:, pc - c_start:pc - c_start + pw], kind, scale)
            if kind == "ones_pair":
                out_ref[:, oc:oc + pw] = jnp.where(lane < HEAD_DIM, z, 1.0).astype(out_ref.dtype)
                mean_ref[:, oc:oc + pw] = jnp.where(lane >= HEAD_DIM, z, 1.0).astype(mean_ref.dtype)
                continue
            if mean_ref is not None:
                mean_ref[0, :, oc:oc + pw] = jnp.mean(z, axis=0, keepdims=True)
            out_ref[:, oc:oc + pw] = z.astype(out_ref.dtype)
        i = j


def _proj(x, g, w, cos_t, sin_t, bias, groups, out_dtypes, name):
    T = x.shape[0]
    n_tiles = T // PROJ_ROWS
    out_shapes = []
    out_specs = []
    for (width, kind, _), dt in zip(groups, out_dtypes):
        out_shapes.append(jax.ShapeDtypeStruct((T, width), dt))
        out_specs.append(pl.BlockSpec((PROJ_ROWS, width), lambda i: (i, 0)))
        if kind == "rot64_mean":
            out_shapes.append(jax.ShapeDtypeStruct((n_tiles, 1, width), F32))
            out_specs.append(pl.BlockSpec((1, 1, width), lambda i: (i, 0, 0)))
        if kind == "ones_pair":
            out_shapes.append(jax.ShapeDtypeStruct((T, width), dt))
            out_specs.append(pl.BlockSpec((PROJ_ROWS, width), lambda i: (i, 0)))
    return pl.pallas_call(
        functools.partial(_proj_kernel, groups=tuple(groups)),
        grid=(n_tiles,),
        in_specs=[
            pl.BlockSpec((PROJ_ROWS, D_MODEL), lambda i: (i, 0)),
            pl.BlockSpec((1, D_MODEL), lambda i: (0, 0)),
            pl.BlockSpec(w.shape, lambda i: (0, 0)),
            pl.BlockSpec((PROJ_ROWS, LANES), lambda i: (i, 0)),
            pl.BlockSpec((PROJ_ROWS, LANES), lambda i: (i, 0)),
            pl.BlockSpec((1, LANES), lambda i: (0, 0)),
        ],
        out_specs=out_specs,
        out_shape=out_shapes,
        compiler_params=_params("arbitrary"),
        name=name,
    )(x, g, w, cos_t, sin_t, bias)


def _compress_kernel(a_ref, an_ref, pea_ref, peb_ref, w1a_ref, w1b_ref, w2_ref, out_ref):
    xa = (a_ref[...].astype(F32) + pea_ref[...]).astype(BF16)
    xb = (an_ref[...].astype(F32) + peb_ref[...]).astype(BF16)
    hid = _silu(_dot(xa, w1a_ref[...]) + _dot(xb, w1b_ref[...]))
    out_ref[...] = _dot(hid.astype(BF16), w2_ref[...]).astype(out_ref.dtype)


def _compress(a, pe, w1, w2, name):
    T = a.shape[0]
    n_rows = T // NSA_CMP_STRIDE
    row_w = NSA_CMP_STRIDE * LANES
    a2 = a.reshape(n_rows, row_w)
    a2n = jnp.concatenate([a2[1:], jnp.zeros((1, row_w), a2.dtype)], axis=0)
    eye2 = jnp.eye(2, dtype=F32)

    def expand(w1_half):
        return jnp.einsum("lde,gh->lgdhe", w1_half, eye2).reshape(row_w, LANES).astype(BF16)

    def pe_row(pe_half):
        return jnp.broadcast_to(pe_half[:, None, :], (NSA_CMP_STRIDE, 2, HEAD_DIM)).reshape(1, row_w)

    w2bd = jnp.einsum("ef,gh->gehf", w2, eye2).reshape(LANES, LANES).astype(BF16)
    tile = min(256, n_rows)
    return pl.pallas_call(
        _compress_kernel,
        grid=(n_rows // tile,),
        in_specs=[
            pl.BlockSpec((tile, row_w), lambda i: (i, 0)),
            pl.BlockSpec((tile, row_w), lambda i: (i, 0)),
            pl.BlockSpec((1, row_w), lambda i: (0, 0)),
            pl.BlockSpec((1, row_w), lambda i: (0, 0)),
            pl.BlockSpec((row_w, LANES), lambda i: (0, 0)),
            pl.BlockSpec((row_w, LANES), lambda i: (0, 0)),
            pl.BlockSpec((LANES, LANES), lambda i: (0, 0)),
        ],
        out_specs=pl.BlockSpec((tile, LANES), lambda i: (i, 0)),
        out_shape=jax.ShapeDtypeStruct((n_rows, LANES), BF16),
        compiler_params=_params("arbitrary"),
        name=name,
    )(a2, a2n, pe_row(pe[:NSA_CMP_STRIDE]), pe_row(pe[NSA_CMP_STRIDE:]),
      expand(w1[:NSA_CMP_STRIDE]), expand(w1[NSA_CMP_STRIDE:]), w2bd)


def _nsa_head_query(q, h, lane):
    colv = q[:, (h % 4) * LANES:(h % 4 + 1) * LANES]
    keep = (lane < HEAD_DIM) if h < NSA_GROUP_HEADS else (lane >= HEAD_DIM)
    return jnp.where(keep, colv, jnp.zeros_like(colv))


def _nsa_stacked_queries(q, lane):
    return jnp.concatenate([_nsa_head_query(q, h, lane) for h in range(NSA_HEADS)], axis=0)


def _nsa_interleave_heads(o, gates, gate0, lane, out_ref, accumulate=False):
    tq = out_ref.shape[0]
    outs = [o[h * tq:(h + 1) * tq] * gates[:, gate0 + h:gate0 + h + 1] for h in range(NSA_HEADS)]
    for j in range(4):
        cs = slice(j * LANES, (j + 1) * LANES)
        a = jnp.where(lane < HEAD_DIM, outs[j], outs[4 + j])
        out_ref[:, cs] = out_ref[:, cs] + a if accumulate else a


def _values_with_ones(v):
    ones = jnp.ones((v.shape[0], HEAD_DIM), v.dtype)
    return jnp.concatenate([v[:, :HEAD_DIM], ones], axis=1), jnp.concatenate([ones, v[:, HEAD_DIM:]], axis=1)


def _top_k_flags(x, idx, k):
    sel = jnp.zeros(x.shape, jnp.bool_)
    for _ in range(k):
        mx = jnp.max(x, axis=0, keepdims=True)
        first = jnp.min(jnp.where(x == mx, idx, x.shape[0]), axis=0, keepdims=True)
        hit = idx == first
        sel = sel | hit
        x = jnp.where(hit, -jnp.inf, x)
    return sel


CMP_KEY_BUCKET = 256


def _nsa_cmp_kernel(q_ref, kc_ref, vca_ref, vcb_ref, ovt_ref, kw_ref, wa_ref, wb_ref, gate_ref, oc_ref, bias_ref,
                    *, k_sel):
    tq = q_ref.shape[0]
    n_cmp = kc_ref.shape[0]
    n_selp = ovt_ref.shape[0]
    rows = NSA_HEADS * tq
    half = rows // 2
    t0 = pl.program_id(0) * tq
    lane = lax.broadcasted_iota(jnp.int32, (tq, LANES), 1)
    t_rows = t0 + (lax.broadcasted_iota(jnp.int32, (rows, 1), 0) & (tq - 1))

    span = NSA_WINDOW + tq
    start = pl.multiple_of(jnp.maximum(t0 - NSA_WINDOW, 0), tq)
    wpos = start + lax.broadcasted_iota(jnp.int32, (1, span), 1)
    s = _dot_nt(_nsa_stacked_queries(q_ref[...], lane), kw_ref[pl.ds(start, span), :])
    s = jnp.where((wpos <= t_rows) & (wpos > t_rows - NSA_WINDOW), s, NEG_INF)
    pb = jnp.exp2(s - jnp.max(s, axis=-1, keepdims=True)).astype(BF16)
    acc = jnp.concatenate([_dot(pb[:half], wa_ref[pl.ds(start, span), :]),
                           _dot(pb[half:], wb_ref[pl.ds(start, span), :])], axis=0)
    _nsa_interleave_heads(acc / pltpu.roll(acc, HEAD_DIM, 1), gate_ref[...], 2 * NSA_HEADS, lane, oc_ref)

    def attend(nk):
        cmp_end = lax.broadcasted_iota(jnp.int32, (1, nk), 1) * NSA_CMP_STRIDE + (NSA_CMP_LEN - 1)
        s = _dot_nt(_nsa_stacked_queries(q_ref[...], lane), kc_ref[0:nk, :])
        s = jnp.where(cmp_end <= t_rows, s, NEG_INF)
        m = jnp.max(s, axis=-1, keepdims=True)
        p = jnp.exp2(s - m)
        pb = p.astype(BF16)
        acc_a = _dot(pb[:half], vca_ref[0:nk, :])
        acc_b = _dot(pb[half:], vcb_ref[0:nk, :])
        l = jnp.concatenate([acc_a[:, HEAD_DIM:HEAD_DIM + 1], acc_b[:, 0:1]], axis=0)
        rl = jnp.where(m > 0.5 * NEG_INF, 1.0 / jnp.maximum(l, 1e-30), 0.0)
        _nsa_interleave_heads(jnp.concatenate([acc_a, acc_b], axis=0) * rl, gate_ref[...], 0, lane, oc_ref,
                              accumulate=True)
        pr = p * rl
        nb = min(n_selp, nk * NSA_CMP_STRIDE // NSA_SEL_LEN)
        blk = lax.broadcasted_iota(jnp.int32, (nb, tq), 0)
        cur = (t0 + lax.broadcasted_iota(jnp.int32, (1, tq), 1)) >> 6
        valid = blk <= cur
        forced = (blk == 0) | (blk == cur) | (blk == cur - 1)
        for g in range(2):
            r0 = g * half
            pg = pr[r0:r0 + tq] + pr[r0 + tq:r0 + 2 * tq] + pr[r0 + 2 * tq:r0 + 3 * tq] + pr[r0 + 3 * tq:r0 + 4 * tq]
            imp_t = _dot_nt(ovt_ref[0:nb, 0:nk], pg.astype(BF16))
            x = jnp.where(valid & jnp.logical_not(forced), imp_t, NEG_INF)
            sel = _top_k_flags(x, blk, k_sel - 3) | forced
            bias_t = jnp.where(sel & valid, 0.0, MASK_BIAS)
            if nb < n_selp:
                bias_t = jnp.concatenate([bias_t, jnp.full((n_selp - nb, tq), MASK_BIAS, F32)], axis=0)
            bias_ref[g] = bias_t.T.astype(BF16)

    needed = (t0 + tq - NSA_CMP_LEN) // NSA_CMP_STRIDE + 1
    buckets = sorted({min(b, n_cmp) for b in range(CMP_KEY_BUCKET, n_cmp + CMP_KEY_BUCKET, CMP_KEY_BUCKET)})
    for i, nk in enumerate(buckets):
        lo = buckets[i - 1] if i else -(1 << 30)
        cond = (needed > lo) if i == len(buckets) - 1 else ((needed > lo) & (needed <= nk))
        pl.when(cond)(functools.partial(attend, nk))


def _nsa_cmp(nq, kc, vc, ov, kw, vw, gates, k_sel):
    T = nq.shape[0]
    n_cmp = kc.shape[0]
    n_selp = ov.shape[0]
    vca, vcb = _values_with_ones(vc)
    wa, wb = vw if isinstance(vw, tuple) else _values_with_ones(vw)
    cmp_full = pl.BlockSpec((n_cmp, LANES), lambda i: (0, 0))
    full = pl.BlockSpec((T, LANES), lambda i: (0, 0))
    return pl.pallas_call(
        functools.partial(_nsa_cmp_kernel, k_sel=k_sel),
        grid=(T // Q_TILE,),
        in_specs=[
            pl.BlockSpec((Q_TILE, 512), lambda i: (i, 0)),
            cmp_full, cmp_full, cmp_full,
            pl.BlockSpec((n_selp, n_cmp), lambda i: (0, 0)),
            full, full, full,
            pl.BlockSpec((Q_TILE, LANES), lambda i: (i, 0)),
        ],
        out_specs=[
            pl.BlockSpec((Q_TILE, 512), lambda i: (i, 0)),
            pl.BlockSpec((2, Q_TILE, n_selp), lambda i: (0, i, 0)),
        ],
        out_shape=[
            jax.ShapeDtypeStruct((T, 512), F32),
            jax.ShapeDtypeStruct((2, T, n_selp), BF16),
        ],
        compiler_params=_params("arbitrary"),
        name="nsa_cmp_win_select",
    )(nq, kc, vca, vcb, ov, kw, wa, wb, gates)


def _flash_scratch(rows):
    return [
        pltpu.VMEM((rows, K_TILE), F32), pltpu.VMEM((rows, K_TILE), F32),
        pltpu.VMEM((rows, K_TILE), BF16), pltpu.VMEM((rows, K_TILE), BF16),
        pltpu.VMEM((rows, 1), F32), pltpu.VMEM((rows, 1), F32),
        pltpu.VMEM((rows, 1), F32),
        pltpu.VMEM((rows, LANES), F32),
    ]


def _flash_body(lhs_ref, k_ref, e_ref, va_ref, vb_ref, t_of_rows, n_full, tokens_per_half, scratch):
    s_a, s_b, p_a, p_b, al_a, al_b, m_ref, acc_ref = scratch
    tk = K_TILE
    rows = acc_ref.shape[0]
    half_rows = rows // 2
    last_mem_tile = k_ref.shape[0] // tk - 1

    def tile_start(j):
        return pl.multiple_of(jnp.clip(j, 0, last_mem_tile) * tk, tk)

    def scores(j, s_out):
        k0 = tile_start(j)
        rhs = jnp.concatenate([k_ref[pl.ds(k0, tk), :], e_ref[pl.ds(k0, tk), :]], axis=1)
        s_out[...] = _dot_nt(lhs_ref[k0 // tokens_per_half], rhs)

    def softmax(j, s_in, p_out, al_out, causal):
        s = s_in[...]
        if causal:
            kpos = j * tk + lax.broadcasted_iota(jnp.int32, (1, tk), 1)
            s = jnp.where(kpos <= t_of_rows(0, rows), s, NEG_INF)
        m_old = m_ref[...]
        m_new = jnp.maximum(m_old, jnp.max(s, axis=-1, keepdims=True))
        m_ref[...] = m_new
        al_out[...] = jnp.exp2(m_old - m_new)
        p_out[...] = jnp.exp2(s - m_new).astype(BF16)

    def values(j, p_in, al_in):
        k0 = tile_start(j)
        for rs, v_ref in ((slice(0, half_rows), va_ref), (slice(half_rows, rows), vb_ref)):
            acc_ref[rs, :] = al_in[rs, :] * acc_ref[rs, :] + _dot(p_in[rs, :], v_ref[pl.ds(k0, tk), :])

    m_ref[...] = jnp.full(m_ref.shape, NEG_INF, F32)
    acc_ref[...] = jnp.zeros(acc_ref.shape, F32)
    p_b[...] = jnp.zeros(p_b.shape, BF16)
    al_b[...] = jnp.ones(al_b.shape, F32)
    scores(0, s_a)

    def two_full_tiles(i, carry):
        j = 2 * i
        values(j - 1, p_b, al_b)
        scores(j + 1, s_b)
        softmax(j, s_a, p_a, al_a, False)
        values(j, p_a, al_a)
        scores(j + 2, s_a)
        softmax(j + 1, s_b, p_b, al_b, False)
        return carry

    n_pairs = n_full // 2
    lax.fori_loop(0, n_pairs, two_full_tiles, 0)
    j = 2 * n_pairs
    odd = (n_full & 1) == 1

    @pl.when(odd)
    def _():
        values(j - 1, p_b, al_b)
        scores(j + 1, s_b)
        softmax(j, s_a, p_a, al_a, True)
        values(j, p_a, al_a)
        softmax(j + 1, s_b, p_b, al_b, True)
        values(j + 1, p_b, al_b)

    @pl.when(jnp.logical_not(odd))
    def _():
        values(j - 1, p_b, al_b)
        softmax(j, s_a, p_a, al_a, True)
        values(j, p_a, al_a)

    acc = acc_ref[...]
    return acc / pltpu.roll(acc, HEAD_DIM, 1)


def _nsa_sel_kernel(q_ref, bias_ref, ks_ref, va_ref, vb_ref, e_ref, gate_ref, ocw_ref, sz_ref, out_ref,
                    lhs_ref, *scratch):
    tq = q_ref.shape[0]
    t0 = pl.program_id(0) * tq
    lane = lax.broadcasted_iota(jnp.int32, (tq, LANES), 1)
    q = q_ref[...]
    for half in range(lhs_ref.shape[0]):
        for h in range(NSA_HEADS):
            b = bias_ref[h // NSA_GROUP_HEADS, :, half * LANES:(half + 1) * LANES]
            lhs_ref[half, h * tq:(h + 1) * tq, :] = jnp.concatenate([_nsa_head_query(q, h, lane), b], axis=1)

    def t_of_rows(r0, n):
        return t0 + ((r0 + lax.broadcasted_iota(jnp.int32, (n, 1), 0)) & (tq - 1))

    o = _flash_body(lhs_ref, ks_ref, e_ref, va_ref, vb_ref, t_of_rows, t0 // K_TILE, NSA_SEL_LEN * LANES, scratch)
    gates = gate_ref[...]
    outs = [o[h * tq:(h + 1) * tq] * gates[:, NSA_HEADS + h:NSA_HEADS + h + 1] for h in range(NSA_HEADS)]
    for j in range(4):
        cs = slice(j * LANES, (j + 1) * LANES)
        a = jnp.where(lane < HEAD_DIM, outs[j], outs[4 + j]) + ocw_ref[:, cs]
        out_ref[:, cs] = (a * sz_ref[:, cs].astype(F32)).astype(out_ref.dtype)


def _nsa_sel(nq, bias, ks, vs, e_sel, gates, ocw, sz):
    T = nq.shape[0]
    n_selp = bias.shape[2]
    rows = NSA_HEADS * Q_TILE
    row = lambda w: pl.BlockSpec((Q_TILE, w), lambda qi: (qi, 0))
    full = pl.BlockSpec((T, LANES), lambda qi: (0, 0))
    va, vb = vs if isinstance(vs, tuple) else _values_with_ones(vs)
    return pl.pallas_call(
        _nsa_sel_kernel,
        grid=(T // Q_TILE,),
        in_specs=[row(512), pl.BlockSpec((2, Q_TILE, n_selp), lambda qi: (0, qi, 0)), full, full, full, full,
                  row(LANES), row(512), row(512)],
        out_specs=row(512),
        out_shape=jax.ShapeDtypeStruct((T, 512), BF16),
        scratch_shapes=[pltpu.VMEM((n_selp // LANES, rows, 2 * LANES), BF16)] + _flash_scratch(rows),
        compiler_params=_params("arbitrary"),
        name="nsa_selected",
    )(nq, bias, ks, va, vb, e_sel, gates, ocw, sz)


MOBA_Q_TILE = 512


def _moba_kernel(q_ref, kmean_ref, k_ref, va_ref, vb_ref, e_ref, sz_ref, out_ref, lhs_ref, *scratch, top):
    tq = q_ref.shape[0]
    t0 = pl.program_id(1) * tq
    lane = lax.broadcasted_iota(jnp.int32, (tq, LANES), 1)
    colv = q_ref[...]
    zero = jnp.zeros_like(colv)
    qp = jnp.concatenate([jnp.where(lane < HEAD_DIM, colv, zero), jnp.where(lane >= HEAD_DIM, colv, zero)], axis=0)

    def t_of_rows(r0, n):
        return t0 + ((r0 + lax.broadcasted_iota(jnp.int32, (n, 1), 0)) & (tq - 1))

    n_blk_rows = kmean_ref.shape[0]
    cur = (t0 + (lax.broadcasted_iota(jnp.int32, (1, 2 * tq), 1) & (tq - 1))) >> 8
    blk = lax.broadcasted_iota(jnp.int32, (n_blk_rows, 2 * tq), 0)
    past = blk < cur
    gate = jnp.where(past, _dot_nt(kmean_ref[...], qp), NEG_INF)
    sel = _top_k_flags(gate, blk, top)
    bias_t = jnp.where((sel & past) | (blk == cur), 0.0, MASK_BIAS)
    if n_blk_rows < LANES:
        bias_t = jnp.concatenate([bias_t, jnp.full((LANES - n_blk_rows, 2 * tq), MASK_BIAS, F32)], axis=0)
    lhs_ref[0] = jnp.concatenate([qp, bias_t.T.astype(BF16)], axis=1)
    o = _flash_body(lhs_ref, k_ref, e_ref, va_ref, vb_ref, t_of_rows, t0 // K_TILE, MOBA_BLOCK * LANES, scratch)
    a = jnp.where(lane < HEAD_DIM, o[:tq], o[tq:])
    out_ref[...] = (a * sz_ref[...].astype(F32)).astype(out_ref.dtype)


def _moba(mq, kmean, mk, mv, e_blk, sz, top):
    T = mq.shape[0]
    tq = MOBA_Q_TILE
    assert MOBA_BLOCK == 256 and T * 1 <= MOBA_BLOCK * LANES
    rows = 2 * tq
    row = pl.BlockSpec((tq, LANES), lambda p, qi: (qi, p))
    col = pl.BlockSpec((T, LANES), lambda p, qi: (0, p))
    if isinstance(mv, tuple):
        va, vb = mv
    else:
        v4 = mv.reshape(T, MOBA_HEADS // 2, 2, HEAD_DIM)
        ones = jnp.ones((T, MOBA_HEADS // 2, HEAD_DIM), mv.dtype)
        va = jnp.stack([v4[:, :, 0], ones], axis=2).reshape(T, 512)
        vb = jnp.stack([ones, v4[:, :, 1]], axis=2).reshape(T, 512)
    return pl.pallas_call(
        functools.partial(_moba_kernel, top=top),
        grid=(MOBA_HEADS // 2, T // tq),
        in_specs=[row, pl.BlockSpec((kmean.shape[0], LANES), lambda p, qi: (0, p)), col, col, col,
                  pl.BlockSpec((T, LANES), lambda p, qi: (0, 0)), row],
        out_specs=row,
        out_shape=jax.ShapeDtypeStruct((T, 512), BF16),
        scratch_shapes=[pltpu.VMEM((1, rows, 2 * LANES), BF16)] + _flash_scratch(rows),
        compiler_params=_params("arbitrary", "arbitrary"),
        name="moba",
    )(mq, kmean, mk, va, vb, e_blk, sz)


def _mem_kv_kernel(mem_ref, g_ref, w0_ref, w1_ref, kv0_ref, kv1_ref):
    mem_n = _rms(mem_ref[...], g_ref[...]).astype(BF16)
    kv0_ref[...] = _dot(mem_n, w0_ref[...]).astype(kv0_ref.dtype)
    kv1_ref[...] = _dot(mem_n, w1_ref[...]).astype(kv1_ref.dtype)


def _mem_kv(mem, g, w0, w1):
    n = mem.shape[0]
    kv_w = 2 * MEM_HEADS * HEAD_DIM
    return pl.pallas_call(
        _mem_kv_kernel,
        grid=(1,),
        in_specs=[
            pl.BlockSpec((n, D_MODEL), lambda i: (0, 0)),
            pl.BlockSpec((1, D_MODEL), lambda i: (0, 0)),
            pl.BlockSpec((D_MODEL, kv_w), lambda i: (0, 0)),
            pl.BlockSpec((D_MODEL, kv_w), lambda i: (0, 0)),
        ],
        out_specs=[pl.BlockSpec((n, kv_w), lambda i: (0, 0)), pl.BlockSpec((n, kv_w), lambda i: (0, 0))],
        out_shape=[jax.ShapeDtypeStruct((n, kv_w), BF16), jax.ShapeDtypeStruct((n, kv_w), BF16)],
        compiler_params=_params("arbitrary"),
        name="mem_kv",
    )(mem, g, w0, w1)


def _mem_attention(eq, kv, sez, lane):
    rows = eq.shape[0]
    mem_w = MEM_HEADS * HEAD_DIM
    cols = []
    for p in range(MEM_HEADS // 2):
        cs = slice(p * LANES, (p + 1) * LANES)
        colv = eq[:, cs]
        zero = jnp.zeros_like(colv)
        qp = jnp.concatenate([jnp.where(lane < HEAD_DIM, colv, zero), jnp.where(lane >= HEAD_DIM, colv, zero)], axis=0)
        s = _dot_nt(qp, kv[:, cs])
        pe = jnp.exp(s - jnp.max(s, axis=-1, keepdims=True))
        o = _dot(pe.astype(BF16), kv[:, mem_w + p * LANES:mem_w + (p + 1) * LANES]) / jnp.sum(pe, axis=-1, keepdims=True)
        cols.append(jnp.where(lane < HEAD_DIM, o[:rows], o[rows:]) * sez[:, cs].astype(F32))
    return jnp.concatenate(cols, axis=1).astype(BF16)


def _out0_kernel(ag_ref, bg_ref, eq_ref, sez_ref, kv_ref, wa_ref, wb_ref, wm_ref, x_ref, out_ref):
    rows = x_ref.shape[0]
    lane = lax.broadcasted_iota(jnp.int32, (rows, LANES), 1)
    mg = _mem_attention(eq_ref[...], kv_ref[...], sez_ref[...], lane)
    y = _dot(ag_ref[...], wa_ref[...]) + _dot(bg_ref[...], wb_ref[...]) + _dot(mg, wm_ref[...])
    out_ref[...] = x_ref[...] + y


def _out0(ag, bg, eq, sez, kv, wa, wb, wm, x):
    T = x.shape[0]
    rt = PROJ_ROWS
    row = lambda w: pl.BlockSpec((rt, w), lambda i: (i, 0))
    full = lambda a: pl.BlockSpec(a.shape, lambda i: (0, 0))
    return pl.pallas_call(
        _out0_kernel,
        grid=(T // rt,),
        in_specs=[row(512), row(512), row(256), row(256), full(kv), full(wa), full(wb), full(wm), row(D_MODEL)],
        out_specs=row(D_MODEL),
        out_shape=jax.ShapeDtypeStruct((T, D_MODEL), F32),
        compiler_params=_params("arbitrary"),
        name="out_proj0",
    )(ag, bg, eq, sez, kv, wa, wb, wm, x)


def _out1_kernel(rg_ref, eq_ref, sez_ref, kv_ref, wr_ref, wm_ref, x_ref, g_ref, out_ref):
    rows = x_ref.shape[0]
    lane = lax.broadcasted_iota(jnp.int32, (rows, LANES), 1)
    mg = _mem_attention(eq_ref[...], kv_ref[...], sez_ref[...], lane)
    y = _dot(rg_ref[...], wr_ref[...]) + _dot(mg, wm_ref[...])
    out_ref[...] = _rms(x_ref[...] + y, g_ref[...])


def _out1(rg, eq, sez, kv, wr, wm, x, g):
    T = x.shape[0]
    rt = PROJ_ROWS
    row = lambda w: pl.BlockSpec((rt, w), lambda i: (i, 0))
    full = lambda a: pl.BlockSpec(a.shape, lambda i: (0, 0))
    return pl.pallas_call(
        _out1_kernel,
        grid=(T // rt,),
        in_specs=[row(RET_HEADS * RET_V_DIM), row(256), row(256), full(kv), full(wr), full(wm), row(D_MODEL), full(g)],
        out_specs=row(D_MODEL),
        out_shape=jax.ShapeDtypeStruct((T, D_MODEL), F32),
        compiler_params=_params("arbitrary"),
        name="out_proj1_norm",
    )(rg, eq, sez, kv, wr, wm, x, g)


def _retention_kernel(q_ref, k_ref, v_ref, sz_ref, decay_ref, qd_ref, kd_ref, cd_ref, out_ref, state_ref):
    @pl.when(pl.program_id(0) == 0)
    def _():
        state_ref[...] = jnp.zeros(state_ref.shape, F32)

    for h in range(RET_HEADS):
        qs = slice(h * RET_QK_DIM, (h + 1) * RET_QK_DIM)
        vsl = slice(h * RET_V_DIM, (h + 1) * RET_V_DIM)
        qh = q_ref[:, qs]
        kh = k_ref[:, qs]
        vh = v_ref[:, vsl]
        st = state_ref[h]
        inner = _dot_nt(qh, kh.astype(BF16)) * decay_ref[h]
        o = _dot(inner.astype(BF16), vh) + _dot(qh, st.astype(BF16)) * qd_ref[h]
        state_ref[h] = st * cd_ref[h] + _dot_tn((kh * kd_ref[h]).astype(BF16), vh)
        mu = jnp.mean(o, axis=-1, keepdims=True)
        d = o - mu
        on = d * lax.rsqrt(jnp.mean(d * d, axis=-1, keepdims=True) + NORM_EPS)
        out_ref[:, vsl] = (on * sz_ref[:, vsl].astype(F32)).astype(out_ref.dtype)


def _retention(rq, rk, rv, sz):
    T = rq.shape[0]
    C = RET_CHUNK
    log_g = jnp.log(1.0 - 2.0 ** (-5.0 - jnp.arange(RET_HEADS, dtype=F32)))
    i = jnp.arange(C, dtype=F32)
    diff = i[:, None] - i[None, :]
    decay = jnp.where(diff >= 0, jnp.exp(jnp.maximum(diff, 0.0)[None] * log_g[:, None, None]), 0.0)
    q_decay = jnp.exp((i + 1.0)[None, :] * log_g[:, None])[:, :, None]
    k_decay = jnp.exp((C - 1.0 - i)[None, :] * log_g[:, None])[:, :, None]
    chunk_decay = jnp.exp(C * log_g)[:, None, None]
    qk_w = RET_HEADS * RET_QK_DIM
    v_w = RET_HEADS * RET_V_DIM
    return pl.pallas_call(
        _retention_kernel,
        grid=(T // C,),
        in_specs=[
            pl.BlockSpec((C, qk_w), lambda c: (c, 0)),
            pl.BlockSpec((C, qk_w), lambda c: (c, 0)),
            pl.BlockSpec((C, v_w), lambda c: (c, 0)),
            pl.BlockSpec((C, v_w), lambda c: (c, 0)),
            pl.BlockSpec((RET_HEADS, C, C), lambda c: (0, 0, 0)),
            pl.BlockSpec((RET_HEADS, C, 1), lambda c: (0, 0, 0)),
            pl.BlockSpec((RET_HEADS, C, 1), lambda c: (0, 0, 0)),
            pl.BlockSpec((RET_HEADS, 1, 1), lambda c: (0, 0, 0)),
        ],
        out_specs=pl.BlockSpec((C, v_w), lambda c: (c, 0)),
        out_shape=jax.ShapeDtypeStruct((T, v_w), BF16),
        scratch_shapes=[pltpu.VMEM((RET_HEADS, RET_QK_DIM, RET_V_DIM), F32)],
        compiler_params=_params("arbitrary"),
        name="retention",
    )(rq, rk, rv, sz, decay, q_decay, k_decay, chunk_decay)


def _head_cols(order):
    return np.concatenate([np.arange(h * HEAD_DIM, (h + 1) * HEAD_DIM) for h in order])


def _one_hot_blocks(T, block):
    b = (np.arange(T) // block) % LANES
    return jnp.asarray(np.eye(LANES, dtype=np.float32)[b], dtype=BF16)


def kernel(x, mem, positions, l0_norm_g, l0_w_in, l0_nsa_gate_b, l0_cmp_pe_k, l0_cmp_w1_k, l0_cmp_w2_k,
           l0_cmp_pe_v, l0_cmp_w1_v, l0_cmp_w2_v, l0_w_mem_kv, l0_w_out,
           l1_norm_g, l1_w_in, l1_w_mem_kv, l1_w_out, mem_norm_g, final_norm_g):
    B, T, _ = x.shape
    assert B == 1 and T % K_TILE == 0 and T >= NSA_WINDOW + Q_TILE
    n_sel = T // NSA_SEL_LEN
    n_selp = -(-n_sel // LANES) * LANES
    k_sel = min(NSA_SEL_TOPK, n_sel)
    assert k_sel >= 3
    n_blk = T // MOBA_BLOCK
    assert n_blk <= LANES
    moba_top = max(1, min(MOBA_TOPK, n_blk - 1))
    x2 = x[0]
    pos = positions[0].astype(F32)

    attn_inv = 1.0 / (ROPE_THETA ** (jnp.arange(0, HEAD_DIM, 2, dtype=F32) / HEAD_DIM))
    ret_inv = 1.0 / (ROPE_THETA ** jnp.linspace(0.0, 1.0, RET_QK_DIM // 2, dtype=F32))
    ang = pos[:, None] * attn_inv
    cos64 = jnp.tile(jnp.cos(ang), (1, 4))
    sin64 = jnp.tile(jnp.concatenate([-jnp.sin(ang), jnp.sin(ang)], axis=1), (1, 2))
    rang = pos[:, None] * ret_inv
    cos256, sin256 = jnp.cos(rang), jnp.sin(rang)

    offs = np.cumsum([0, 512, 128, 128, 128, 128, 128, 128, 24, 512, 512, 512, 512, 512, 256, 256])
    nsa_cols = _head_cols(NSA_HEAD_ORDER)
    col_idx = np.concatenate([
        offs[0] + nsa_cols,
        np.arange(offs[1], offs[7]),
    ])
    w0 = jnp.concatenate([
        l0_w_in[:, col_idx],
        l0_w_in[:, offs[7]:offs[8]], jnp.zeros((D_MODEL, LANES - 24), F32),
        l0_w_in[:, offs[8] + nsa_cols],
        l0_w_in[:, offs[9]:],
    ], axis=1).astype(BF16)
    gate_b = jnp.concatenate([l0_nsa_gate_b, jnp.zeros((LANES - 24,), F32)])[None, :]
    groups0 = [
        (512, "rot64", LOG2_SCALE), (128, "rot64", 1.0), (128, "plain", 1.0), (128, "rot64", 1.0),
        (128, "ones_pair", 1.0), (128, "rot64", 1.0), (128, "ones_pair", 1.0), (128, "gate", 1.0), (512, "silu", 1.0),
        (512, "rot64", LOG2_SCALE), (512, "rot64_mean", 1.0), (512, "ones_pair", 1.0), (512, "silu", 1.0),
        (256, "plain", 0.125), (256, "silu", 1.0),
    ]
    dts0 = [BF16] * 7 + [F32] + [BF16] * 7
    (nq, nkc, nvc, nks, nvs_a, nvs_b, nkw, nvw_a, nvw_b, gates, snz, mq, mk, kmean, mv_a, mv_b, smz, eq0, sez0) = _proj(
        x2, l0_norm_g[None, :], w0, cos64, sin64, gate_b, groups0, dts0, "proj0")

    kv0, kv1 = _mem_kv(mem[0], mem_norm_g[None, :], l0_w_mem_kv.astype(BF16), l1_w_mem_kv.astype(BF16))

    kc = _compress(nkc, l0_cmp_pe_k, l0_cmp_w1_k, l0_cmp_w2_k, "compress_k")
    vc = _compress(nvc, l0_cmp_pe_v, l0_cmp_w1_v, l0_cmp_w2_v, "compress_v")
    n_cmp_rows = T // NSA_CMP_STRIDE
    cmp_start = np.arange(n_cmp_rows)[:, None] * NSA_CMP_STRIDE
    sel_start = np.arange(n_selp)[None, :] * NSA_SEL_LEN
    overlap = ((cmp_start < sel_start + NSA_SEL_LEN) & (cmp_start + NSA_CMP_LEN > sel_start)
               & (np.arange(n_selp)[None, :] < n_sel))
    ov = jnp.asarray(overlap.T.astype(np.float32), dtype=BF16)
    ocw, sel_bias = _nsa_cmp(nq, kc, vc, ov, nkw, (nvw_a, nvw_b), gates, k_sel)
    ag = _nsa_sel(nq, sel_bias, nks, (nvs_a, nvs_b), _one_hot_blocks(T, NSA_SEL_LEN), gates, ocw, snz)

    n_blk_rows = -(-n_blk // 16) * 16
    kmean_p = jnp.concatenate([kmean[:, 0, :], jnp.zeros((n_blk_rows - n_blk, 512), F32)], axis=0).astype(BF16)
    bg = _moba(mq, kmean_p, mk, (mv_a, mv_b), _one_hot_blocks(T, MOBA_BLOCK), smz, moba_top)

    wa = l0_w_out[:512][nsa_cols].astype(BF16)
    wb = l0_w_out[512:1024].astype(BF16)
    wm0 = l0_w_out[1024:].astype(BF16)
    x1 = _out0(ag, bg, eq0, sez0, kv0, wa, wb, wm0, x2)

    groups1 = [
        (1024, "rot256", 1.0), (1024, "rot256", RET_QK_DIM ** -0.5), (2048, "plain", 1.0), (2048, "silu", 1.0),
        (256, "plain", 0.125), (256, "silu", 1.0),
    ]
    dts1 = [BF16, F32, BF16, BF16, BF16, BF16]
    rq, rk, rv, srz, eq1, sez1 = _proj(x1, l1_norm_g[None, :], l1_w_in.astype(BF16), cos256, sin256,
                                       jnp.zeros((1, LANES), F32), groups1, dts1, "proj1")
    rg = _retention(rq, rk, rv, srz)
    v_w = RET_HEADS * RET_V_DIM
    out = _out1(rg, eq1, sez1, kv1, l1_w_out[:v_w].astype(BF16), l1_w_out[v_w:].astype(BF16), x1,
                final_norm_g[None, :])
    return out[None]
```

```python
import functools

import numpy as np
import jax
import jax.numpy as jnp
from jax import lax
from jax.experimental import pallas as pl
from jax.experimental.pallas import tpu as pltpu

F32 = jnp.float32
BF16 = jnp.bfloat16

D_MODEL = 1024
HEAD_DIM = 64
ROPE_THETA = 10000.0
NORM_EPS = 1e-6
NEG_INF = -1e30
MASK_BIAS = NEG_INF
LOG2_SCALE = HEAD_DIM ** -0.5 * float(np.log2(np.e))

NSA_HEADS = 8
NSA_GROUP_HEADS = 4
NSA_CMP_LEN = 32
NSA_CMP_STRIDE = 16
NSA_SEL_LEN = 64
NSA_SEL_TOPK = 16
NSA_WINDOW = 512
MOBA_HEADS = 8
MOBA_BLOCK = 256
MOBA_TOPK = 3
MEM_LEN = 256
MEM_HEADS = 4
RET_HEADS = 4
RET_QK_DIM = 256
RET_V_DIM = 512
RET_CHUNK = 256

LANES = 128
PROJ_ROWS = 256
PROJ_COLS = 512
Q_TILE = 128
K_TILE = 512
VMEM_LIMIT = 56 * 1024 * 1024

NSA_HEAD_ORDER = (0, 4, 1, 5, 2, 6, 3, 7)


def _dot(a, b):
    return jnp.dot(a, b, preferred_element_type=F32)


def _dot_nt(a, b):
    return lax.dot_general(a, b, (((1,), (1,)), ((), ())), preferred_element_type=F32)


def _dot_tn(a, b):
    return lax.dot_general(a, b, (((0,), (0,)), ((), ())), preferred_element_type=F32)


def _silu(y):
    return y * (1.0 / (1.0 + jnp.exp(-y)))


def _rms(x, g):
    return x * lax.rsqrt(jnp.mean(x * x, axis=-1, keepdims=True) + NORM_EPS) * g


def _params(*sem):
    return pltpu.CompilerParams(dimension_semantics=sem, vmem_limit_bytes=VMEM_LIMIT)


def _proj_kernel(x_ref, g_ref, w_ref, c_ref, s_ref, b_ref, *out_refs, groups):
    h = _rms(x_ref[...], g_ref[...]).astype(BF16)
    rows = x_ref.shape[0]
    lane = lax.broadcasted_iota(jnp.int32, (rows, LANES), 1)
    low_half = (lane & 32) == 0
    pieces = []
    col = 0
    oi = 0
    for width, kind, scale in groups:
        out_ref = out_refs[oi]
        oi += 1
        mean_ref = None
        if kind in ("rot64_mean", "ones_pair"):
            mean_ref = out_refs[oi]
            oi += 1
        pw = RET_QK_DIM if kind == "rot256" else LANES
        for c0 in range(0, width, pw):
            pieces.append((col + c0, pw, kind, scale, out_ref, c0, mean_ref))
        col += width

    def epilogue(z, kind, scale):
        if kind in ("rot64", "rot64_mean"):
            partner = jnp.where(low_half, pltpu.roll(z, LANES - 32, 1), pltpu.roll(z, 32, 1))
            z = z * c_ref[...] + partner * s_ref[...]
        elif kind == "rot256":
            x1, x2 = z[:, :LANES], z[:, LANES:]
            z = jnp.concatenate([x1 * c_ref[...] - x2 * s_ref[...], x1 * s_ref[...] + x2 * c_ref[...]], axis=1)
        elif kind == "silu":
            z = _silu(z)
        elif kind == "gate":
            z = 1.0 / (1.0 + jnp.exp(-(z + b_ref[...])))
        return z if scale == 1.0 else z * scale

    i = 0
    while i < len(pieces):
        j, run_w = i, 0
        while j < len(pieces) and run_w + pieces[j][1] <= PROJ_COLS:
            run_w += pieces[j][1]
            j += 1
        c_start = pieces[i][0]
        y = _dot(h, w_ref[:, c_start:c_start + run_w])
        for pc, pw, kind, scale, out_ref, oc, mean_ref in pieces[i:j]:
            z = epilogue(y[:, pc - c_start:pc - c_start + pw], kind, scale)
            if kind == "ones_pair":
                out_ref[:, oc:oc + pw] = jnp.where(lane < HEAD_DIM, z, 1.0).astype(out_ref.dtype)
                mean_ref[:, oc:oc + pw] = jnp.where(lane >= HEAD_DIM, z, 1.0).astype(mean_ref.dtype)
                continue
            if mean_ref is not None:
                mean_ref[0, :, oc:oc + pw] = jnp.mean(z, axis=0, keepdims=True)
            out_ref[:, oc:oc + pw] = z.astype(out_ref.dtype)
        i = j


def _proj(x, g, w, cos_t, sin_t, bias, groups, out_dtypes, name):
    T = x.shape[0]
    n_tiles = T // PROJ_ROWS
    out_shapes = []
    out_specs = []
    for (width, kind, _), dt in zip(groups, out_dtypes):
        out_shapes.append(jax.ShapeDtypeStruct((T, width), dt))
        out_specs.append(pl.BlockSpec((PROJ_ROWS, width), lambda i: (i, 0)))
        if kind == "rot64_mean":
            out_shapes.append(jax.ShapeDtypeStruct((n_tiles, 1, width), F32))
            out_specs.append(pl.BlockSpec((1, 1, width), lambda i: (i, 0, 0)))
        if kind == "ones_pair":
            out_shapes.append(jax.ShapeDtypeStruct((T, width), dt))
            out_specs.append(pl.BlockSpec((PROJ_ROWS, width), lambda i: (i, 0)))
    return pl.pallas_call(
        functools.partial(_proj_kernel, groups=tuple(groups)),
        grid=(n_tiles,),
        in_specs=[
            pl.BlockSpec((PROJ_ROWS, D_MODEL), lambda i: (i, 0)),
            pl.BlockSpec((1, D_MODEL), lambda i: (0, 0)),
            pl.BlockSpec(w.shape, lambda i: (0, 0)),
            pl.BlockSpec((PROJ_ROWS, LANES), lambda i: (i, 0)),
            pl.BlockSpec((PROJ_ROWS, LANES), lambda i: (i, 0)),
            pl.BlockSpec((1, LANES), lambda i: (0, 0)),
        ],
        out_specs=out_specs,
        out_shape=out_shapes,
        compiler_params=_params("arbitrary"),
        name=name,
    )(x, g, w, cos_t, sin_t, bias)


def _compress_kernel(a_ref, an_ref, pea_ref, peb_ref, w1a_ref, w1b_ref, w2_ref, out_ref):
    xa = (a_ref[...].astype(F32) + pea_ref[...]).astype(BF16)
    xb = (an_ref[...].astype(F32) + peb_ref[...]).astype(BF16)
    hid = _silu(_dot(xa, w1a_ref[...]) + _dot(xb, w1b_ref[...]))
    out_ref[...] = _dot(hid.astype(BF16), w2_ref[...]).astype(out_ref.dtype)


def _compress(a, pe, w1, w2, name):
    T = a.shape[0]
    n_rows = T // NSA_CMP_STRIDE
    row_w = NSA_CMP_STRIDE * LANES
    a2 = a.reshape(n_rows, row_w)
    a2n = jnp.concatenate([a2[1:], jnp.zeros((1, row_w), a2.dtype)], axis=0)
    eye2 = jnp.eye(2, dtype=F32)

    def expand(w1_half):
        return jnp.einsum("lde,gh->lgdhe", w1_half, eye2).reshape(row_w, LANES).astype(BF16)

    def pe_row(pe_half):
        return jnp.broadcast_to(pe_half[:, None, :], (NSA_CMP_STRIDE, 2, HEAD_DIM)).reshape(1, row_w)

    w2bd = jnp.einsum("ef,gh->gehf", w2, eye2).reshape(LANES, LANES).astype(BF16)
    tile = min(256, n_rows)
    return pl.pallas_call(
        _compress_kernel,
        grid=(n_rows // tile,),
        in_specs=[
            pl.BlockSpec((tile, row_w), lambda i: (i, 0)),
            pl.BlockSpec((tile, row_w), lambda i: (i, 0)),
            pl.BlockSpec((1, row_w), lambda i: (0, 0)),
            pl.BlockSpec((1, row_w), lambda i: (0, 0)),
            pl.BlockSpec((row_w, LANES), lambda i: (0, 0)),
            pl.BlockSpec((row_w, LANES), lambda i: (0, 0)),
            pl.BlockSpec((LANES, LANES), lambda i: (0, 0)),
        ],
        out_specs=pl.BlockSpec((tile, LANES), lambda i: (i, 0)),
        out_shape=jax.ShapeDtypeStruct((n_rows, LANES), BF16),
        compiler_params=_params("arbitrary"),
        name=name,
    )(a2, a2n, pe_row(pe[:NSA_CMP_STRIDE]), pe_row(pe[NSA_CMP_STRIDE:]),
      expand(w1[:NSA_CMP_STRIDE]), expand(w1[NSA_CMP_STRIDE:]), w2bd)


def _nsa_head_query(q, h, lane):
    colv = q[:, (h % 4) * LANES:(h % 4 + 1) * LANES]
    keep = (lane < HEAD_DIM) if h < NSA_GROUP_HEADS else (lane >= HEAD_DIM)
    return jnp.where(keep, colv, jnp.zeros_like(colv))


def _nsa_stacked_queries(q, lane):
    return jnp.concatenate([_nsa_head_query(q, h, lane) for h in range(NSA_HEADS)], axis=0)


def _nsa_interleave_heads(o, gates, gate0, lane, out_ref, accumulate=False):
    tq = out_ref.shape[0]
    outs = [o[h * tq:(h + 1) * tq] * gates[:, gate0 + h:gate0 + h + 1] for h in range(NSA_HEADS)]
    for j in range(4):
        cs = slice(j * LANES, (j + 1) * LANES)
        a = jnp.where(lane < HEAD_DIM, outs[j], outs[4 + j])
        out_ref[:, cs] = out_ref[:, cs] + a if accumulate else a


def _values_with_ones(v):
    ones = jnp.ones((v.shape[0], HEAD_DIM), v.dtype)
    return jnp.concatenate([v[:, :HEAD_DIM], ones], axis=1), jnp.concatenate([ones, v[:, HEAD_DIM:]], axis=1)


def _top_k_flags(x, idx, k):
    sel = jnp.zeros(x.shape, jnp.bool_)
    for _ in range(k):
        mx = jnp.max(x, axis=0, keepdims=True)
        first = jnp.min(jnp.where(x == mx, idx, x.shape[0]), axis=0, keepdims=True)
        hit = idx == first
        sel = sel | hit
        x = jnp.where(hit, -jnp.inf, x)
    return sel


CMP_KEY_BUCKET = 256


def _nsa_cmp_kernel(q_ref, kc_ref, vca_ref, vcb_ref, ovt_ref, kw_ref, wa_ref, wb_ref, gate_ref, oc_ref, bias_ref,
                    *, k_sel):
    tq = q_ref.shape[0]
    n_cmp = kc_ref.shape[0]
    n_selp = ovt_ref.shape[0]
    rows = NSA_HEADS * tq
    half = rows // 2
    t0 = pl.program_id(0) * tq
    lane = lax.broadcasted_iota(jnp.int32, (tq, LANES), 1)
    t_rows = t0 + (lax.broadcasted_iota(jnp.int32, (rows, 1), 0) & (tq - 1))

    span = NSA_WINDOW + tq
    start = pl.multiple_of(jnp.maximum(t0 - NSA_WINDOW, 0), tq)

    def window(mask_scores):
        s = mask_scores(_dot_nt(_nsa_stacked_queries(q_ref[...], lane), kw_ref[pl.ds(start, span), :]))
        pb = jnp.exp2(s - jnp.max(s, axis=-1, keepdims=True)).astype(BF16)
        acc = jnp.concatenate([_dot(pb[:half], wa_ref[pl.ds(start, span), :]),
                               _dot(pb[half:], wb_ref[pl.ds(start, span), :])], axis=0)
        _nsa_interleave_heads(acc / pltpu.roll(acc, HEAD_DIM, 1), gate_ref[...], 2 * NSA_HEADS, lane, oc_ref)

    @pl.when(t0 >= NSA_WINDOW)
    def _():
        q_local = lax.broadcasted_iota(jnp.int32, (rows, 1), 0) & (tq - 1)
        k_local = lax.broadcasted_iota(jnp.int32, (1, tq), 1)
        window(lambda s: jnp.concatenate([jnp.where(k_local > q_local, s[:, :tq], NEG_INF), s[:, tq:NSA_WINDOW],
                                          jnp.where(k_local <= q_local, s[:, NSA_WINDOW:], NEG_INF)], axis=1))

    @pl.when(t0 < NSA_WINDOW)
    def _():
        wpos = lax.broadcasted_iota(jnp.int32, (1, span), 1)
        window(lambda s: jnp.where(wpos <= t_rows, s, NEG_INF))

    def attend(nk):
        cmp_end = lax.broadcasted_iota(jnp.int32, (1, nk), 1) * NSA_CMP_STRIDE + (NSA_CMP_LEN - 1)
        s = _dot_nt(_nsa_stacked_queries(q_ref[...], lane), kc_ref[0:nk, :])
        s = jnp.where(cmp_end <= t_rows, s, NEG_INF)
        m = jnp.max(s, axis=-1, keepdims=True)
        p = jnp.exp2(s - m)
        pb = p.astype(BF16)
        acc_a = _dot(pb[:half], vca_ref[0:nk, :])
        acc_b = _dot(pb[half:], vcb_ref[0:nk, :])
        l = jnp.concatenate([acc_a[:, HEAD_DIM:HEAD_DIM + 1], acc_b[:, 0:1]], axis=0)
        rl = jnp.where(m > 0.5 * NEG_INF, 1.0 / jnp.maximum(l, 1e-30), 0.0)
        _nsa_interleave_heads(jnp.concatenate([acc_a, acc_b], axis=0) * rl, gate_ref[...], 0, lane, oc_ref,
                              accumulate=True)
        pr = p * rl
        nb = min(n_selp, nk * NSA_CMP_STRIDE // NSA_SEL_LEN)
        blk = lax.broadcasted_iota(jnp.int32, (nb, tq), 0)
        cur = (t0 + lax.broadcasted_iota(jnp.int32, (1, tq), 1)) >> 6
        valid = blk <= cur
        forced = (blk == 0) | (blk == cur) | (blk == cur - 1)
        for g in range(2):
            r0 = g * half
            pg = pr[r0:r0 + tq] + pr[r0 + tq:r0 + 2 * tq] + pr[r0 + 2 * tq:r0 + 3 * tq] + pr[r0 + 3 * tq:r0 + 4 * tq]
            imp_t = _dot_nt(ovt_ref[0:nb, 0:nk], pg.astype(BF16))
            x = jnp.where(valid & jnp.logical_not(forced), imp_t, NEG_INF)
            sel = _top_k_flags(x, blk, k_sel - 3) | forced
            bias_t = jnp.where(sel & valid, 0.0, MASK_BIAS)
            if nb < n_selp:
                bias_t = jnp.concatenate([bias_t, jnp.full((n_selp - nb, tq), MASK_BIAS, F32)], axis=0)
            bias_ref[g] = bias_t.T.astype(BF16)

    needed = (t0 + tq - NSA_CMP_LEN) // NSA_CMP_STRIDE + 1
    buckets = sorted({min(b, n_cmp) for b in range(CMP_KEY_BUCKET, n_cmp + CMP_KEY_BUCKET, CMP_KEY_BUCKET)})
    for i, nk in enumerate(buckets):
        lo = buckets[i - 1] if i else -(1 << 30)
        cond = (needed > lo) if i == len(buckets) - 1 else ((needed > lo) & (needed <= nk))
        pl.when(cond)(functools.partial(attend, nk))


def _nsa_cmp(nq, kc, vc, ov, kw, vw, gates, k_sel):
    T = nq.shape[0]
    n_cmp = kc.shape[0]
    n_selp = ov.shape[0]
    vca, vcb = _values_with_ones(vc)
    wa, wb = vw if isinstance(vw, tuple) else _values_with_ones(vw)
    cmp_full = pl.BlockSpec((n_cmp, LANES), lambda i: (0, 0))
    full = pl.BlockSpec((T, LANES), lambda i: (0, 0))
    return pl.pallas_call(
        functools.partial(_nsa_cmp_kernel, k_sel=k_sel),
        grid=(T // Q_TILE,),
        in_specs=[
            pl.BlockSpec((Q_TILE, 512), lambda i: (i, 0)),
            cmp_full, cmp_full, cmp_full,
            pl.BlockSpec((n_selp, n_cmp), lambda i: (0, 0)),
            full, full, full,
            pl.BlockSpec((Q_TILE, LANES), lambda i: (i, 0)),
        ],
        out_specs=[
            pl.BlockSpec((Q_TILE, 512), lambda i: (i, 0)),
            pl.BlockSpec((2, Q_TILE, n_selp), lambda i: (0, i, 0)),
        ],
        out_shape=[
            jax.ShapeDtypeStruct((T, 512), F32),
            jax.ShapeDtypeStruct((2, T, n_selp), BF16),
        ],
        compiler_params=_params("arbitrary"),
        name="nsa_cmp_win_select",
    )(nq, kc, vca, vcb, ov, kw, wa, wb, gates)


def _flash_scratch(rows):
    return [
        pltpu.VMEM((rows, K_TILE), F32), pltpu.VMEM((rows, K_TILE), F32),
        pltpu.VMEM((rows, K_TILE), BF16), pltpu.VMEM((rows, K_TILE), BF16),
        pltpu.VMEM((rows, 1), F32), pltpu.VMEM((rows, 1), F32),
        pltpu.VMEM((rows, 1), F32),
        pltpu.VMEM((rows, LANES), F32),
    ]


def _flash_body(lhs_ref, k_ref, e_ref, va_ref, vb_ref, t_of_rows, n_full, tokens_per_half, scratch):
    s_a, s_b, p_a, p_b, al_a, al_b, m_ref, acc_ref = scratch
    tk = K_TILE
    rows = acc_ref.shape[0]
    half_rows = rows // 2
    last_mem_tile = k_ref.shape[0] // tk - 1

    def tile_start(j):
        return pl.multiple_of(jnp.clip(j, 0, last_mem_tile) * tk, tk)

    def scores(j, s_out):
        k0 = tile_start(j)
        rhs = jnp.concatenate([k_ref[pl.ds(k0, tk), :], e_ref[pl.ds(k0, tk), :]], axis=1)
        s_out[...] = _dot_nt(lhs_ref[k0 // tokens_per_half], rhs)

    def softmax(j, s_in, p_out, al_out, causal):
        s = s_in[...]
        if causal:
            kpos = j * tk + lax.broadcasted_iota(jnp.int32, (1, tk), 1)
            s = jnp.where(kpos <= t_of_rows(0, rows), s, NEG_INF)
        m_old = m_ref[...]
        m_new = jnp.maximum(m_old, jnp.max(s, axis=-1, keepdims=True))
        m_ref[...] = m_new
        al_out[...] = jnp.exp2(m_old - m_new)
        p_out[...] = jnp.exp2(s - m_new).astype(BF16)

    def values(j, p_in, al_in):
        k0 = tile_start(j)
        for rs, v_ref in ((slice(0, half_rows), va_ref), (slice(half_rows, rows), vb_ref)):
            acc_ref[rs, :] = al_in[rs, :] * acc_ref[rs, :] + _dot(p_in[rs, :], v_ref[pl.ds(k0, tk), :])

    m_ref[...] = jnp.full(m_ref.shape, NEG_INF, F32)
    acc_ref[...] = jnp.zeros(acc_ref.shape, F32)
    p_b[...] = jnp.zeros(p_b.shape, BF16)
    al_b[...] = jnp.ones(al_b.shape, F32)
    scores(0, s_a)

    def two_full_tiles(i, carry):
        j = 2 * i
        values(j - 1, p_b, al_b)
        scores(j + 1, s_b)
        softmax(j, s_a, p_a, al_a, False)
        values(j, p_a, al_a)
        scores(j + 2, s_a)
        softmax(j + 1, s_b, p_b, al_b, False)
        return carry

    n_pairs = n_full // 2
    lax.fori_loop(0, n_pairs, two_full_tiles, 0)
    j = 2 * n_pairs
    odd = (n_full & 1) == 1

    @pl.when(odd)
    def _():
        values(j - 1, p_b, al_b)
        scores(j + 1, s_b)
        softmax(j, s_a, p_a, al_a, True)
        values(j, p_a, al_a)
        softmax(j + 1, s_b, p_b, al_b, True)
        values(j + 1, p_b, al_b)

    @pl.when(jnp.logical_not(odd))
    def _():
        values(j - 1, p_b, al_b)
        softmax(j, s_a, p_a, al_a, True)
        values(j, p_a, al_a)

    acc = acc_ref[...]
    return acc / pltpu.roll(acc, HEAD_DIM, 1)


def _nsa_sel_kernel(q_ref, bias_ref, ks_ref, va_ref, vb_ref, e_ref, gate_ref, ocw_ref, sz_ref, out_ref,
                    lhs_ref, *scratch):
    tq = q_ref.shape[0]
    t0 = pl.program_id(0) * tq
    lane = lax.broadcasted_iota(jnp.int32, (tq, LANES), 1)
    q = q_ref[...]
    for half in range(lhs_ref.shape[0]):
        for h in range(NSA_HEADS):
            b = bias_ref[h // NSA_GROUP_HEADS, :, half * LANES:(half + 1) * LANES]
            lhs_ref[half, h * tq:(h + 1) * tq, :] = jnp.concatenate([_nsa_head_query(q, h, lane), b], axis=1)

    def t_of_rows(r0, n):
        return t0 + ((r0 + lax.broadcasted_iota(jnp.int32, (n, 1), 0)) & (tq - 1))

    o = _flash_body(lhs_ref, ks_ref, e_ref, va_ref, vb_ref, t_of_rows, t0 // K_TILE, NSA_SEL_LEN * LANES, scratch)
    gates = gate_ref[...]
    outs = [o[h * tq:(h + 1) * tq] * gates[:, NSA_HEADS + h:NSA_HEADS + h + 1] for h in range(NSA_HEADS)]
    for j in range(4):
        cs = slice(j * LANES, (j + 1) * LANES)
        a = jnp.where(lane < HEAD_DIM, outs[j], outs[4 + j]) + ocw_ref[:, cs]
        out_ref[:, cs] = (a * sz_ref[:, cs].astype(F32)).astype(out_ref.dtype)


def _nsa_sel(nq, bias, ks, vs, e_sel, gates, ocw, sz):
    T = nq.shape[0]
    n_selp = bias.shape[2]
    rows = NSA_HEADS * Q_TILE
    row = lambda w: pl.BlockSpec((Q_TILE, w), lambda qi: (qi, 0))
    full = pl.BlockSpec((T, LANES), lambda qi: (0, 0))
    va, vb = vs if isinstance(vs, tuple) else _values_with_ones(vs)
    return pl.pallas_call(
        _nsa_sel_kernel,
        grid=(T // Q_TILE,),
        in_specs=[row(512), pl.BlockSpec((2, Q_TILE, n_selp), lambda qi: (0, qi, 0)), full, full, full, full,
                  row(LANES), row(512), row(512)],
        out_specs=row(512),
        out_shape=jax.ShapeDtypeStruct((T, 512), BF16),
        scratch_shapes=[pltpu.VMEM((n_selp // LANES, rows, 2 * LANES), BF16)] + _flash_scratch(rows),
        compiler_params=_params("arbitrary"),
        name="nsa_selected",
    )(nq, bias, ks, va, vb, e_sel, gates, ocw, sz)


MOBA_Q_TILE = 512


def _moba_kernel(q_ref, kmean_ref, k_ref, va_ref, vb_ref, e_ref, sz_ref, out_ref, lhs_ref, *scratch, top):
    tq = q_ref.shape[0]
    t0 = pl.program_id(1) * tq
    lane = lax.broadcasted_iota(jnp.int32, (tq, LANES), 1)
    colv = q_ref[...]
    zero = jnp.zeros_like(colv)
    qp = jnp.concatenate([jnp.where(lane < HEAD_DIM, colv, zero), jnp.where(lane >= HEAD_DIM, colv, zero)], axis=0)

    def t_of_rows(r0, n):
        return t0 + ((r0 + lax.broadcasted_iota(jnp.int32, (n, 1), 0)) & (tq - 1))

    n_blk_rows = kmean_ref.shape[0]
    cur = (t0 + (lax.broadcasted_iota(jnp.int32, (1, 2 * tq), 1) & (tq - 1))) >> 8
    blk = lax.broadcasted_iota(jnp.int32, (n_blk_rows, 2 * tq), 0)
    past = blk < cur
    gate = jnp.where(past, _dot_nt(kmean_ref[...], qp), NEG_INF)
    sel = _top_k_flags(gate, blk, top)
    bias_t = jnp.where((sel & past) | (blk == cur), 0.0, MASK_BIAS)
    if n_blk_rows < LANES:
        bias_t = jnp.concatenate([bias_t, jnp.full((LANES - n_blk_rows, 2 * tq), MASK_BIAS, F32)], axis=0)
    lhs_ref[0] = jnp.concatenate([qp, bias_t.T.astype(BF16)], axis=1)
    o = _flash_body(lhs_ref, k_ref, e_ref, va_ref, vb_ref, t_of_rows, t0 // K_TILE, MOBA_BLOCK * LANES, scratch)
    a = jnp.where(lane < HEAD_DIM, o[:tq], o[tq:])
    out_ref[...] = (a * sz_ref[...].astype(F32)).astype(out_ref.dtype)


def _moba(mq, kmean, mk, mv, e_blk, sz, top):
    T = mq.shape[0]
    tq = MOBA_Q_TILE
    assert MOBA_BLOCK == 256 and T * 1 <= MOBA_BLOCK * LANES
    rows = 2 * tq
    row = pl.BlockSpec((tq, LANES), lambda p, qi: (qi, p))
    col = pl.BlockSpec((T, LANES), lambda p, qi: (0, p))
    if isinstance(mv, tuple):
        va, vb = mv
    else:
        v4 = mv.reshape(T, MOBA_HEADS // 2, 2, HEAD_DIM)
        ones = jnp.ones((T, MOBA_HEADS // 2, HEAD_DIM), mv.dtype)
        va = jnp.stack([v4[:, :, 0], ones], axis=2).reshape(T, 512)
        vb = jnp.stack([ones, v4[:, :, 1]], axis=2).reshape(T, 512)
    return pl.pallas_call(
        functools.partial(_moba_kernel, top=top),
        grid=(MOBA_HEADS // 2, T // tq),
        in_specs=[row, pl.BlockSpec((kmean.shape[0], LANES), lambda p, qi: (0, p)), col, col, col,
                  pl.BlockSpec((T, LANES), lambda p, qi: (0, 0)), row],
        out_specs=row,
        out_shape=jax.ShapeDtypeStruct((T, 512), BF16),
        scratch_shapes=[pltpu.VMEM((1, rows, 2 * LANES), BF16)] + _flash_scratch(rows),
        compiler_params=_params("arbitrary", "arbitrary"),
        name="moba",
    )(mq, kmean, mk, va, vb, e_blk, sz)


def _mem_kv_kernel(mem_ref, g_ref, w0_ref, w1_ref, kv0_ref, kv1_ref):
    mem_n = _rms(mem_ref[...], g_ref[...]).astype(BF16)
    kv0_ref[...] = _dot(mem_n, w0_ref[...]).astype(kv0_ref.dtype)
    kv1_ref[...] = _dot(mem_n, w1_ref[...]).astype(kv1_ref.dtype)


def _mem_kv(mem, g, w0, w1):
    n = mem.shape[0]
    kv_w = 2 * MEM_HEADS * HEAD_DIM
    return pl.pallas_call(
        _mem_kv_kernel,
        grid=(1,),
        in_specs=[
            pl.BlockSpec((n, D_MODEL), lambda i: (0, 0)),
            pl.BlockSpec((1, D_MODEL), lambda i: (0, 0)),
            pl.BlockSpec((D_MODEL, kv_w), lambda i: (0, 0)),
            pl.BlockSpec((D_MODEL, kv_w), lambda i: (0, 0)),
        ],
        out_specs=[pl.BlockSpec((n, kv_w), lambda i: (0, 0)), pl.BlockSpec((n, kv_w), lambda i: (0, 0))],
        out_shape=[jax.ShapeDtypeStruct((n, kv_w), BF16), jax.ShapeDtypeStruct((n, kv_w), BF16)],
        compiler_params=_params("arbitrary"),
        name="mem_kv",
    )(mem, g, w0, w1)


def _mem_attention(eq, kv, sez, lane):
    rows = eq.shape[0]
    mem_w = MEM_HEADS * HEAD_DIM
    cols = []
    for p in range(MEM_HEADS // 2):
        cs = slice(p * LANES, (p + 1) * LANES)
        colv = eq[:, cs]
        zero = jnp.zeros_like(colv)
        qp = jnp.concatenate([jnp.where(lane < HEAD_DIM, colv, zero), jnp.where(lane >= HEAD_DIM, colv, zero)], axis=0)
        s = _dot_nt(qp, kv[:, cs])
        pe = jnp.exp(s - jnp.max(s, axis=-1, keepdims=True))
        o = _dot(pe.astype(BF16), kv[:, mem_w + p * LANES:mem_w + (p + 1) * LANES]) / jnp.sum(pe, axis=-1, keepdims=True)
        cols.append(jnp.where(lane < HEAD_DIM, o[:rows], o[rows:]) * sez[:, cs].astype(F32))
    return jnp.concatenate(cols, axis=1).astype(BF16)


def _out0_kernel(ag_ref, bg_ref, eq_ref, sez_ref, kv_ref, wa_ref, wb_ref, wm_ref, x_ref, out_ref):
    rows = x_ref.shape[0]
    lane = lax.broadcasted_iota(jnp.int32, (rows, LANES), 1)
    mg = _mem_attention(eq_ref[...], kv_ref[...], sez_ref[...], lane)
    y = _dot(ag_ref[...], wa_ref[...]) + _dot(bg_ref[...], wb_ref[...]) + _dot(mg, wm_ref[...])
    out_ref[...] = x_ref[...] + y


def _out0(ag, bg, eq, sez, kv, wa, wb, wm, x):
    T = x.shape[0]
    rt = PROJ_ROWS
    row = lambda w: pl.BlockSpec((rt, w), lambda i: (i, 0))
    full = lambda a: pl.BlockSpec(a.shape, lambda i: (0, 0))
    return pl.pallas_call(
        _out0_kernel,
        grid=(T // rt,),
        in_specs=[row(512), row(512), row(256), row(256), full(kv), full(wa), full(wb), full(wm), row(D_MODEL)],
        out_specs=row(D_MODEL),
        out_shape=jax.ShapeDtypeStruct((T, D_MODEL), F32),
        compiler_params=_params("arbitrary"),
        name="out_proj0",
    )(ag, bg, eq, sez, kv, wa, wb, wm, x)


def _out1_kernel(rg_ref, eq_ref, sez_ref, kv_ref, wr_ref, wm_ref, x_ref, g_ref, out_ref):
    rows = x_ref.shape[0]
    lane = lax.broadcasted_iota(jnp.int32, (rows, LANES), 1)
    mg = _mem_attention(eq_ref[...], kv_ref[...], sez_ref[...], lane)
    y = _dot(rg_ref[...], wr_ref[...]) + _dot(mg, wm_ref[...])
    out_ref[...] = _rms(x_ref[...] + y, g_ref[...])


def _out1(rg, eq, sez, kv, wr, wm, x, g):
    T = x.shape[0]
    rt = PROJ_ROWS
    row = lambda w: pl.BlockSpec((rt, w), lambda i: (i, 0))
    full = lambda a: pl.BlockSpec(a.shape, lambda i: (0, 0))
    return pl.pallas_call(
        _out1_kernel,
        grid=(T // rt,),
        in_specs=[row(RET_HEADS * RET_V_DIM), row(256), row(256), full(kv), full(wr), full(wm), row(D_MODEL), full(g)],
        out_specs=row(D_MODEL),
        out_shape=jax.ShapeDtypeStruct((T, D_MODEL), F32),
        compiler_params=_params("arbitrary"),
        name="out_proj1_norm",
    )(rg, eq, sez, kv, wr, wm, x, g)


def _retention_kernel(q_ref, k_ref, v_ref, sz_ref, decay_ref, qd_ref, kd_ref, cd_ref, out_ref, state_ref):
    @pl.when(pl.program_id(0) == 0)
    def _():
        state_ref[...] = jnp.zeros(state_ref.shape, F32)

    for h in range(RET_HEADS):
        qs = slice(h * RET_QK_DIM, (h + 1) * RET_QK_DIM)
        vsl = slice(h * RET_V_DIM, (h + 1) * RET_V_DIM)
        qh = q_ref[:, qs]
        kh = k_ref[:, qs]
        vh = v_ref[:, vsl]
        st = state_ref[h]
        inner = _dot_nt(qh, kh.astype(BF16)) * decay_ref[h]
        o = _dot(inner.astype(BF16), vh) + _dot(qh, st.astype(BF16)) * qd_ref[h]
        state_ref[h] = st * cd_ref[h] + _dot_tn((kh * kd_ref[h]).astype(BF16), vh)
        mu = jnp.mean(o, axis=-1, keepdims=True)
        d = o - mu
        on = d * lax.rsqrt(jnp.mean(d * d, axis=-1, keepdims=True) + NORM_EPS)
        out_ref[:, vsl] = (on * sz_ref[:, vsl].astype(F32)).astype(out_ref.dtype)


def _retention(rq, rk, rv, sz):
    T = rq.shape[0]
    C = RET_CHUNK
    log_g = jnp.log(1.0 - 2.0 ** (-5.0 - jnp.arange(RET_HEADS, dtype=F32)))
    i = jnp.arange(C, dtype=F32)
    diff = i[:, None] - i[None, :]
    decay = jnp.where(diff >= 0, jnp.exp(jnp.maximum(diff, 0.0)[None] * log_g[:, None, None]), 0.0)
    q_decay = jnp.exp((i + 1.0)[None, :] * log_g[:, None])[:, :, None]
    k_decay = jnp.exp((C - 1.0 - i)[None, :] * log_g[:, None])[:, :, None]
    chunk_decay = jnp.exp(C * log_g)[:, None, None]
    qk_w = RET_HEADS * RET_QK_DIM
    v_w = RET_HEADS * RET_V_DIM
    return pl.pallas_call(
        _retention_kernel,
        grid=(T // C,),
        in_specs=[
            pl.BlockSpec((C, qk_w), lambda c: (c, 0)),
            pl.BlockSpec((C, qk_w), lambda c: (c, 0)),
            pl.BlockSpec((C, v_w), lambda c: (c, 0)),
            pl.BlockSpec((C, v_w), lambda c: (c, 0)),
            pl.BlockSpec((RET_HEADS, C, C), lambda c: (0, 0, 0)),
            pl.BlockSpec((RET_HEADS, C, 1), lambda c: (0, 0, 0)),
            pl.BlockSpec((RET_HEADS, C, 1), lambda c: (0, 0, 0)),
            pl.BlockSpec((RET_HEADS, 1, 1), lambda c: (0, 0, 0)),
        ],
        out_specs=pl.BlockSpec((C, v_w), lambda c: (c, 0)),
        out_shape=jax.ShapeDtypeStruct((T, v_w), BF16),
        scratch_shapes=[pltpu.VMEM((RET_HEADS, RET_QK_DIM, RET_V_DIM), F32)],
        compiler_params=_params("arbitrary"),
        name="retention",
    )(rq, rk, rv, sz, decay, q_decay, k_decay, chunk_decay)


def _head_cols(order):
    return np.concatenate([np.arange(h * HEAD_DIM, (h + 1) * HEAD_DIM) for h in order])


def _one_hot_blocks(T, block):
    b = (np.arange(T) // block) % LANES
    return jnp.asarray(np.eye(LANES, dtype=np.float32)[b], dtype=BF16)


def kernel(x, mem, positions, l0_norm_g, l0_w_in, l0_nsa_gate_b, l0_cmp_pe_k, l0_cmp_w1_k, l0_cmp_w2_k,
           l0_cmp_pe_v, l0_cmp_w1_v, l0_cmp_w2_v, l0_w_mem_kv, l0_w_out,
           l1_norm_g, l1_w_in, l1_w_mem_kv, l1_w_out, mem_norm_g, final_norm_g):
    B, T, _ = x.shape
    assert B == 1 and T % K_TILE == 0 and T >= NSA_WINDOW + Q_TILE and NSA_WINDOW % Q_TILE == 0
    assert T % RET_CHUNK == 0 and T % MOBA_Q_TILE == 0
    n_sel = T // NSA_SEL_LEN
    n_selp = -(-n_sel // LANES) * LANES
    k_sel = min(NSA_SEL_TOPK, n_sel)
    assert k_sel >= 3
    n_blk = T // MOBA_BLOCK
    assert n_blk <= LANES
    moba_top = max(1, min(MOBA_TOPK, n_blk - 1))
    x2 = x[0]
    pos = positions[0].astype(F32)

    attn_inv = 1.0 / (ROPE_THETA ** (jnp.arange(0, HEAD_DIM, 2, dtype=F32) / HEAD_DIM))
    ret_inv = 1.0 / (ROPE_THETA ** jnp.linspace(0.0, 1.0, RET_QK_DIM // 2, dtype=F32))
    ang = pos[:, None] * attn_inv
    cos64 = jnp.tile(jnp.cos(ang), (1, 4))
    sin64 = jnp.tile(jnp.concatenate([-jnp.sin(ang), jnp.sin(ang)], axis=1), (1, 2))
    rang = pos[:, None] * ret_inv
    cos256, sin256 = jnp.cos(rang), jnp.sin(rang)

    offs = np.cumsum([0, 512, 128, 128, 128, 128, 128, 128, 24, 512, 512, 512, 512, 512, 256, 256])
    nsa_cols = _head_cols(NSA_HEAD_ORDER)
    col_idx = np.concatenate([
        offs[0] + nsa_cols,
        np.arange(offs[1], offs[7]),
    ])
    w0 = jnp.concatenate([
        l0_w_in[:, col_idx],
        l0_w_in[:, offs[7]:offs[8]], jnp.zeros((D_MODEL, LANES - 24), F32),
        l0_w_in[:, offs[8] + nsa_cols],
        l0_w_in[:, offs[9]:],
    ], axis=1).astype(BF16)
    gate_b = jnp.concatenate([l0_nsa_gate_b, jnp.zeros((LANES - 24,), F32)])[None, :]
    groups0 = [
        (512, "rot64", LOG2_SCALE), (128, "rot64", 1.0), (128, "plain", 1.0), (128, "rot64", 1.0),
        (128, "ones_pair", 1.0), (128, "rot64", 1.0), (128, "ones_pair", 1.0), (128, "gate", 1.0), (512, "silu", 1.0),
        (512, "rot64", LOG2_SCALE), (512, "rot64_mean", 1.0), (512, "ones_pair", 1.0), (512, "silu", 1.0),
        (256, "plain", 0.125), (256, "silu", 1.0),
    ]
    dts0 = [BF16] * 7 + [F32] + [BF16] * 7
    (nq, nkc, nvc, nks, nvs_a, nvs_b, nkw, nvw_a, nvw_b, gates, snz, mq, mk, kmean, mv_a, mv_b, smz, eq0, sez0) = _proj(
        x2, l0_norm_g[None, :], w0, cos64, sin64, gate_b, groups0, dts0, "proj0")

    kv0, kv1 = _mem_kv(mem[0], mem_norm_g[None, :], l0_w_mem_kv.astype(BF16), l1_w_mem_kv.astype(BF16))

    kc = _compress(nkc, l0_cmp_pe_k, l0_cmp_w1_k, l0_cmp_w2_k, "compress_k")
    vc = _compress(nvc, l0_cmp_pe_v, l0_cmp_w1_v, l0_cmp_w2_v, "compress_v")
    n_cmp_rows = T // NSA_CMP_STRIDE
    cmp_start = np.arange(n_cmp_rows)[:, None] * NSA_CMP_STRIDE
    sel_start = np.arange(n_selp)[None, :] * NSA_SEL_LEN
    overlap = ((cmp_start < sel_start + NSA_SEL_LEN) & (cmp_start + NSA_CMP_LEN > sel_start)
               & (np.arange(n_selp)[None, :] < n_sel))
    ov = jnp.asarray(overlap.T.astype(np.float32), dtype=BF16)
    ocw, sel_bias = _nsa_cmp(nq, kc, vc, ov, nkw, (nvw_a, nvw_b), gates, k_sel)
    ag = _nsa_sel(nq, sel_bias, nks, (nvs_a, nvs_b), _one_hot_blocks(T, NSA_SEL_LEN), gates, ocw, snz)

    n_blk_rows = -(-n_blk // 16) * 16
    kmean_p = jnp.concatenate([kmean[:, 0, :], jnp.zeros((n_blk_rows - n_blk, 512), F32)], axis=0).astype(BF16)
    bg = _moba(mq, kmean_p, mk, (mv_a, mv_b), _one_hot_blocks(T, MOBA_BLOCK), smz, moba_top)

    wa = l0_w_out[:512][nsa_cols].astype(BF16)
    wb = l0_w_out[512:1024].astype(BF16)
    wm0 = l0_w_out[1024:].astype(BF16)
    x1 = _out0(ag, bg, eq0, sez0, kv0, wa, wb, wm0, x2)

    groups1 = [
        (1024, "rot256", 1.0), (1024, "rot256", RET_QK_DIM ** -0.5), (2048, "plain", 1.0), (2048, "silu", 1.0),
        (256, "plain", 0.125), (256, "silu", 1.0),
    ]
    dts1 = [BF16, F32, BF16, BF16, BF16, BF16]
    rq, rk, rv, srz, eq1, sez1 = _proj(x1, l1_norm_g[None, :], l1_w_in.astype(BF16), cos256, sin256,
                                       jnp.zeros((1, LANES), F32), groups1, dts1, "proj1")
    rg = _retention(rq, rk, rv, srz)
    v_w = RET_HEADS * RET_V_DIM
    out = _out1(rg, eq1, sez1, kv1, l1_w_out[:v_w].astype(BF16), l1_w_out[v_w:].astype(BF16), x1,
                final_norm_g[None, :])
    return out[None]
```

```python
import functools

import numpy as np
import jax
import jax.numpy as jnp
from jax import lax
from jax.experimental import pallas as pl
from jax.experimental.pallas import tpu as pltpu

F32 = jnp.float32
BF16 = jnp.bfloat16

D_MODEL = 1024
HEAD_DIM = 64
ROPE_THETA = 10000.0
NORM_EPS = 1e-6
NEG_INF = -1e30
MASK_BIAS = NEG_INF
LOG2_SCALE = HEAD_DIM ** -0.5 * float(np.log2(np.e))

NSA_HEADS = 8
NSA_GROUP_HEADS = 4
NSA_CMP_LEN = 32
NSA_CMP_STRIDE = 16
NSA_SEL_LEN = 64
NSA_SEL_TOPK = 16
NSA_WINDOW = 512
MOBA_HEADS = 8
MOBA_BLOCK = 256
MOBA_TOPK = 3
MEM_LEN = 256
MEM_HEADS = 4
RET_HEADS = 4
RET_QK_DIM = 256
RET_V_DIM = 512
RET_CHUNK = 256

LANES = 128
PROJ_ROWS = 256
PROJ_COLS = 512
OUT_ROWS = 512
Q_TILE = 128
K_TILE = 512
VMEM_LIMIT = 56 * 1024 * 1024

NSA_HEAD_ORDER = (0, 4, 1, 5, 2, 6, 3, 7)


def _dot(a, b):
    return jnp.dot(a, b, preferred_element_type=F32)


def _dot_nt(a, b):
    return lax.dot_general(a, b, (((1,), (1,)), ((), ())), preferred_element_type=F32)


def _dot_tn(a, b):
    return lax.dot_general(a, b, (((0,), (0,)), ((), ())), preferred_element_type=F32)


def _silu(y):
    return y * (1.0 / (1.0 + jnp.exp(-y)))


def _rms(x, g):
    return x * lax.rsqrt(jnp.mean(x * x, axis=-1, keepdims=True) + NORM_EPS) * g


def _params(*sem):
    return pltpu.CompilerParams(dimension_semantics=sem, vmem_limit_bytes=VMEM_LIMIT)


def _proj_kernel(x_ref, g_ref, w_ref, c_ref, s_ref, b_ref, *out_refs, groups):
    h = _rms(x_ref[...], g_ref[...]).astype(BF16)
    rows = x_ref.shape[0]
    lane = lax.broadcasted_iota(jnp.int32, (rows, LANES), 1)
    low_half = (lane & 32) == 0
    pieces = []
    col = 0
    oi = 0
    for width, kind, scale in groups:
        out_ref = out_refs[oi]
        oi += 1
        mean_ref = None
        if kind in ("rot64_mean", "ones_pair"):
            mean_ref = out_refs[oi]
            oi += 1
        pw = RET_QK_DIM if kind == "rot256" else LANES
        for c0 in range(0, width, pw):
            pieces.append((col + c0, pw, kind, scale, out_ref, c0, mean_ref))
        col += width

    def epilogue(z, kind, scale):
        if kind in ("rot64", "rot64_mean"):
            partner = jnp.where(low_half, pltpu.roll(z, LANES - 32, 1), pltpu.roll(z, 32, 1))
            z = z * c_ref[...] + partner * s_ref[...]
        elif kind == "rot256":
            x1, x2 = z[:, :LANES], z[:, LANES:]
            z = jnp.concatenate([x1 * c_ref[...] - x2 * s_ref[...], x1 * s_ref[...] + x2 * c_ref[...]], axis=1)
        elif kind == "silu":
            z = _silu(z)
        elif kind == "gate":
            z = 1.0 / (1.0 + jnp.exp(-(z + b_ref[...])))
        return z if scale == 1.0 else z * scale

    i = 0
    while i < len(pieces):
        j, run_w = i, 0
        while j < len(pieces) and run_w + pieces[j][1] <= PROJ_COLS:
            run_w += pieces[j][1]
            j += 1
        c_start = pieces[i][0]
        y = _dot(h, w_ref[:, c_start:c_start + run_w])
        for pc, pw, kind, scale, out_ref, oc, mean_ref in pieces[i:j]:
            z = epilogue(y[:, pc - c_start:pc - c_start + pw], kind, scale)
            if kind == "ones_pair":
                out_ref[:, oc:oc + pw] = jnp.where(lane < HEAD_DIM, z, 1.0).astype(out_ref.dtype)
                mean_ref[:, oc:oc + pw] = jnp.where(lane >= HEAD_DIM, z, 1.0).astype(mean_ref.dtype)
                continue
            if mean_ref is not None:
                mean_ref[0, :, oc:oc + pw] = jnp.mean(z, axis=0, keepdims=True)
            out_ref[:, oc:oc + pw] = z.astype(out_ref.dtype)
        i = j


def _proj(x, g, w, cos_t, sin_t, bias, groups, out_dtypes, name):
    T = x.shape[0]
    n_tiles = T // PROJ_ROWS
    out_shapes = []
    out_specs = []
    for (width, kind, _), dt in zip(groups, out_dtypes):
        out_shapes.append(jax.ShapeDtypeStruct((T, width), dt))
        out_specs.append(pl.BlockSpec((PROJ_ROWS, width), lambda i: (i, 0)))
        if kind == "rot64_mean":
            out_shapes.append(jax.ShapeDtypeStruct((n_tiles, 1, width), F32))
            out_specs.append(pl.BlockSpec((1, 1, width), lambda i: (i, 0, 0)))
        if kind == "ones_pair":
            out_shapes.append(jax.ShapeDtypeStruct((T, width), dt))
            out_specs.append(pl.BlockSpec((PROJ_ROWS, width), lambda i: (i, 0)))
    return pl.pallas_call(
        functools.partial(_proj_kernel, groups=tuple(groups)),
        grid=(n_tiles,),
        in_specs=[
            pl.BlockSpec((PROJ_ROWS, D_MODEL), lambda i: (i, 0)),
            pl.BlockSpec((1, D_MODEL), lambda i: (0, 0)),
            pl.BlockSpec(w.shape, lambda i: (0, 0)),
            pl.BlockSpec((PROJ_ROWS, LANES), lambda i: (i, 0)),
            pl.BlockSpec((PROJ_ROWS, LANES), lambda i: (i, 0)),
            pl.BlockSpec((1, LANES), lambda i: (0, 0)),
        ],
        out_specs=out_specs,
        out_shape=out_shapes,
        compiler_params=_params("arbitrary"),
        name=name,
    )(x, g, w, cos_t, sin_t, bias)


def _compress_kernel(a_ref, an_ref, pea_ref, peb_ref, w1a_ref, w1b_ref, w2_ref, out_ref):
    xa = (a_ref[...].astype(F32) + pea_ref[...]).astype(BF16)
    xb = (an_ref[...].astype(F32) + peb_ref[...]).astype(BF16)
    hid = _silu(_dot(xa, w1a_ref[...]) + _dot(xb, w1b_ref[...]))
    out_ref[...] = _dot(hid.astype(BF16), w2_ref[...]).astype(out_ref.dtype)


def _compress(a, pe, w1, w2, name):
    T = a.shape[0]
    n_rows = T // NSA_CMP_STRIDE
    row_w = NSA_CMP_STRIDE * LANES
    a2 = a.reshape(n_rows, row_w)
    a2n = jnp.concatenate([a2[1:], jnp.zeros((1, row_w), a2.dtype)], axis=0)
    eye2 = jnp.eye(2, dtype=F32)

    def expand(w1_half):
        return jnp.einsum("lde,gh->lgdhe", w1_half, eye2).reshape(row_w, LANES).astype(BF16)

    def pe_row(pe_half):
        return jnp.broadcast_to(pe_half[:, None, :], (NSA_CMP_STRIDE, 2, HEAD_DIM)).reshape(1, row_w)

    w2bd = jnp.einsum("ef,gh->gehf", w2, eye2).reshape(LANES, LANES).astype(BF16)
    tile = min(256, n_rows)
    return pl.pallas_call(
        _compress_kernel,
        grid=(n_rows // tile,),
        in_specs=[
            pl.BlockSpec((tile, row_w), lambda i: (i, 0)),
            pl.BlockSpec((tile, row_w), lambda i: (i, 0)),
            pl.BlockSpec((1, row_w), lambda i: (0, 0)),
            pl.BlockSpec((1, row_w), lambda i: (0, 0)),
            pl.BlockSpec((row_w, LANES), lambda i: (0, 0)),
            pl.BlockSpec((row_w, LANES), lambda i: (0, 0)),
            pl.BlockSpec((LANES, LANES), lambda i: (0, 0)),
        ],
        out_specs=pl.BlockSpec((tile, LANES), lambda i: (i, 0)),
        out_shape=jax.ShapeDtypeStruct((n_rows, LANES), BF16),
        compiler_params=_params("arbitrary"),
        name=name,
    )(a2, a2n, pe_row(pe[:NSA_CMP_STRIDE]), pe_row(pe[NSA_CMP_STRIDE:]),
      expand(w1[:NSA_CMP_STRIDE]), expand(w1[NSA_CMP_STRIDE:]), w2bd)


def _nsa_head_query(q, h, lane):
    colv = q[:, (h % 4) * LANES:(h % 4 + 1) * LANES]
    keep = (lane < HEAD_DIM) if h < NSA_GROUP_HEADS else (lane >= HEAD_DIM)
    return jnp.where(keep, colv, jnp.zeros_like(colv))


def _nsa_stacked_queries(q, lane):
    return jnp.concatenate([_nsa_head_query(q, h, lane) for h in range(NSA_HEADS)], axis=0)


def _nsa_interleave_heads(o, gates, gate0, lane, out_ref, accumulate=False):
    tq = out_ref.shape[0]
    outs = [o[h * tq:(h + 1) * tq] * gates[:, gate0 + h:gate0 + h + 1] for h in range(NSA_HEADS)]
    for j in range(4):
        cs = slice(j * LANES, (j + 1) * LANES)
        a = jnp.where(lane < HEAD_DIM, outs[j], outs[4 + j])
        out_ref[:, cs] = out_ref[:, cs] + a if accumulate else a


def _values_with_ones(v):
    ones = jnp.ones((v.shape[0], HEAD_DIM), v.dtype)
    return jnp.concatenate([v[:, :HEAD_DIM], ones], axis=1), jnp.concatenate([ones, v[:, HEAD_DIM:]], axis=1)


def _top_k_flags(x, idx, k):
    sel = jnp.zeros(x.shape, jnp.bool_)
    for _ in range(k):
        mx = jnp.max(x, axis=0, keepdims=True)
        first = jnp.min(jnp.where(x == mx, idx, x.shape[0]), axis=0, keepdims=True)
        hit = idx == first
        sel = sel | hit
        x = jnp.where(hit, -jnp.inf, x)
    return sel


CMP_KEY_BUCKET = 128


def _nsa_cmp_kernel(q_ref, kc_ref, vca_ref, vcb_ref, ovt_ref, kw_ref, wa_ref, wb_ref, gate_ref, oc_ref, bias_ref,
                    *, k_sel):
    tq = q_ref.shape[0]
    n_cmp = kc_ref.shape[0]
    n_selp = ovt_ref.shape[0]
    rows = NSA_HEADS * tq
    half = rows // 2
    t0 = pl.program_id(0) * tq
    lane = lax.broadcasted_iota(jnp.int32, (tq, LANES), 1)
    t_rows = t0 + (lax.broadcasted_iota(jnp.int32, (rows, 1), 0) & (tq - 1))

    span = NSA_WINDOW + tq
    start = pl.multiple_of(jnp.maximum(t0 - NSA_WINDOW, 0), tq)

    def window(mask_scores):
        s = mask_scores(_dot_nt(_nsa_stacked_queries(q_ref[...], lane), kw_ref[pl.ds(start, span), :]))
        pb = jnp.exp2(s - jnp.max(s, axis=-1, keepdims=True)).astype(BF16)
        acc = jnp.concatenate([_dot(pb[:half], wa_ref[pl.ds(start, span), :]),
                               _dot(pb[half:], wb_ref[pl.ds(start, span), :])], axis=0)
        _nsa_interleave_heads(acc / pltpu.roll(acc, HEAD_DIM, 1), gate_ref[...], 2 * NSA_HEADS, lane, oc_ref)

    @pl.when(t0 >= NSA_WINDOW)
    def _():
        q_local = lax.broadcasted_iota(jnp.int32, (rows, 1), 0) & (tq - 1)
        k_local = lax.broadcasted_iota(jnp.int32, (1, tq), 1)
        window(lambda s: jnp.concatenate([jnp.where(k_local > q_local, s[:, :tq], NEG_INF), s[:, tq:NSA_WINDOW],
                                          jnp.where(k_local <= q_local, s[:, NSA_WINDOW:], NEG_INF)], axis=1))

    @pl.when(t0 < NSA_WINDOW)
    def _():
        wpos = lax.broadcasted_iota(jnp.int32, (1, span), 1)
        window(lambda s: jnp.where(wpos <= t_rows, s, NEG_INF))

    def attend(nk):
        cmp_end = lax.broadcasted_iota(jnp.int32, (1, nk), 1) * NSA_CMP_STRIDE + (NSA_CMP_LEN - 1)
        s = _dot_nt(_nsa_stacked_queries(q_ref[...], lane), kc_ref[0:nk, :])
        s = jnp.where(cmp_end <= t_rows, s, NEG_INF)
        m = jnp.max(s, axis=-1, keepdims=True)
        p = jnp.exp2(s - m)
        pb = p.astype(BF16)
        acc_a = _dot(pb[:half], vca_ref[0:nk, :])
        acc_b = _dot(pb[half:], vcb_ref[0:nk, :])
        l = jnp.concatenate([acc_a[:, HEAD_DIM:HEAD_DIM + 1], acc_b[:, 0:1]], axis=0)
        rl = jnp.where(m > 0.5 * NEG_INF, 1.0 / jnp.maximum(l, 1e-30), 0.0)
        _nsa_interleave_heads(jnp.concatenate([acc_a, acc_b], axis=0) * rl, gate_ref[...], 0, lane, oc_ref,
                              accumulate=True)
        pr = p * rl
        nb = min(n_selp, nk * NSA_CMP_STRIDE // NSA_SEL_LEN)
        blk = lax.broadcasted_iota(jnp.int32, (nb, tq), 0)
        cur = (t0 + lax.broadcasted_iota(jnp.int32, (1, tq), 1)) >> 6
        valid = blk <= cur
        forced = (blk == 0) | (blk == cur) | (blk == cur - 1)
        for g in range(2):
            r0 = g * half
            pg = pr[r0:r0 + tq] + pr[r0 + tq:r0 + 2 * tq] + pr[r0 + 2 * tq:r0 + 3 * tq] + pr[r0 + 3 * tq:r0 + 4 * tq]
            imp_t = _dot_nt(ovt_ref[0:nb, 0:nk], pg.astype(BF16))
            x = jnp.where(valid & jnp.logical_not(forced), imp_t, NEG_INF)
            sel = _top_k_flags(x, blk, k_sel - 3) | forced
            bias_t = jnp.where(sel & valid, 0.0, MASK_BIAS)
            if nb < n_selp:
                bias_t = jnp.concatenate([bias_t, jnp.full((n_selp - nb, tq), MASK_BIAS, F32)], axis=0)
            bias_ref[g] = bias_t.T.astype(BF16)

    needed = (t0 + tq - NSA_CMP_LEN) // NSA_CMP_STRIDE + 1
    buckets = sorted({min(b, n_cmp) for b in range(CMP_KEY_BUCKET, n_cmp + CMP_KEY_BUCKET, CMP_KEY_BUCKET)})
    for i, nk in enumerate(buckets):
        lo = buckets[i - 1] if i else -(1 << 30)
        cond = (needed > lo) if i == len(buckets) - 1 else ((needed > lo) & (needed <= nk))
        pl.when(cond)(functools.partial(attend, nk))


def _nsa_cmp(nq, kc, vc, ov, kw, vw, gates, k_sel):
    T = nq.shape[0]
    n_cmp = kc.shape[0]
    n_selp = ov.shape[0]
    vca, vcb = _values_with_ones(vc)
    wa, wb = vw if isinstance(vw, tuple) else _values_with_ones(vw)
    cmp_full = pl.BlockSpec((n_cmp, LANES), lambda i: (0, 0))
    full = pl.BlockSpec((T, LANES), lambda i: (0, 0))
    return pl.pallas_call(
        functools.partial(_nsa_cmp_kernel, k_sel=k_sel),
        grid=(T // Q_TILE,),
        in_specs=[
            pl.BlockSpec((Q_TILE, 512), lambda i: (i, 0)),
            cmp_full, cmp_full, cmp_full,
            pl.BlockSpec((n_selp, n_cmp), lambda i: (0, 0)),
            full, full, full,
            pl.BlockSpec((Q_TILE, LANES), lambda i: (i, 0)),
        ],
        out_specs=[
            pl.BlockSpec((Q_TILE, 512), lambda i: (i, 0)),
            pl.BlockSpec((2, Q_TILE, n_selp), lambda i: (0, i, 0)),
        ],
        out_shape=[
            jax.ShapeDtypeStruct((T, 512), F32),
            jax.ShapeDtypeStruct((2, T, n_selp), BF16),
        ],
        compiler_params=_params("arbitrary"),
        name="nsa_cmp_win_select",
    )(nq, kc, vca, vcb, ov, kw, wa, wb, gates)


def _flash_scratch(rows):
    return [
        pltpu.VMEM((rows, K_TILE), F32), pltpu.VMEM((rows, K_TILE), F32),
        pltpu.VMEM((rows, K_TILE), BF16), pltpu.VMEM((rows, K_TILE), BF16),
        pltpu.VMEM((rows, 1), F32), pltpu.VMEM((rows, 1), F32),
        pltpu.VMEM((rows, 1), F32),
        pltpu.VMEM((rows, LANES), F32),
    ]


def _flash_body(lhs_ref, k_ref, e_ref, va_ref, vb_ref, t_of_rows, n_full, tokens_per_half, scratch):
    s_a, s_b, p_a, p_b, al_a, al_b, m_ref, acc_ref = scratch
    tk = K_TILE
    rows = acc_ref.shape[0]
    half_rows = rows // 2
    last_mem_tile = k_ref.shape[0] // tk - 1

    def tile_start(j):
        return pl.multiple_of(jnp.clip(j, 0, last_mem_tile) * tk, tk)

    def scores(j, s_out):
        k0 = tile_start(j)
        rhs = jnp.concatenate([k_ref[pl.ds(k0, tk), :], e_ref[pl.ds(k0, tk), :]], axis=1)
        s_out[...] = _dot_nt(lhs_ref[k0 // tokens_per_half], rhs)

    def softmax(j, s_in, p_out, al_out, causal):
        s = s_in[...]
        if causal:
            kpos = j * tk + lax.broadcasted_iota(jnp.int32, (1, tk), 1)
            s = jnp.where(kpos <= t_of_rows(0, rows), s, NEG_INF)
        m_old = m_ref[...]
        m_new = jnp.maximum(m_old, jnp.max(s, axis=-1, keepdims=True))
        m_ref[...] = m_new
        al_out[...] = jnp.exp2(m_old - m_new)
        p_out[...] = jnp.exp2(s - m_new).astype(BF16)

    def values(j, p_in, al_in):
        k0 = tile_start(j)
        for rs, v_ref in ((slice(0, half_rows), va_ref), (slice(half_rows, rows), vb_ref)):
            acc_ref[rs, :] = al_in[rs, :] * acc_ref[rs, :] + _dot(p_in[rs, :], v_ref[pl.ds(k0, tk), :])

    m_ref[...] = jnp.full(m_ref.shape, NEG_INF, F32)
    acc_ref[...] = jnp.zeros(acc_ref.shape, F32)
    p_b[...] = jnp.zeros(p_b.shape, BF16)
    al_b[...] = jnp.ones(al_b.shape, F32)
    scores(0, s_a)

    def two_full_tiles(i, carry):
        j = 2 * i
        values(j - 1, p_b, al_b)
        scores(j + 1, s_b)
        softmax(j, s_a, p_a, al_a, False)
        values(j, p_a, al_a)
        scores(j + 2, s_a)
        softmax(j + 1, s_b, p_b, al_b, False)
        return carry

    n_pairs = n_full // 2
    lax.fori_loop(0, n_pairs, two_full_tiles, 0)
    j = 2 * n_pairs
    odd = (n_full & 1) == 1

    @pl.when(odd)
    def _():
        values(j - 1, p_b, al_b)
        scores(j + 1, s_b)
        softmax(j, s_a, p_a, al_a, True)
        values(j, p_a, al_a)
        softmax(j + 1, s_b, p_b, al_b, True)
        values(j + 1, p_b, al_b)

    @pl.when(jnp.logical_not(odd))
    def _():
        values(j - 1, p_b, al_b)
        softmax(j, s_a, p_a, al_a, True)
        values(j, p_a, al_a)

    acc = acc_ref[...]
    return acc / pltpu.roll(acc, HEAD_DIM, 1)


def _nsa_sel_kernel(q_ref, bias_ref, ks_ref, va_ref, vb_ref, e_ref, gate_ref, ocw_ref, sz_ref, out_ref,
                    lhs_ref, *scratch):
    tq = q_ref.shape[0]
    t0 = pl.program_id(0) * tq
    lane = lax.broadcasted_iota(jnp.int32, (tq, LANES), 1)
    q = q_ref[...]
    for half in range(lhs_ref.shape[0]):
        for h in range(NSA_HEADS):
            b = bias_ref[h // NSA_GROUP_HEADS, :, half * LANES:(half + 1) * LANES]
            lhs_ref[half, h * tq:(h + 1) * tq, :] = jnp.concatenate([_nsa_head_query(q, h, lane), b], axis=1)

    def t_of_rows(r0, n):
        return t0 + ((r0 + lax.broadcasted_iota(jnp.int32, (n, 1), 0)) & (tq - 1))

    o = _flash_body(lhs_ref, ks_ref, e_ref, va_ref, vb_ref, t_of_rows, t0 // K_TILE, NSA_SEL_LEN * LANES, scratch)
    gates = gate_ref[...]
    outs = [o[h * tq:(h + 1) * tq] * gates[:, NSA_HEADS + h:NSA_HEADS + h + 1] for h in range(NSA_HEADS)]
    for j in range(4):
        cs = slice(j * LANES, (j + 1) * LANES)
        a = jnp.where(lane < HEAD_DIM, outs[j], outs[4 + j]) + ocw_ref[:, cs]
        out_ref[:, cs] = (a * sz_ref[:, cs].astype(F32)).astype(out_ref.dtype)


def _nsa_sel(nq, bias, ks, vs, e_sel, gates, ocw, sz):
    T = nq.shape[0]
    n_selp = bias.shape[2]
    rows = NSA_HEADS * Q_TILE
    row = lambda w: pl.BlockSpec((Q_TILE, w), lambda qi: (qi, 0))
    full = pl.BlockSpec((T, LANES), lambda qi: (0, 0))
    va, vb = vs if isinstance(vs, tuple) else _values_with_ones(vs)
    return pl.pallas_call(
        _nsa_sel_kernel,
        grid=(T // Q_TILE,),
        in_specs=[row(512), pl.BlockSpec((2, Q_TILE, n_selp), lambda qi: (0, qi, 0)), full, full, full, full,
                  row(LANES), row(512), row(512)],
        out_specs=row(512),
        out_shape=jax.ShapeDtypeStruct((T, 512), BF16),
        scratch_shapes=[pltpu.VMEM((n_selp // LANES, rows, 2 * LANES), BF16)] + _flash_scratch(rows),
        compiler_params=_params("arbitrary"),
        name="nsa_selected",
    )(nq, bias, ks, va, vb, e_sel, gates, ocw, sz)


MOBA_Q_TILE = 512


def _moba_kernel(q_ref, kmean_ref, k_ref, va_ref, vb_ref, e_ref, sz_ref, out_ref, lhs_ref, *scratch, top):
    tq = q_ref.shape[0]
    t0 = pl.program_id(1) * tq
    lane = lax.broadcasted_iota(jnp.int32, (tq, LANES), 1)
    colv = q_ref[...]
    zero = jnp.zeros_like(colv)
    qp = jnp.concatenate([jnp.where(lane < HEAD_DIM, colv, zero), jnp.where(lane >= HEAD_DIM, colv, zero)], axis=0)

    def t_of_rows(r0, n):
        return t0 + ((r0 + lax.broadcasted_iota(jnp.int32, (n, 1), 0)) & (tq - 1))

    n_blk_rows = kmean_ref.shape[0]
    cur = (t0 + (lax.broadcasted_iota(jnp.int32, (1, 2 * tq), 1) & (tq - 1))) >> 8
    blk = lax.broadcasted_iota(jnp.int32, (n_blk_rows, 2 * tq), 0)
    past = blk < cur
    gate = jnp.where(past, _dot_nt(kmean_ref[...], qp), NEG_INF)
    sel = _top_k_flags(gate, blk, top)
    bias_t = jnp.where((sel & past) | (blk == cur), 0.0, MASK_BIAS)
    if n_blk_rows < LANES:
        bias_t = jnp.concatenate([bias_t, jnp.full((LANES - n_blk_rows, 2 * tq), MASK_BIAS, F32)], axis=0)
    lhs_ref[0] = jnp.concatenate([qp, bias_t.T.astype(BF16)], axis=1)
    o = _flash_body(lhs_ref, k_ref, e_ref, va_ref, vb_ref, t_of_rows, t0 // K_TILE, MOBA_BLOCK * LANES, scratch)
    a = jnp.where(lane < HEAD_DIM, o[:tq], o[tq:])
    out_ref[...] = (a * sz_ref[...].astype(F32)).astype(out_ref.dtype)


def _moba(mq, kmean, mk, mv, e_blk, sz, top):
    T = mq.shape[0]
    tq = MOBA_Q_TILE
    assert MOBA_BLOCK == 256 and T * 1 <= MOBA_BLOCK * LANES
    rows = 2 * tq
    row = pl.BlockSpec((tq, LANES), lambda p, qi: (qi, p))
    col = pl.BlockSpec((T, LANES), lambda p, qi: (0, p))
    if isinstance(mv, tuple):
        va, vb = mv
    else:
        v4 = mv.reshape(T, MOBA_HEADS // 2, 2, HEAD_DIM)
        ones = jnp.ones((T, MOBA_HEADS // 2, HEAD_DIM), mv.dtype)
        va = jnp.stack([v4[:, :, 0], ones], axis=2).reshape(T, 512)
        vb = jnp.stack([ones, v4[:, :, 1]], axis=2).reshape(T, 512)
    return pl.pallas_call(
        functools.partial(_moba_kernel, top=top),
        grid=(MOBA_HEADS // 2, T // tq),
        in_specs=[row, pl.BlockSpec((kmean.shape[0], LANES), lambda p, qi: (0, p)), col, col, col,
                  pl.BlockSpec((T, LANES), lambda p, qi: (0, 0)), row],
        out_specs=row,
        out_shape=jax.ShapeDtypeStruct((T, 512), BF16),
        scratch_shapes=[pltpu.VMEM((1, rows, 2 * LANES), BF16)] + _flash_scratch(rows),
        compiler_params=_params("arbitrary", "arbitrary"),
        name="moba",
    )(mq, kmean, mk, va, vb, e_blk, sz)


def _mem_kv_kernel(mem_ref, g_ref, w0_ref, w1_ref, kv0_ref, kv1_ref):
    mem_n = _rms(mem_ref[...], g_ref[...]).astype(BF16)
    kv0_ref[...] = _dot(mem_n, w0_ref[...]).astype(kv0_ref.dtype)
    kv1_ref[...] = _dot(mem_n, w1_ref[...]).astype(kv1_ref.dtype)


def _mem_kv(mem, g, w0, w1):
    n = mem.shape[0]
    kv_w = 2 * MEM_HEADS * HEAD_DIM
    return pl.pallas_call(
        _mem_kv_kernel,
        grid=(1,),
        in_specs=[
            pl.BlockSpec((n, D_MODEL), lambda i: (0, 0)),
            pl.BlockSpec((1, D_MODEL), lambda i: (0, 0)),
            pl.BlockSpec((D_MODEL, kv_w), lambda i: (0, 0)),
            pl.BlockSpec((D_MODEL, kv_w), lambda i: (0, 0)),
        ],
        out_specs=[pl.BlockSpec((n, kv_w), lambda i: (0, 0)), pl.BlockSpec((n, kv_w), lambda i: (0, 0))],
        out_shape=[jax.ShapeDtypeStruct((n, kv_w), BF16), jax.ShapeDtypeStruct((n, kv_w), BF16)],
        compiler_params=_params("arbitrary"),
        name="mem_kv",
    )(mem, g, w0, w1)


def _mem_attention(eq, kv, sez, lane):
    rows = eq.shape[0]
    mem_w = MEM_HEADS * HEAD_DIM
    cols = []
    for p in range(MEM_HEADS // 2):
        cs = slice(p * LANES, (p + 1) * LANES)
        colv = eq[:, cs]
        zero = jnp.zeros_like(colv)
        qp = jnp.concatenate([jnp.where(lane < HEAD_DIM, colv, zero), jnp.where(lane >= HEAD_DIM, colv, zero)], axis=0)
        s = _dot_nt(qp, kv[:, cs])
        pe = jnp.exp(s - jnp.max(s, axis=-1, keepdims=True))
        o = _dot(pe.astype(BF16), kv[:, mem_w + p * LANES:mem_w + (p + 1) * LANES]) / jnp.sum(pe, axis=-1, keepdims=True)
        cols.append(jnp.where(lane < HEAD_DIM, o[:rows], o[rows:]) * sez[:, cs].astype(F32))
    return jnp.concatenate(cols, axis=1).astype(BF16)


def _out0_kernel(ag_ref, bg_ref, eq_ref, sez_ref, kv_ref, wa_ref, wb_ref, wm_ref, x_ref, out_ref):
    rows = x_ref.shape[0]
    lane = lax.broadcasted_iota(jnp.int32, (rows, LANES), 1)
    mg = _mem_attention(eq_ref[...], kv_ref[...], sez_ref[...], lane)
    y = _dot(ag_ref[...], wa_ref[...]) + _dot(bg_ref[...], wb_ref[...]) + _dot(mg, wm_ref[...])
    out_ref[...] = x_ref[...] + y


def _out0(ag, bg, eq, sez, kv, wa, wb, wm, x):
    T = x.shape[0]
    rt = OUT_ROWS
    row = lambda w: pl.BlockSpec((rt, w), lambda i: (i, 0))
    full = lambda a: pl.BlockSpec(a.shape, lambda i: (0, 0))
    return pl.pallas_call(
        _out0_kernel,
        grid=(T // rt,),
        in_specs=[row(512), row(512), row(256), row(256), full(kv), full(wa), full(wb), full(wm), row(D_MODEL)],
        out_specs=row(D_MODEL),
        out_shape=jax.ShapeDtypeStruct((T, D_MODEL), F32),
        compiler_params=_params("arbitrary"),
        name="out_proj0",
    )(ag, bg, eq, sez, kv, wa, wb, wm, x)


def _out1_kernel(rg_ref, eq_ref, sez_ref, kv_ref, wr_ref, wm_ref, x_ref, g_ref, out_ref):
    rows = x_ref.shape[0]
    lane = lax.broadcasted_iota(jnp.int32, (rows, LANES), 1)
    mg = _mem_attention(eq_ref[...], kv_ref[...], sez_ref[...], lane)
    y = _dot(rg_ref[...], wr_ref[...]) + _dot(mg, wm_ref[...])
    out_ref[...] = _rms(x_ref[...] + y, g_ref[...])


def _out1(rg, eq, sez, kv, wr, wm, x, g):
    T = x.shape[0]
    rt = OUT_ROWS
    row = lambda w: pl.BlockSpec((rt, w), lambda i: (i, 0))
    full = lambda a: pl.BlockSpec(a.shape, lambda i: (0, 0))
    return pl.pallas_call(
        _out1_kernel,
        grid=(T // rt,),
        in_specs=[row(RET_HEADS * RET_V_DIM), row(256), row(256), full(kv), full(wr), full(wm), row(D_MODEL), full(g)],
        out_specs=row(D_MODEL),
        out_shape=jax.ShapeDtypeStruct((T, D_MODEL), F32),
        compiler_params=_params("arbitrary"),
        name="out_proj1_norm",
    )(rg, eq, sez, kv, wr, wm, x, g)


def _retention_kernel(q_ref, k_ref, v_ref, sz_ref, decay_ref, qd_ref, kd_ref, cd_ref, out_ref, state_ref):
    @pl.when(pl.program_id(0) == 0)
    def _():
        state_ref[...] = jnp.zeros(state_ref.shape, F32)

    for h in range(RET_HEADS):
        qs = slice(h * RET_QK_DIM, (h + 1) * RET_QK_DIM)
        vsl = slice(h * RET_V_DIM, (h + 1) * RET_V_DIM)
        qh = q_ref[:, qs]
        kh = k_ref[:, qs]
        vh = v_ref[:, vsl]
        st = state_ref[h]
        inner = _dot_nt(qh, kh.astype(BF16)) * decay_ref[h]
        o = _dot(inner.astype(BF16), vh) + _dot(qh, st.astype(BF16)) * qd_ref[h]
        state_ref[h] = st * cd_ref[h] + _dot_tn((kh * kd_ref[h]).astype(BF16), vh)
        mu = jnp.mean(o, axis=-1, keepdims=True)
        d = o - mu
        on = d * lax.rsqrt(jnp.mean(d * d, axis=-1, keepdims=True) + NORM_EPS)
        out_ref[:, vsl] = (on * sz_ref[:, vsl].astype(F32)).astype(out_ref.dtype)


def _retention(rq, rk, rv, sz):
    T = rq.shape[0]
    C = RET_CHUNK
    log_g = jnp.log(1.0 - 2.0 ** (-5.0 - jnp.arange(RET_HEADS, dtype=F32)))
    i = jnp.arange(C, dtype=F32)
    diff = i[:, None] - i[None, :]
    decay = jnp.where(diff >= 0, jnp.exp(jnp.maximum(diff, 0.0)[None] * log_g[:, None, None]), 0.0)
    q_decay = jnp.exp((i + 1.0)[None, :] * log_g[:, None])[:, :, None]
    k_decay = jnp.exp((C - 1.0 - i)[None, :] * log_g[:, None])[:, :, None]
    chunk_decay = jnp.exp(C * log_g)[:, None, None]
    qk_w = RET_HEADS * RET_QK_DIM
    v_w = RET_HEADS * RET_V_DIM
    return pl.pallas_call(
        _retention_kernel,
        grid=(T // C,),
        in_specs=[
            pl.BlockSpec((C, qk_w), lambda c: (c, 0)),
            pl.BlockSpec((C, qk_w), lambda c: (c, 0)),
            pl.BlockSpec((C, v_w), lambda c: (c, 0)),
            pl.BlockSpec((C, v_w), lambda c: (c, 0)),
            pl.BlockSpec((RET_HEADS, C, C), lambda c: (0, 0, 0)),
            pl.BlockSpec((RET_HEADS, C, 1), lambda c: (0, 0, 0)),
            pl.BlockSpec((RET_HEADS, C, 1), lambda c: (0, 0, 0)),
            pl.BlockSpec((RET_HEADS, 1, 1), lambda c: (0, 0, 0)),
        ],
        out_specs=pl.BlockSpec((C, v_w), lambda c: (c, 0)),
        out_shape=jax.ShapeDtypeStruct((T, v_w), BF16),
        scratch_shapes=[pltpu.VMEM((RET_HEADS, RET_QK_DIM, RET_V_DIM), F32)],
        compiler_params=_params("arbitrary"),
        name="retention",
    )(rq, rk, rv, sz, decay, q_decay, k_decay, chunk_decay)


def _head_cols(order):
    return np.concatenate([np.arange(h * HEAD_DIM, (h + 1) * HEAD_DIM) for h in order])


def _one_hot_blocks(T, block):
    b = (np.arange(T) // block) % LANES
    return jnp.asarray(np.eye(LANES, dtype=np.float32)[b], dtype=BF16)


def kernel(x, mem, positions, l0_norm_g, l0_w_in, l0_nsa_gate_b, l0_cmp_pe_k, l0_cmp_w1_k, l0_cmp_w2_k,
           l0_cmp_pe_v, l0_cmp_w1_v, l0_cmp_w2_v, l0_w_mem_kv, l0_w_out,
           l1_norm_g, l1_w_in, l1_w_mem_kv, l1_w_out, mem_norm_g, final_norm_g):
    B, T, _ = x.shape
    assert B == 1 and T % K_TILE == 0 and T >= NSA_WINDOW + Q_TILE and NSA_WINDOW % Q_TILE == 0
    assert T % RET_CHUNK == 0 and T % MOBA_Q_TILE == 0 and T % OUT_ROWS == 0
    n_sel = T // NSA_SEL_LEN
    n_selp = -(-n_sel // LANES) * LANES
    k_sel = min(NSA_SEL_TOPK, n_sel)
    assert k_sel >= 3
    n_blk = T // MOBA_BLOCK
    assert n_blk <= LANES
    moba_top = max(1, min(MOBA_TOPK, n_blk - 1))
    x2 = x[0]
    pos = positions[0].astype(F32)

    attn_inv = 1.0 / (ROPE_THETA ** (jnp.arange(0, HEAD_DIM, 2, dtype=F32) / HEAD_DIM))
    ret_inv = 1.0 / (ROPE_THETA ** jnp.linspace(0.0, 1.0, RET_QK_DIM // 2, dtype=F32))
    ang = pos[:, None] * attn_inv
    cos64 = jnp.tile(jnp.cos(ang), (1, 4))
    sin64 = jnp.tile(jnp.concatenate([-jnp.sin(ang), jnp.sin(ang)], axis=1), (1, 2))
    rang = pos[:, None] * ret_inv
    cos256, sin256 = jnp.cos(rang), jnp.sin(rang)

    offs = np.cumsum([0, 512, 128, 128, 128, 128, 128, 128, 24, 512, 512, 512, 512, 512, 256, 256])
    nsa_cols = _head_cols(NSA_HEAD_ORDER)
    n_gate = offs[8] - offs[7]
    col_idx = np.concatenate([
        offs[0] + nsa_cols,
        np.arange(offs[1], offs[7]),
        np.arange(offs[7], offs[8]), np.zeros(LANES - n_gate, np.int64),
        offs[8] + nsa_cols,
        np.arange(offs[9], offs[-1]),
    ])
    keep = np.ones(col_idx.shape, np.float32)
    keep[offs[7] + n_gate:offs[7] + LANES] = 0.0
    w0 = jnp.take(l0_w_in.astype(BF16), col_idx, axis=1) * jnp.asarray(keep, BF16)[None, :]
    gate_b = jnp.concatenate([l0_nsa_gate_b, jnp.zeros((LANES - 24,), F32)])[None, :]
    groups0 = [
        (512, "rot64", LOG2_SCALE), (128, "rot64", 1.0), (128, "plain", 1.0), (128, "rot64", 1.0),
        (128, "ones_pair", 1.0), (128, "rot64", 1.0), (128, "ones_pair", 1.0), (128, "gate", 1.0), (512, "silu", 1.0),
        (512, "rot64", LOG2_SCALE), (512, "rot64_mean", 1.0), (512, "ones_pair", 1.0), (512, "silu", 1.0),
        (256, "plain", 0.125), (256, "silu", 1.0),
    ]
    dts0 = [BF16] * 7 + [F32] + [BF16] * 7
    (nq, nkc, nvc, nks, nvs_a, nvs_b, nkw, nvw_a, nvw_b, gates, snz, mq, mk, kmean, mv_a, mv_b, smz, eq0, sez0) = _proj(
        x2, l0_norm_g[None, :], w0, cos64, sin64, gate_b, groups0, dts0, "proj0")

    kv0, kv1 = _mem_kv(mem[0], mem_norm_g[None, :], l0_w_mem_kv.astype(BF16), l1_w_mem_kv.astype(BF16))

    kc = _compress(nkc, l0_cmp_pe_k, l0_cmp_w1_k, l0_cmp_w2_k, "compress_k")
    vc = _compress(nvc, l0_cmp_pe_v, l0_cmp_w1_v, l0_cmp_w2_v, "compress_v")
    n_cmp_rows = T // NSA_CMP_STRIDE
    cmp_start = np.arange(n_cmp_rows)[:, None] * NSA_CMP_STRIDE
    sel_start = np.arange(n_selp)[None, :] * NSA_SEL_LEN
    overlap = ((cmp_start < sel_start + NSA_SEL_LEN) & (cmp_start + NSA_CMP_LEN > sel_start)
               & (np.arange(n_selp)[None, :] < n_sel))
    ov = jnp.asarray(overlap.T.astype(np.float32), dtype=BF16)
    ocw, sel_bias = _nsa_cmp(nq, kc, vc, ov, nkw, (nvw_a, nvw_b), gates, k_sel)
    ag = _nsa_sel(nq, sel_bias, nks, (nvs_a, nvs_b), _one_hot_blocks(T, NSA_SEL_LEN), gates, ocw, snz)

    n_blk_rows = -(-n_blk // 16) * 16
    kmean_p = jnp.concatenate([kmean[:, 0, :], jnp.zeros((n_blk_rows - n_blk, 512), F32)], axis=0).astype(BF16)
    bg = _moba(mq, kmean_p, mk, (mv_a, mv_b), _one_hot_blocks(T, MOBA_BLOCK), smz, moba_top)

    w_out0 = l0_w_out.astype(BF16)
    wa = w_out0[:512][nsa_cols]
    wb = w_out0[512:1024]
    wm0 = w_out0[1024:]
    x1 = _out0(ag, bg, eq0, sez0, kv0, wa, wb, wm0, x2)

    groups1 = [
        (1024, "rot256", 1.0), (1024, "rot256", RET_QK_DIM ** -0.5), (2048, "plain", 1.0), (2048, "silu", 1.0),
        (256, "plain", 0.125), (256, "silu", 1.0),
    ]
    dts1 = [BF16, F32, BF16, BF16, BF16, BF16]
    rq, rk, rv, srz, eq1, sez1 = _proj(x1, l1_norm_g[None, :], l1_w_in.astype(BF16), cos256, sin256,
                                       jnp.zeros((1, LANES), F32), groups1, dts1, "proj1")
    rg = _retention(rq, rk, rv, srz)
    v_w = RET_HEADS * RET_V_DIM
    out = _out1(rg, eq1, sez1, kv1, l1_w_out[:v_w].astype(BF16), l1_w_out[v_w:].astype(BF16), x1,
                final_norm_g[None, :])
    return out[None]
```

```python
import functools

import numpy as np
import jax
import jax.numpy as jnp
from jax import lax
from jax.experimental import pallas as pl
from jax.experimental.pallas import tpu as pltpu

F32 = jnp.float32
BF16 = jnp.bfloat16

D_MODEL = 1024
HEAD_DIM = 64
ROPE_THETA = 10000.0
NORM_EPS = 1e-6
NEG_INF = -1e30
MASK_BIAS = NEG_INF
LOG2_SCALE = HEAD_DIM ** -0.5 * float(np.log2(np.e))

NSA_HEADS = 8
NSA_GROUP_HEADS = 4
NSA_CMP_LEN = 32
NSA_CMP_STRIDE = 16
NSA_SEL_LEN = 64
NSA_SEL_TOPK = 16
NSA_WINDOW = 512
MOBA_HEADS = 8
MOBA_BLOCK = 256
MOBA_TOPK = 3
MEM_LEN = 256
MEM_HEADS = 4
RET_HEADS = 4
RET_QK_DIM = 256
RET_V_DIM = 512
RET_CHUNK = 256

LANES = 128
PROJ_ROWS = 512
PROJ_COLS = 512
OUT_ROWS = 512
Q_TILE = 128
K_TILE = 512
VMEM_LIMIT = 56 * 1024 * 1024

NSA_HEAD_ORDER = (0, 4, 1, 5, 2, 6, 3, 7)


def _dot(a, b):
    return jnp.dot(a, b, preferred_element_type=F32)


def _dot_nt(a, b):
    return lax.dot_general(a, b, (((1,), (1,)), ((), ())), preferred_element_type=F32)


def _dot_tn(a, b):
    return lax.dot_general(a, b, (((0,), (0,)), ((), ())), preferred_element_type=F32)


def _silu(y):
    return y * (1.0 / (1.0 + jnp.exp(-y)))


def _rms(x, g):
    return x * lax.rsqrt(jnp.mean(x * x, axis=-1, keepdims=True) + NORM_EPS) * g


def _params(*sem):
    return pltpu.CompilerParams(dimension_semantics=sem, vmem_limit_bytes=VMEM_LIMIT)


def _proj_kernel(x_ref, g_ref, w_ref, c_ref, s_ref, b_ref, *out_refs, groups):
    h = _rms(x_ref[...], g_ref[...]).astype(BF16)
    rows = x_ref.shape[0]
    lane = lax.broadcasted_iota(jnp.int32, (rows, LANES), 1)
    low_half = (lane & 32) == 0
    pieces = []
    col = 0
    oi = 0
    for width, kind, scale in groups:
        out_ref = out_refs[oi]
        oi += 1
        mean_ref = None
        if kind in ("rot64_mean", "ones_pair"):
            mean_ref = out_refs[oi]
            oi += 1
        pw = RET_QK_DIM if kind == "rot256" else LANES
        for c0 in range(0, width, pw):
            pieces.append((col + c0, pw, kind, scale, out_ref, c0, mean_ref))
        col += width

    def epilogue(z, kind, scale):
        if kind in ("rot64", "rot64_mean"):
            partner = jnp.where(low_half, pltpu.roll(z, LANES - 32, 1), pltpu.roll(z, 32, 1))
            z = z * c_ref[...] + partner * s_ref[...]
        elif kind == "rot256":
            x1, x2 = z[:, :LANES], z[:, LANES:]
            z = jnp.concatenate([x1 * c_ref[...] - x2 * s_ref[...], x1 * s_ref[...] + x2 * c_ref[...]], axis=1)
        elif kind == "silu":
            z = _silu(z)
        elif kind == "gate":
            z = 1.0 / (1.0 + jnp.exp(-(z + b_ref[...])))
        return z if scale == 1.0 else z * scale

    i = 0
    while i < len(pieces):
        j, run_w = i, 0
        while j < len(pieces) and run_w + pieces[j][1] <= PROJ_COLS:
            run_w += pieces[j][1]
            j += 1
        c_start = pieces[i][0]
        y = _dot(h, w_ref[:, c_start:c_start + run_w])
        for pc, pw, kind, scale, out_ref, oc, mean_ref in pieces[i:j]:
            z = epilogue(y[:, pc - c_start:pc - c_start + pw], kind, scale)
            if kind == "ones_pair":
                out_ref[:, oc:oc + pw] = jnp.where(lane < HEAD_DIM, z, 1.0).astype(out_ref.dtype)
                mean_ref[:, oc:oc + pw] = jnp.where(lane >= HEAD_DIM, z, 1.0).astype(mean_ref.dtype)
                continue
            if mean_ref is not None:
                for r in range(rows // MOBA_BLOCK):
                    mean_ref[r, :, oc:oc + pw] = jnp.mean(z[r * MOBA_BLOCK:(r + 1) * MOBA_BLOCK], axis=0, keepdims=True)
            out_ref[:, oc:oc + pw] = z.astype(out_ref.dtype)
        i = j


def _proj(x, g, w, cos_t, sin_t, bias, groups, out_dtypes, name):
    T = x.shape[0]
    n_tiles = T // PROJ_ROWS
    out_shapes = []
    out_specs = []
    for (width, kind, _), dt in zip(groups, out_dtypes):
        out_shapes.append(jax.ShapeDtypeStruct((T, width), dt))
        out_specs.append(pl.BlockSpec((PROJ_ROWS, width), lambda i: (i, 0)))
        if kind == "rot64_mean":
            out_shapes.append(jax.ShapeDtypeStruct((T // MOBA_BLOCK, 1, width), F32))
            out_specs.append(pl.BlockSpec((PROJ_ROWS // MOBA_BLOCK, 1, width), lambda i: (i, 0, 0)))
        if kind == "ones_pair":
            out_shapes.append(jax.ShapeDtypeStruct((T, width), dt))
            out_specs.append(pl.BlockSpec((PROJ_ROWS, width), lambda i: (i, 0)))
    return pl.pallas_call(
        functools.partial(_proj_kernel, groups=tuple(groups)),
        grid=(n_tiles,),
        in_specs=[
            pl.BlockSpec((PROJ_ROWS, D_MODEL), lambda i: (i, 0)),
            pl.BlockSpec((1, D_MODEL), lambda i: (0, 0)),
            pl.BlockSpec(w.shape, lambda i: (0, 0), pipeline_mode=pl.Buffered(1)),
            pl.BlockSpec((PROJ_ROWS, LANES), lambda i: (i, 0)),
            pl.BlockSpec((PROJ_ROWS, LANES), lambda i: (i, 0)),
            pl.BlockSpec((1, LANES), lambda i: (0, 0)),
        ],
        out_specs=out_specs,
        out_shape=out_shapes,
        compiler_params=_params("arbitrary"),
        name=name,
    )(x, g, w, cos_t, sin_t, bias)


def _compress_kernel(a_ref, an_ref, pea_ref, peb_ref, w1a_ref, w1b_ref, w2_ref, out_ref):
    xa = (a_ref[...].astype(F32) + pea_ref[...]).astype(BF16)
    xb = (an_ref[...].astype(F32) + peb_ref[...]).astype(BF16)
    hid = _silu(_dot(xa, w1a_ref[...]) + _dot(xb, w1b_ref[...]))
    out_ref[...] = _dot(hid.astype(BF16), w2_ref[...]).astype(out_ref.dtype)


def _compress(a, pe, w1, w2, name):
    T = a.shape[0]
    n_rows = T // NSA_CMP_STRIDE
    row_w = NSA_CMP_STRIDE * LANES
    a2 = a.reshape(n_rows, row_w)
    a2n = jnp.concatenate([a2[1:], jnp.zeros((1, row_w), a2.dtype)], axis=0)
    eye2 = jnp.eye(2, dtype=F32)

    def expand(w1_half):
        return jnp.einsum("lde,gh->lgdhe", w1_half, eye2).reshape(row_w, LANES).astype(BF16)

    def pe_row(pe_half):
        return jnp.broadcast_to(pe_half[:, None, :], (NSA_CMP_STRIDE, 2, HEAD_DIM)).reshape(1, row_w)

    w2bd = jnp.einsum("ef,gh->gehf", w2, eye2).reshape(LANES, LANES).astype(BF16)
    tile = min(256, n_rows)
    return pl.pallas_call(
        _compress_kernel,
        grid=(n_rows // tile,),
        in_specs=[
            pl.BlockSpec((tile, row_w), lambda i: (i, 0)),
            pl.BlockSpec((tile, row_w), lambda i: (i, 0)),
            pl.BlockSpec((1, row_w), lambda i: (0, 0)),
            pl.BlockSpec((1, row_w), lambda i: (0, 0)),
            pl.BlockSpec((row_w, LANES), lambda i: (0, 0)),
            pl.BlockSpec((row_w, LANES), lambda i: (0, 0)),
            pl.BlockSpec((LANES, LANES), lambda i: (0, 0)),
        ],
        out_specs=pl.BlockSpec((tile, LANES), lambda i: (i, 0)),
        out_shape=jax.ShapeDtypeStruct((n_rows, LANES), BF16),
        compiler_params=_params("arbitrary"),
        name=name,
    )(a2, a2n, pe_row(pe[:NSA_CMP_STRIDE]), pe_row(pe[NSA_CMP_STRIDE:]),
      expand(w1[:NSA_CMP_STRIDE]), expand(w1[NSA_CMP_STRIDE:]), w2bd)


def _nsa_head_query(q, h, lane):
    colv = q[:, (h % 4) * LANES:(h % 4 + 1) * LANES]
    keep = (lane < HEAD_DIM) if h < NSA_GROUP_HEADS else (lane >= HEAD_DIM)
    return jnp.where(keep, colv, jnp.zeros_like(colv))


def _nsa_stacked_queries(q, lane):
    return jnp.concatenate([_nsa_head_query(q, h, lane) for h in range(NSA_HEADS)], axis=0)


def _nsa_interleave_heads(o, gates, gate0, lane, out_ref, accumulate=False):
    tq = out_ref.shape[0]
    outs = [o[h * tq:(h + 1) * tq] * gates[:, gate0 + h:gate0 + h + 1] for h in range(NSA_HEADS)]
    for j in range(4):
        cs = slice(j * LANES, (j + 1) * LANES)
        a = jnp.where(lane < HEAD_DIM, outs[j], outs[4 + j])
        out_ref[:, cs] = out_ref[:, cs] + a if accumulate else a


def _values_with_ones(v):
    ones = jnp.ones((v.shape[0], HEAD_DIM), v.dtype)
    return jnp.concatenate([v[:, :HEAD_DIM], ones], axis=1), jnp.concatenate([ones, v[:, HEAD_DIM:]], axis=1)


def _top_k_flags(x, idx, k):
    sel = jnp.zeros(x.shape, jnp.bool_)
    for _ in range(k):
        mx = jnp.max(x, axis=0, keepdims=True)
        first = jnp.min(jnp.where(x == mx, idx, x.shape[0]), axis=0, keepdims=True)
        hit = idx == first
        sel = sel | hit
        x = jnp.where(hit, -jnp.inf, x)
    return sel


CMP_KEY_BUCKET = 128


def _nsa_cmp_kernel(q_ref, kc_ref, vca_ref, vcb_ref, ovt_ref, kw_ref, wa_ref, wb_ref, gate_ref, oc_ref, bias_ref,
                    *, k_sel):
    tq = q_ref.shape[0]
    n_cmp = kc_ref.shape[0]
    n_selp = ovt_ref.shape[0]
    rows = NSA_HEADS * tq
    half = rows // 2
    t0 = pl.program_id(0) * tq
    lane = lax.broadcasted_iota(jnp.int32, (tq, LANES), 1)
    t_rows = t0 + (lax.broadcasted_iota(jnp.int32, (rows, 1), 0) & (tq - 1))

    span = NSA_WINDOW + tq
    start = pl.multiple_of(jnp.maximum(t0 - NSA_WINDOW, 0), tq)

    def window(mask_scores):
        s = mask_scores(_dot_nt(_nsa_stacked_queries(q_ref[...], lane), kw_ref[pl.ds(start, span), :]))
        pb = jnp.exp2(s - jnp.max(s, axis=-1, keepdims=True)).astype(BF16)
        acc = jnp.concatenate([_dot(pb[:half], wa_ref[pl.ds(start, span), :]),
                               _dot(pb[half:], wb_ref[pl.ds(start, span), :])], axis=0)
        _nsa_interleave_heads(acc / pltpu.roll(acc, HEAD_DIM, 1), gate_ref[...], 2 * NSA_HEADS, lane, oc_ref)

    @pl.when(t0 >= NSA_WINDOW)
    def _():
        q_local = lax.broadcasted_iota(jnp.int32, (rows, 1), 0) & (tq - 1)
        k_local = lax.broadcasted_iota(jnp.int32, (1, tq), 1)
        window(lambda s: jnp.concatenate([jnp.where(k_local > q_local, s[:, :tq], NEG_INF), s[:, tq:NSA_WINDOW],
                                          jnp.where(k_local <= q_local, s[:, NSA_WINDOW:], NEG_INF)], axis=1))

    @pl.when(t0 < NSA_WINDOW)
    def _():
        wpos = lax.broadcasted_iota(jnp.int32, (1, span), 1)
        window(lambda s: jnp.where(wpos <= t_rows, s, NEG_INF))

    def attend(nk):
        cmp_end = lax.broadcasted_iota(jnp.int32, (1, nk), 1) * NSA_CMP_STRIDE + (NSA_CMP_LEN - 1)
        s = _dot_nt(_nsa_stacked_queries(q_ref[...], lane), kc_ref[0:nk, :])
        s = jnp.where(cmp_end <= t_rows, s, NEG_INF)
        m = jnp.max(s, axis=-1, keepdims=True)
        p = jnp.exp2(s - m)
        pb = p.astype(BF16)
        acc_a = _dot(pb[:half], vca_ref[0:nk, :])
        acc_b = _dot(pb[half:], vcb_ref[0:nk, :])
        l = jnp.concatenate([acc_a[:, HEAD_DIM:HEAD_DIM + 1], acc_b[:, 0:1]], axis=0)
        rl = jnp.where(m > 0.5 * NEG_INF, 1.0 / jnp.maximum(l, 1e-30), 0.0)
        _nsa_interleave_heads(jnp.concatenate([acc_a, acc_b], axis=0) * rl, gate_ref[...], 0, lane, oc_ref,
                              accumulate=True)
        pr = p * rl
        nb = min(n_selp, nk * NSA_CMP_STRIDE // NSA_SEL_LEN)
        blk = lax.broadcasted_iota(jnp.int32, (nb, tq), 0)
        cur = (t0 + lax.broadcasted_iota(jnp.int32, (1, tq), 1)) >> 6
        valid = blk <= cur
        forced = (blk == 0) | (blk == cur) | (blk == cur - 1)
        for g in range(2):
            r0 = g * half
            pg = pr[r0:r0 + tq] + pr[r0 + tq:r0 + 2 * tq] + pr[r0 + 2 * tq:r0 + 3 * tq] + pr[r0 + 3 * tq:r0 + 4 * tq]
            imp_t = _dot_nt(ovt_ref[0:nb, 0:nk], pg.astype(BF16))
            x = jnp.where(valid & jnp.logical_not(forced), imp_t, NEG_INF)
            sel = _top_k_flags(x, blk, k_sel - 3) | forced
            bias_t = jnp.where(sel & valid, 0.0, MASK_BIAS)
            if nb < n_selp:
                bias_t = jnp.concatenate([bias_t, jnp.full((n_selp - nb, tq), MASK_BIAS, F32)], axis=0)
            bias_ref[g] = bias_t.T.astype(BF16)

    needed = (t0 + tq - NSA_CMP_LEN) // NSA_CMP_STRIDE + 1
    buckets = sorted({min(b, n_cmp) for b in range(CMP_KEY_BUCKET, n_cmp + CMP_KEY_BUCKET, CMP_KEY_BUCKET)})
    for i, nk in enumerate(buckets):
        lo = buckets[i - 1] if i else -(1 << 30)
        cond = (needed > lo) if i == len(buckets) - 1 else ((needed > lo) & (needed <= nk))
        pl.when(cond)(functools.partial(attend, nk))


def _nsa_cmp(nq, kc, vc, ov, kw, vw, gates, k_sel):
    T = nq.shape[0]
    n_cmp = kc.shape[0]
    n_selp = ov.shape[0]
    vca, vcb = _values_with_ones(vc)
    wa, wb = vw if isinstance(vw, tuple) else _values_with_ones(vw)
    cmp_full = pl.BlockSpec((n_cmp, LANES), lambda i: (0, 0))
    full = pl.BlockSpec((T, LANES), lambda i: (0, 0))
    return pl.pallas_call(
        functools.partial(_nsa_cmp_kernel, k_sel=k_sel),
        grid=(T // Q_TILE,),
        in_specs=[
            pl.BlockSpec((Q_TILE, 512), lambda i: (i, 0)),
            cmp_full, cmp_full, cmp_full,
            pl.BlockSpec((n_selp, n_cmp), lambda i: (0, 0)),
            full, full, full,
            pl.BlockSpec((Q_TILE, LANES), lambda i: (i, 0)),
        ],
        out_specs=[
            pl.BlockSpec((Q_TILE, 512), lambda i: (i, 0)),
            pl.BlockSpec((2, Q_TILE, n_selp), lambda i: (0, i, 0)),
        ],
        out_shape=[
            jax.ShapeDtypeStruct((T, 512), F32),
            jax.ShapeDtypeStruct((2, T, n_selp), BF16),
        ],
        compiler_params=_params("arbitrary"),
        name="nsa_cmp_win_select",
    )(nq, kc, vca, vcb, ov, kw, wa, wb, gates)


def _flash_scratch(rows):
    return [
        pltpu.VMEM((rows, K_TILE), F32), pltpu.VMEM((rows, K_TILE), F32),
        pltpu.VMEM((rows, K_TILE), BF16), pltpu.VMEM((rows, K_TILE), BF16),
        pltpu.VMEM((rows, 1), F32), pltpu.VMEM((rows, 1), F32),
        pltpu.VMEM((rows, 1), F32),
        pltpu.VMEM((rows, LANES), F32),
    ]


def _flash_body(lhs_ref, k_ref, e_ref, va_ref, vb_ref, t_of_rows, n_full, tokens_per_half, scratch):
    s_a, s_b, p_a, p_b, al_a, al_b, m_ref, acc_ref = scratch
    tk = K_TILE
    rows = acc_ref.shape[0]
    half_rows = rows // 2
    last_mem_tile = k_ref.shape[0] // tk - 1

    def tile_start(j):
        return pl.multiple_of(jnp.clip(j, 0, last_mem_tile) * tk, tk)

    def scores(j, s_out):
        k0 = tile_start(j)
        rhs = jnp.concatenate([k_ref[pl.ds(k0, tk), :], e_ref[pl.ds(k0, tk), :]], axis=1)
        s_out[...] = _dot_nt(lhs_ref[k0 // tokens_per_half], rhs)

    def softmax(j, s_in, p_out, al_out, causal):
        s = s_in[...]
        if causal:
            kpos = j * tk + lax.broadcasted_iota(jnp.int32, (1, tk), 1)
            s = jnp.where(kpos <= t_of_rows(0, rows), s, NEG_INF)
        m_old = m_ref[...]
        m_new = jnp.maximum(m_old, jnp.max(s, axis=-1, keepdims=True))
        m_ref[...] = m_new
        al_out[...] = jnp.exp2(m_old - m_new)
        p_out[...] = jnp.exp2(s - m_new).astype(BF16)

    def values(j, p_in, al_in):
        k0 = tile_start(j)
        for rs, v_ref in ((slice(0, half_rows), va_ref), (slice(half_rows, rows), vb_ref)):
            acc_ref[rs, :] = al_in[rs, :] * acc_ref[rs, :] + _dot(p_in[rs, :], v_ref[pl.ds(k0, tk), :])

    m_ref[...] = jnp.full(m_ref.shape, NEG_INF, F32)
    acc_ref[...] = jnp.zeros(acc_ref.shape, F32)
    p_b[...] = jnp.zeros(p_b.shape, BF16)
    al_b[...] = jnp.ones(al_b.shape, F32)
    scores(0, s_a)

    def two_full_tiles(i, carry):
        j = 2 * i
        values(j - 1, p_b, al_b)
        scores(j + 1, s_b)
        softmax(j, s_a, p_a, al_a, False)
        values(j, p_a, al_a)
        scores(j + 2, s_a)
        softmax(j + 1, s_b, p_b, al_b, False)
        return carry

    n_pairs = n_full // 2
    lax.fori_loop(0, n_pairs, two_full_tiles, 0)
    j = 2 * n_pairs
    odd = (n_full & 1) == 1

    @pl.when(odd)
    def _():
        values(j - 1, p_b, al_b)
        scores(j + 1, s_b)
        softmax(j, s_a, p_a, al_a, True)
        values(j, p_a, al_a)
        softmax(j + 1, s_b, p_b, al_b, True)
        values(j + 1, p_b, al_b)

    @pl.when(jnp.logical_not(odd))
    def _():
        values(j - 1, p_b, al_b)
        softmax(j, s_a, p_a, al_a, True)
        values(j, p_a, al_a)

    acc = acc_ref[...]
    return acc / pltpu.roll(acc, HEAD_DIM, 1)


def _nsa_sel_kernel(q_ref, bias_ref, ks_ref, va_ref, vb_ref, e_ref, gate_ref, ocw_ref, sz_ref, out_ref,
                    lhs_ref, *scratch):
    tq = q_ref.shape[0]
    t0 = pl.program_id(0) * tq
    lane = lax.broadcasted_iota(jnp.int32, (tq, LANES), 1)
    q = q_ref[...]
    for half in range(lhs_ref.shape[0]):
        for h in range(NSA_HEADS):
            b = bias_ref[h // NSA_GROUP_HEADS, :, half * LANES:(half + 1) * LANES]
            lhs_ref[half, h * tq:(h + 1) * tq, :] = jnp.concatenate([_nsa_head_query(q, h, lane), b], axis=1)

    def t_of_rows(r0, n):
        return t0 + ((r0 + lax.broadcasted_iota(jnp.int32, (n, 1), 0)) & (tq - 1))

    o = _flash_body(lhs_ref, ks_ref, e_ref, va_ref, vb_ref, t_of_rows, t0 // K_TILE, NSA_SEL_LEN * LANES, scratch)
    gates = gate_ref[...]
    outs = [o[h * tq:(h + 1) * tq] * gates[:, NSA_HEADS + h:NSA_HEADS + h + 1] for h in range(NSA_HEADS)]
    for j in range(4):
        cs = slice(j * LANES, (j + 1) * LANES)
        a = jnp.where(lane < HEAD_DIM, outs[j], outs[4 + j]) + ocw_ref[:, cs]
        out_ref[:, cs] = (a * sz_ref[:, cs].astype(F32)).astype(out_ref.dtype)


def _nsa_sel(nq, bias, ks, vs, e_sel, gates, ocw, sz):
    T = nq.shape[0]
    n_selp = bias.shape[2]
    rows = NSA_HEADS * Q_TILE
    row = lambda w: pl.BlockSpec((Q_TILE, w), lambda qi: (qi, 0))
    full = pl.BlockSpec((T, LANES), lambda qi: (0, 0))
    va, vb = vs if isinstance(vs, tuple) else _values_with_ones(vs)
    return pl.pallas_call(
        _nsa_sel_kernel,
        grid=(T // Q_TILE,),
        in_specs=[row(512), pl.BlockSpec((2, Q_TILE, n_selp), lambda qi: (0, qi, 0)), full, full, full, full,
                  row(LANES), row(512), row(512)],
        out_specs=row(512),
        out_shape=jax.ShapeDtypeStruct((T, 512), BF16),
        scratch_shapes=[pltpu.VMEM((n_selp // LANES, rows, 2 * LANES), BF16)] + _flash_scratch(rows),
        compiler_params=_params("arbitrary"),
        name="nsa_selected",
    )(nq, bias, ks, va, vb, e_sel, gates, ocw, sz)


MOBA_Q_TILE = 512


def _moba_kernel(q_ref, kmean_ref, k_ref, va_ref, vb_ref, e_ref, sz_ref, out_ref, lhs_ref, *scratch, top):
    tq = q_ref.shape[0]
    t0 = pl.program_id(1) * tq
    lane = lax.broadcasted_iota(jnp.int32, (tq, LANES), 1)
    colv = q_ref[...]
    zero = jnp.zeros_like(colv)
    qp = jnp.concatenate([jnp.where(lane < HEAD_DIM, colv, zero), jnp.where(lane >= HEAD_DIM, colv, zero)], axis=0)

    def t_of_rows(r0, n):
        return t0 + ((r0 + lax.broadcasted_iota(jnp.int32, (n, 1), 0)) & (tq - 1))

    n_blk_rows = kmean_ref.shape[0]
    cur = (t0 + (lax.broadcasted_iota(jnp.int32, (1, 2 * tq), 1) & (tq - 1))) >> 8
    blk = lax.broadcasted_iota(jnp.int32, (n_blk_rows, 2 * tq), 0)
    past = blk < cur
    gate = jnp.where(past, _dot_nt(kmean_ref[...], qp), NEG_INF)
    sel = _top_k_flags(gate, blk, top)
    bias_t = jnp.where((sel & past) | (blk == cur), 0.0, MASK_BIAS)
    if n_blk_rows < LANES:
        bias_t = jnp.concatenate([bias_t, jnp.full((LANES - n_blk_rows, 2 * tq), MASK_BIAS, F32)], axis=0)
    lhs_ref[0] = jnp.concatenate([qp, bias_t.T.astype(BF16)], axis=1)
    o = _flash_body(lhs_ref, k_ref, e_ref, va_ref, vb_ref, t_of_rows, t0 // K_TILE, MOBA_BLOCK * LANES, scratch)
    a = jnp.where(lane < HEAD_DIM, o[:tq], o[tq:])
    out_ref[...] = (a * sz_ref[...].astype(F32)).astype(out_ref.dtype)


def _moba(mq, kmean, mk, mv, e_blk, sz, top):
    T = mq.shape[0]
    tq = MOBA_Q_TILE
    assert MOBA_BLOCK == 256 and T * 1 <= MOBA_BLOCK * LANES
    rows = 2 * tq
    row = pl.BlockSpec((tq, LANES), lambda p, qi: (qi, p))
    col = pl.BlockSpec((T, LANES), lambda p, qi: (0, p))
    if isinstance(mv, tuple):
        va, vb = mv
    else:
        v4 = mv.reshape(T, MOBA_HEADS // 2, 2, HEAD_DIM)
        ones = jnp.ones((T, MOBA_HEADS // 2, HEAD_DIM), mv.dtype)
        va = jnp.stack([v4[:, :, 0], ones], axis=2).reshape(T, 512)
        vb = jnp.stack([ones, v4[:, :, 1]], axis=2).reshape(T, 512)
    return pl.pallas_call(
        functools.partial(_moba_kernel, top=top),
        grid=(MOBA_HEADS // 2, T // tq),
        in_specs=[row, pl.BlockSpec((kmean.shape[0], LANES), lambda p, qi: (0, p)), col, col, col,
                  pl.BlockSpec((T, LANES), lambda p, qi: (0, 0)), row],
        out_specs=row,
        out_shape=jax.ShapeDtypeStruct((T, 512), BF16),
        scratch_shapes=[pltpu.VMEM((1, rows, 2 * LANES), BF16)] + _flash_scratch(rows),
        compiler_params=_params("arbitrary", "arbitrary"),
        name="moba",
    )(mq, kmean, mk, va, vb, e_blk, sz)


def _mem_kv_kernel(mem_ref, g_ref, w0_ref, w1_ref, kv0_ref, kv1_ref):
    mem_n = _rms(mem_ref[...], g_ref[...]).astype(BF16)
    kv0_ref[...] = _dot(mem_n, w0_ref[...]).astype(kv0_ref.dtype)
    kv1_ref[...] = _dot(mem_n, w1_ref[...]).astype(kv1_ref.dtype)


def _mem_kv(mem, g, w0, w1):
    n = mem.shape[0]
    kv_w = 2 * MEM_HEADS * HEAD_DIM
    return pl.pallas_call(
        _mem_kv_kernel,
        grid=(1,),
        in_specs=[
            pl.BlockSpec((n, D_MODEL), lambda i: (0, 0)),
            pl.BlockSpec((1, D_MODEL), lambda i: (0, 0)),
            pl.BlockSpec((D_MODEL, kv_w), lambda i: (0, 0)),
            pl.BlockSpec((D_MODEL, kv_w), lambda i: (0, 0)),
        ],
        out_specs=[pl.BlockSpec((n, kv_w), lambda i: (0, 0)), pl.BlockSpec((n, kv_w), lambda i: (0, 0))],
        out_shape=[jax.ShapeDtypeStruct((n, kv_w), BF16), jax.ShapeDtypeStruct((n, kv_w), BF16)],
        compiler_params=_params("arbitrary"),
        name="mem_kv",
    )(mem, g, w0, w1)


def _mem_attention(eq, kv, sez, lane):
    rows = eq.shape[0]
    mem_w = MEM_HEADS * HEAD_DIM
    cols = []
    for p in range(MEM_HEADS // 2):
        cs = slice(p * LANES, (p + 1) * LANES)
        colv = eq[:, cs]
        zero = jnp.zeros_like(colv)
        qp = jnp.concatenate([jnp.where(lane < HEAD_DIM, colv, zero), jnp.where(lane >= HEAD_DIM, colv, zero)], axis=0)
        s = _dot_nt(qp, kv[:, cs])
        pe = jnp.exp(s - jnp.max(s, axis=-1, keepdims=True))
        o = _dot(pe.astype(BF16), kv[:, mem_w + p * LANES:mem_w + (p + 1) * LANES]) / jnp.sum(pe, axis=-1, keepdims=True)
        cols.append(jnp.where(lane < HEAD_DIM, o[:rows], o[rows:]) * sez[:, cs].astype(F32))
    return jnp.concatenate(cols, axis=1).astype(BF16)


def _out0_kernel(ag_ref, bg_ref, eq_ref, sez_ref, kv_ref, wa_ref, wb_ref, wm_ref, x_ref, out_ref):
    rows = x_ref.shape[0]
    lane = lax.broadcasted_iota(jnp.int32, (rows, LANES), 1)
    mg = _mem_attention(eq_ref[...], kv_ref[...], sez_ref[...], lane)
    y = _dot(ag_ref[...], wa_ref[...]) + _dot(bg_ref[...], wb_ref[...]) + _dot(mg, wm_ref[...])
    out_ref[...] = x_ref[...] + y


def _out0(ag, bg, eq, sez, kv, wa, wb, wm, x):
    T = x.shape[0]
    rt = OUT_ROWS
    row = lambda w: pl.BlockSpec((rt, w), lambda i: (i, 0))
    full = lambda a: pl.BlockSpec(a.shape, lambda i: (0, 0))
    return pl.pallas_call(
        _out0_kernel,
        grid=(T // rt,),
        in_specs=[row(512), row(512), row(256), row(256), full(kv), full(wa), full(wb), full(wm), row(D_MODEL)],
        out_specs=row(D_MODEL),
        out_shape=jax.ShapeDtypeStruct((T, D_MODEL), F32),
        compiler_params=_params("arbitrary"),
        name="out_proj0",
    )(ag, bg, eq, sez, kv, wa, wb, wm, x)


def _out1_kernel(rg_ref, eq_ref, sez_ref, kv_ref, wr_ref, wm_ref, x_ref, g_ref, out_ref):
    rows = x_ref.shape[0]
    lane = lax.broadcasted_iota(jnp.int32, (rows, LANES), 1)
    mg = _mem_attention(eq_ref[...], kv_ref[...], sez_ref[...], lane)
    y = _dot(rg_ref[...], wr_ref[...]) + _dot(mg, wm_ref[...])
    out_ref[...] = _rms(x_ref[...] + y, g_ref[...])


def _out1(rg, eq, sez, kv, wr, wm, x, g):
    T = x.shape[0]
    rt = OUT_ROWS
    row = lambda w: pl.BlockSpec((rt, w), lambda i: (i, 0))
    full = lambda a: pl.BlockSpec(a.shape, lambda i: (0, 0))
    return pl.pallas_call(
        _out1_kernel,
        grid=(T // rt,),
        in_specs=[row(RET_HEADS * RET_V_DIM), row(256), row(256), full(kv), full(wr), full(wm), row(D_MODEL), full(g)],
        out_specs=row(D_MODEL),
        out_shape=jax.ShapeDtypeStruct((T, D_MODEL), F32),
        compiler_params=_params("arbitrary"),
        name="out_proj1_norm",
    )(rg, eq, sez, kv, wr, wm, x, g)


def _retention_kernel(q_ref, k_ref, v_ref, sz_ref, decay_ref, qd_ref, kd_ref, cd_ref, out_ref, state_ref):
    @pl.when(pl.program_id(0) == 0)
    def _():
        state_ref[...] = jnp.zeros(state_ref.shape, F32)

    for h in range(RET_HEADS):
        qs = slice(h * RET_QK_DIM, (h + 1) * RET_QK_DIM)
        vsl = slice(h * RET_V_DIM, (h + 1) * RET_V_DIM)
        qh = q_ref[:, qs]
        kh = k_ref[:, qs]
        vh = v_ref[:, vsl]
        st = state_ref[h]
        inner = _dot_nt(qh, kh.astype(BF16)) * decay_ref[h]
        o = _dot(inner.astype(BF16), vh) + _dot(qh, st.astype(BF16)) * qd_ref[h]
        state_ref[h] = st * cd_ref[h] + _dot_tn((kh * kd_ref[h]).astype(BF16), vh)
        mu = jnp.mean(o, axis=-1, keepdims=True)
        d = o - mu
        on = d * lax.rsqrt(jnp.mean(d * d, axis=-1, keepdims=True) + NORM_EPS)
        out_ref[:, vsl] = (on * sz_ref[:, vsl].astype(F32)).astype(out_ref.dtype)


def _retention(rq, rk, rv, sz):
    T = rq.shape[0]
    C = RET_CHUNK
    log_g = jnp.log(1.0 - 2.0 ** (-5.0 - jnp.arange(RET_HEADS, dtype=F32)))
    i = jnp.arange(C, dtype=F32)
    diff = i[:, None] - i[None, :]
    decay = jnp.where(diff >= 0, jnp.exp(jnp.maximum(diff, 0.0)[None] * log_g[:, None, None]), 0.0)
    q_decay = jnp.exp((i + 1.0)[None, :] * log_g[:, None])[:, :, None]
    k_decay = jnp.exp((C - 1.0 - i)[None, :] * log_g[:, None])[:, :, None]
    chunk_decay = jnp.exp(C * log_g)[:, None, None]
    qk_w = RET_HEADS * RET_QK_DIM
    v_w = RET_HEADS * RET_V_DIM
    return pl.pallas_call(
        _retention_kernel,
        grid=(T // C,),
        in_specs=[
            pl.BlockSpec((C, qk_w), lambda c: (c, 0)),
            pl.BlockSpec((C, qk_w), lambda c: (c, 0)),
            pl.BlockSpec((C, v_w), lambda c: (c, 0)),
            pl.BlockSpec((C, v_w), lambda c: (c, 0)),
            pl.BlockSpec((RET_HEADS, C, C), lambda c: (0, 0, 0)),
            pl.BlockSpec((RET_HEADS, C, 1), lambda c: (0, 0, 0)),
            pl.BlockSpec((RET_HEADS, C, 1), lambda c: (0, 0, 0)),
            pl.BlockSpec((RET_HEADS, 1, 1), lambda c: (0, 0, 0)),
        ],
        out_specs=pl.BlockSpec((C, v_w), lambda c: (c, 0)),
        out_shape=jax.ShapeDtypeStruct((T, v_w), BF16),
        scratch_shapes=[pltpu.VMEM((RET_HEADS, RET_QK_DIM, RET_V_DIM), F32)],
        compiler_params=_params("arbitrary"),
        name="retention",
    )(rq, rk, rv, sz, decay, q_decay, k_decay, chunk_decay)


def _head_cols(order):
    return np.concatenate([np.arange(h * HEAD_DIM, (h + 1) * HEAD_DIM) for h in order])


def _one_hot_blocks(T, block):
    b = (np.arange(T) // block) % LANES
    return jnp.asarray(np.eye(LANES, dtype=np.float32)[b], dtype=BF16)


def kernel(x, mem, positions, l0_norm_g, l0_w_in, l0_nsa_gate_b, l0_cmp_pe_k, l0_cmp_w1_k, l0_cmp_w2_k,
           l0_cmp_pe_v, l0_cmp_w1_v, l0_cmp_w2_v, l0_w_mem_kv, l0_w_out,
           l1_norm_g, l1_w_in, l1_w_mem_kv, l1_w_out, mem_norm_g, final_norm_g):
    B, T, _ = x.shape
    assert B == 1 and T % K_TILE == 0 and T >= NSA_WINDOW + Q_TILE and NSA_WINDOW % Q_TILE == 0
    assert T % RET_CHUNK == 0 and T % MOBA_Q_TILE == 0 and T % OUT_ROWS == 0
    n_sel = T // NSA_SEL_LEN
    n_selp = -(-n_sel // LANES) * LANES
    k_sel = min(NSA_SEL_TOPK, n_sel)
    assert k_sel >= 3
    n_blk = T // MOBA_BLOCK
    assert n_blk <= LANES
    moba_top = max(1, min(MOBA_TOPK, n_blk - 1))
    x2 = x[0]
    pos = positions[0].astype(F32)

    attn_inv = 1.0 / (ROPE_THETA ** (jnp.arange(0, HEAD_DIM, 2, dtype=F32) / HEAD_DIM))
    ret_inv = 1.0 / (ROPE_THETA ** jnp.linspace(0.0, 1.0, RET_QK_DIM // 2, dtype=F32))
    ang = pos[:, None] * attn_inv
    cos64 = jnp.tile(jnp.cos(ang), (1, 4))
    sin64 = jnp.tile(jnp.concatenate([-jnp.sin(ang), jnp.sin(ang)], axis=1), (1, 2))
    rang = pos[:, None] * ret_inv
    cos256, sin256 = jnp.cos(rang), jnp.sin(rang)

    offs = np.cumsum([0, 512, 128, 128, 128, 128, 128, 128, 24, 512, 512, 512, 512, 512, 256, 256])
    nsa_cols = _head_cols(NSA_HEAD_ORDER)
    n_gate = offs[8] - offs[7]
    col_idx = np.concatenate([
        offs[0] + nsa_cols,
        np.arange(offs[1], offs[7]),
        np.arange(offs[7], offs[8]), np.zeros(LANES - n_gate, np.int64),
        offs[8] + nsa_cols,
        np.arange(offs[9], offs[-1]),
    ])
    keep = np.ones(col_idx.shape, np.float32)
    keep[offs[7] + n_gate:offs[7] + LANES] = 0.0
    w0 = jnp.take(l0_w_in.astype(BF16), col_idx, axis=1) * jnp.asarray(keep, BF16)[None, :]
    gate_b = jnp.concatenate([l0_nsa_gate_b, jnp.zeros((LANES - 24,), F32)])[None, :]
    groups0 = [
        (512, "rot64", LOG2_SCALE), (128, "rot64", 1.0), (128, "plain", 1.0), (128, "rot64", 1.0),
        (128, "ones_pair", 1.0), (128, "rot64", 1.0), (128, "ones_pair", 1.0), (128, "gate", 1.0), (512, "silu", 1.0),
        (512, "rot64", LOG2_SCALE), (512, "rot64_mean", 1.0), (512, "ones_pair", 1.0), (512, "silu", 1.0),
        (256, "plain", 0.125), (256, "silu", 1.0),
    ]
    dts0 = [BF16] * 7 + [F32] + [BF16] * 7
    (nq, nkc, nvc, nks, nvs_a, nvs_b, nkw, nvw_a, nvw_b, gates, snz, mq, mk, kmean, mv_a, mv_b, smz, eq0, sez0) = _proj(
        x2, l0_norm_g[None, :], w0, cos64, sin64, gate_b, groups0, dts0, "proj0")

    kv0, kv1 = _mem_kv(mem[0], mem_norm_g[None, :], l0_w_mem_kv.astype(BF16), l1_w_mem_kv.astype(BF16))

    kc = _compress(nkc, l0_cmp_pe_k, l0_cmp_w1_k, l0_cmp_w2_k, "compress_k")
    vc = _compress(nvc, l0_cmp_pe_v, l0_cmp_w1_v, l0_cmp_w2_v, "compress_v")
    n_cmp_rows = T // NSA_CMP_STRIDE
    cmp_start = np.arange(n_cmp_rows)[:, None] * NSA_CMP_STRIDE
    sel_start = np.arange(n_selp)[None, :] * NSA_SEL_LEN
    overlap = ((cmp_start < sel_start + NSA_SEL_LEN) & (cmp_start + NSA_CMP_LEN > sel_start)
               & (np.arange(n_selp)[None, :] < n_sel))
    ov = jnp.asarray(overlap.T.astype(np.float32), dtype=BF16)
    ocw, sel_bias = _nsa_cmp(nq, kc, vc, ov, nkw, (nvw_a, nvw_b), gates, k_sel)
    ag = _nsa_sel(nq, sel_bias, nks, (nvs_a, nvs_b), _one_hot_blocks(T, NSA_SEL_LEN), gates, ocw, snz)

    n_blk_rows = -(-n_blk // 16) * 16
    kmean_p = jnp.concatenate([kmean[:, 0, :], jnp.zeros((n_blk_rows - n_blk, 512), F32)], axis=0).astype(BF16)
    bg = _moba(mq, kmean_p, mk, (mv_a, mv_b), _one_hot_blocks(T, MOBA_BLOCK), smz, moba_top)

    w_out0 = l0_w_out.astype(BF16)
    wa = w_out0[:512][nsa_cols]
    wb = w_out0[512:1024]
    wm0 = w_out0[1024:]
    x1 = _out0(ag, bg, eq0, sez0, kv0, wa, wb, wm0, x2)

    groups1 = [
        (1024, "rot256", 1.0), (1024, "rot256", RET_QK_DIM ** -0.5), (2048, "plain", 1.0), (2048, "silu", 1.0),
        (256, "plain", 0.125), (256, "silu", 1.0),
    ]
    dts1 = [BF16, F32, BF16, BF16, BF16, BF16]
    rq, rk, rv, srz, eq1, sez1 = _proj(x1, l1_norm_g[None, :], l1_w_in.astype(BF16), cos256, sin256,
                                       jnp.zeros((1, LANES), F32), groups1, dts1, "proj1")
    rg = _retention(rq, rk, rv, srz)
    v_w = RET_HEADS * RET_V_DIM
    out = _out1(rg, eq1, sez1, kv1, l1_w_out[:v_w].astype(BF16), l1_w_out[v_w:].astype(BF16), x1,
                final_norm_g[None, :])
    return out[None]
```

```python
import functools

import numpy as np
import jax
import jax.numpy as jnp
from jax import lax
from jax.experimental import pallas as pl
from jax.experimental.pallas import tpu as pltpu

F32 = jnp.float32
BF16 = jnp.bfloat16

D_MODEL = 1024
HEAD_DIM = 64
ROPE_THETA = 10000.0
NORM_EPS = 1e-6
NEG_INF = -1e30
MASK_BIAS = NEG_INF
LOG2_SCALE = HEAD_DIM ** -0.5 * float(np.log2(np.e))

NSA_HEADS = 8
NSA_GROUP_HEADS = 4
NSA_CMP_LEN = 32
NSA_CMP_STRIDE = 16
NSA_SEL_LEN = 64
NSA_SEL_TOPK = 16
NSA_WINDOW = 512
MOBA_HEADS = 8
MOBA_BLOCK = 256
MOBA_TOPK = 3
MEM_LEN = 256
MEM_HEADS = 4
RET_HEADS = 4
RET_QK_DIM = 256
RET_V_DIM = 512
RET_CHUNK = 256

LANES = 128
PROJ_ROWS = 512
PROJ_COLS = 512
OUT_ROWS = 512
Q_TILE = 128
K_TILE = 512
VMEM_LIMIT = 56 * 1024 * 1024

NSA_HEAD_ORDER = (0, 4, 1, 5, 2, 6, 3, 7)


def _dot(a, b):
    return jnp.dot(a, b, preferred_element_type=F32)


def _dot_nt(a, b):
    return lax.dot_general(a, b, (((1,), (1,)), ((), ())), preferred_element_type=F32)


def _dot_tn(a, b):
    return lax.dot_general(a, b, (((0,), (0,)), ((), ())), preferred_element_type=F32)


def _silu(y):
    return y * (1.0 / (1.0 + jnp.exp(-y)))


def _rms(x, g):
    return x * lax.rsqrt(jnp.mean(x * x, axis=-1, keepdims=True) + NORM_EPS) * g


def _params(*sem):
    return pltpu.CompilerParams(dimension_semantics=sem, vmem_limit_bytes=VMEM_LIMIT)


def _proj_kernel(x_ref, g_ref, w_ref, c_ref, s_ref, b_ref, *out_refs, groups):
    h = _rms(x_ref[...], g_ref[...]).astype(BF16)
    rows = x_ref.shape[0]
    lane = lax.broadcasted_iota(jnp.int32, (rows, LANES), 1)
    low_half = (lane & 32) == 0
    pieces = []
    col = 0
    oi = 0
    for width, kind, scale in groups:
        out_ref = out_refs[oi]
        oi += 1
        mean_ref = None
        if kind in ("rot64_mean", "ones_pair"):
            mean_ref = out_refs[oi]
            oi += 1
        pw = RET_QK_DIM if kind == "rot256" else LANES
        for c0 in range(0, width, pw):
            pieces.append((col + c0, pw, kind, scale, out_ref, c0, mean_ref))
        col += width

    def epilogue(z, kind, scale):
        if kind in ("rot64", "rot64_mean"):
            partner = jnp.where(low_half, pltpu.roll(z, LANES - 32, 1), pltpu.roll(z, 32, 1))
            z = z * c_ref[...] + partner * s_ref[...]
        elif kind == "rot256":
            x1, x2 = z[:, :LANES], z[:, LANES:]
            z = jnp.concatenate([x1 * c_ref[...] - x2 * s_ref[...], x1 * s_ref[...] + x2 * c_ref[...]], axis=1)
        elif kind == "silu":
            z = _silu(z)
        elif kind == "gate":
            z = 1.0 / (1.0 + jnp.exp(-(z + b_ref[...])))
        return z if scale == 1.0 else z * scale

    i = 0
    while i < len(pieces):
        j, run_w = i, 0
        while j < len(pieces) and run_w + pieces[j][1] <= PROJ_COLS:
            run_w += pieces[j][1]
            j += 1
        c_start = pieces[i][0]
        y = _dot(h, w_ref[:, c_start:c_start + run_w])
        for pc, pw, kind, scale, out_ref, oc, mean_ref in pieces[i:j]:
            z = epilogue(y[:, pc - c_start:pc - c_start + pw], kind, scale)
            if kind == "ones_pair":
                out_ref[:, oc:oc + pw] = jnp.where(lane < HEAD_DIM, z, 1.0).astype(out_ref.dtype)
                mean_ref[:, oc:oc + pw] = jnp.where(lane >= HEAD_DIM, z, 1.0).astype(mean_ref.dtype)
                continue
            if mean_ref is not None:
                for r in range(rows // MOBA_BLOCK):
                    mean_ref[r, :, oc:oc + pw] = jnp.mean(z[r * MOBA_BLOCK:(r + 1) * MOBA_BLOCK], axis=0, keepdims=True)
            out_ref[:, oc:oc + pw] = z.astype(out_ref.dtype)
        i = j


def _proj(x, g, w, cos_t, sin_t, bias, groups, out_dtypes, name):
    T = x.shape[0]
    n_tiles = T // PROJ_ROWS
    out_shapes = []
    out_specs = []
    for (width, kind, _), dt in zip(groups, out_dtypes):
        out_shapes.append(jax.ShapeDtypeStruct((T, width), dt))
        out_specs.append(pl.BlockSpec((PROJ_ROWS, width), lambda i: (i, 0)))
        if kind == "rot64_mean":
            out_shapes.append(jax.ShapeDtypeStruct((T // MOBA_BLOCK, 1, width), F32))
            out_specs.append(pl.BlockSpec((PROJ_ROWS // MOBA_BLOCK, 1, width), lambda i: (i, 0, 0)))
        if kind == "ones_pair":
            out_shapes.append(jax.ShapeDtypeStruct((T, width), dt))
            out_specs.append(pl.BlockSpec((PROJ_ROWS, width), lambda i: (i, 0)))
    return pl.pallas_call(
        functools.partial(_proj_kernel, groups=tuple(groups)),
        grid=(n_tiles,),
        in_specs=[
            pl.BlockSpec((PROJ_ROWS, D_MODEL), lambda i: (i, 0)),
            pl.BlockSpec((1, D_MODEL), lambda i: (0, 0)),
            pl.BlockSpec(w.shape, lambda i: (0, 0), pipeline_mode=pl.Buffered(1)),
            pl.BlockSpec((PROJ_ROWS, LANES), lambda i: (i, 0)),
            pl.BlockSpec((PROJ_ROWS, LANES), lambda i: (i, 0)),
            pl.BlockSpec((1, LANES), lambda i: (0, 0)),
        ],
        out_specs=out_specs,
        out_shape=out_shapes,
        compiler_params=_params("arbitrary"),
        name=name,
    )(x, g, w, cos_t, sin_t, bias)


def _compress_kernel(a_ref, an_ref, pea_ref, peb_ref, w1a_ref, w1b_ref, w2_ref, out_ref):
    xa = (a_ref[...].astype(F32) + pea_ref[...]).astype(BF16)
    xb = (an_ref[...].astype(F32) + peb_ref[...]).astype(BF16)
    hid = _silu(_dot(xa, w1a_ref[...]) + _dot(xb, w1b_ref[...]))
    out_ref[...] = _dot(hid.astype(BF16), w2_ref[...]).astype(out_ref.dtype)


def _compress(a, pe, w1, w2, name):
    T = a.shape[0]
    n_rows = T // NSA_CMP_STRIDE
    row_w = NSA_CMP_STRIDE * LANES
    a2 = a.reshape(n_rows, row_w)
    a2n = jnp.concatenate([a2[1:], jnp.zeros((1, row_w), a2.dtype)], axis=0)
    eye2 = jnp.eye(2, dtype=F32)

    def expand(w1_half):
        return jnp.einsum("lde,gh->lgdhe", w1_half, eye2).reshape(row_w, LANES).astype(BF16)

    def pe_row(pe_half):
        return jnp.broadcast_to(pe_half[:, None, :], (NSA_CMP_STRIDE, 2, HEAD_DIM)).reshape(1, row_w)

    w2bd = jnp.einsum("ef,gh->gehf", w2, eye2).reshape(LANES, LANES).astype(BF16)
    tile = min(256, n_rows)
    return pl.pallas_call(
        _compress_kernel,
        grid=(n_rows // tile,),
        in_specs=[
            pl.BlockSpec((tile, row_w), lambda i: (i, 0)),
            pl.BlockSpec((tile, row_w), lambda i: (i, 0)),
            pl.BlockSpec((1, row_w), lambda i: (0, 0)),
            pl.BlockSpec((1, row_w), lambda i: (0, 0)),
            pl.BlockSpec((row_w, LANES), lambda i: (0, 0)),
            pl.BlockSpec((row_w, LANES), lambda i: (0, 0)),
            pl.BlockSpec((LANES, LANES), lambda i: (0, 0)),
        ],
        out_specs=pl.BlockSpec((tile, LANES), lambda i: (i, 0)),
        out_shape=jax.ShapeDtypeStruct((n_rows, LANES), BF16),
        compiler_params=_params("arbitrary"),
        name=name,
    )(a2, a2n, pe_row(pe[:NSA_CMP_STRIDE]), pe_row(pe[NSA_CMP_STRIDE:]),
      expand(w1[:NSA_CMP_STRIDE]), expand(w1[NSA_CMP_STRIDE:]), w2bd)


def _nsa_head_query(q, h, lane):
    colv = q[:, (h % 4) * LANES:(h % 4 + 1) * LANES]
    keep = (lane < HEAD_DIM) if h < NSA_GROUP_HEADS else (lane >= HEAD_DIM)
    return jnp.where(keep, colv, jnp.zeros_like(colv))


def _nsa_stacked_queries(q, lane):
    return jnp.concatenate([_nsa_head_query(q, h, lane) for h in range(NSA_HEADS)], axis=0)


def _nsa_interleave_heads(o, gates, gate0, lane, out_ref, accumulate=False):
    tq = out_ref.shape[0]
    outs = [o[h * tq:(h + 1) * tq] * gates[:, gate0 + h:gate0 + h + 1] for h in range(NSA_HEADS)]
    for j in range(4):
        cs = slice(j * LANES, (j + 1) * LANES)
        a = jnp.where(lane < HEAD_DIM, outs[j], outs[4 + j])
        out_ref[:, cs] = out_ref[:, cs] + a if accumulate else a


def _values_with_ones(v):
    ones = jnp.ones((v.shape[0], HEAD_DIM), v.dtype)
    return jnp.concatenate([v[:, :HEAD_DIM], ones], axis=1), jnp.concatenate([ones, v[:, HEAD_DIM:]], axis=1)


def _top_k_flags(x, idx, k):
    sel = jnp.zeros(x.shape, jnp.bool_)
    for _ in range(k):
        mx = jnp.max(x, axis=0, keepdims=True)
        first = jnp.min(jnp.where(x == mx, idx, x.shape[0]), axis=0, keepdims=True)
        hit = idx == first
        sel = sel | hit
        x = jnp.where(hit, -jnp.inf, x)
    return sel


CMP_KEY_BUCKET = 128


def _nsa_cmp_kernel(q_ref, kc_ref, vca_ref, vcb_ref, ovt_ref, kw_ref, wa_ref, wb_ref, gate_ref, oc_ref, bias_ref,
                    *, k_sel):
    tq = q_ref.shape[0]
    n_cmp = kc_ref.shape[0]
    n_selp = ovt_ref.shape[0]
    rows = NSA_HEADS * tq
    half = rows // 2
    t0 = pl.program_id(0) * tq
    lane = lax.broadcasted_iota(jnp.int32, (tq, LANES), 1)
    t_rows = t0 + (lax.broadcasted_iota(jnp.int32, (rows, 1), 0) & (tq - 1))

    span = NSA_WINDOW + tq
    start = pl.multiple_of(jnp.maximum(t0 - NSA_WINDOW, 0), tq)

    def window(mask_scores):
        s = mask_scores(_dot_nt(_nsa_stacked_queries(q_ref[...], lane), kw_ref[pl.ds(start, span), :]))
        pb = jnp.exp2(s - jnp.max(s, axis=-1, keepdims=True)).astype(BF16)
        acc = jnp.concatenate([_dot(pb[:half], wa_ref[pl.ds(start, span), :]),
                               _dot(pb[half:], wb_ref[pl.ds(start, span), :])], axis=0)
        _nsa_interleave_heads(acc / pltpu.roll(acc, HEAD_DIM, 1), gate_ref[...], 2 * NSA_HEADS, lane, oc_ref)

    @pl.when(t0 >= NSA_WINDOW)
    def _():
        q_local = lax.broadcasted_iota(jnp.int32, (rows, 1), 0) & (tq - 1)
        k_local = lax.broadcasted_iota(jnp.int32, (1, tq), 1)
        window(lambda s: jnp.concatenate([jnp.where(k_local > q_local, s[:, :tq], NEG_INF), s[:, tq:NSA_WINDOW],
                                          jnp.where(k_local <= q_local, s[:, NSA_WINDOW:], NEG_INF)], axis=1))

    @pl.when(t0 < NSA_WINDOW)
    def _():
        wpos = lax.broadcasted_iota(jnp.int32, (1, span), 1)
        window(lambda s: jnp.where(wpos <= t_rows, s, NEG_INF))

    def attend(nk):
        cmp_end = lax.broadcasted_iota(jnp.int32, (1, nk), 1) * NSA_CMP_STRIDE + (NSA_CMP_LEN - 1)
        s = _dot_nt(_nsa_stacked_queries(q_ref[...], lane), kc_ref[0:nk, :])
        s = jnp.where(cmp_end <= t_rows, s, NEG_INF)
        m = jnp.max(s, axis=-1, keepdims=True)
        p = jnp.exp2(s - m)
        pb = p.astype(BF16)
        acc_a = _dot(pb[:half], vca_ref[0:nk, :])
        acc_b = _dot(pb[half:], vcb_ref[0:nk, :])
        l = jnp.concatenate([acc_a[:, HEAD_DIM:HEAD_DIM + 1], acc_b[:, 0:1]], axis=0)
        rl = jnp.where(m > 0.5 * NEG_INF, 1.0 / jnp.maximum(l, 1e-30), 0.0)
        _nsa_interleave_heads(jnp.concatenate([acc_a, acc_b], axis=0) * rl, gate_ref[...], 0, lane, oc_ref,
                              accumulate=True)
        pr = p * rl
        nb = min(n_selp, nk * NSA_CMP_STRIDE // NSA_SEL_LEN)
        blk = lax.broadcasted_iota(jnp.int32, (nb, tq), 0)
        cur = (t0 + lax.broadcasted_iota(jnp.int32, (1, tq), 1)) >> 6
        valid = blk <= cur
        forced = (blk == 0) | (blk == cur) | (blk == cur - 1)
        for g in range(2):
            r0 = g * half
            pg = pr[r0:r0 + tq] + pr[r0 + tq:r0 + 2 * tq] + pr[r0 + 2 * tq:r0 + 3 * tq] + pr[r0 + 3 * tq:r0 + 4 * tq]
            imp_t = _dot_nt(ovt_ref[0:nb, 0:nk], pg.astype(BF16))
            x = jnp.where(valid & jnp.logical_not(forced), imp_t, NEG_INF)
            sel = _top_k_flags(x, blk, k_sel - 3) | forced
            bias_t = jnp.where(sel & valid, 0.0, MASK_BIAS)
            if nb < n_selp:
                bias_t = jnp.concatenate([bias_t, jnp.full((n_selp - nb, tq), MASK_BIAS, F32)], axis=0)
            bias_ref[g] = bias_t.T.astype(BF16)

    needed = (t0 + tq - NSA_CMP_LEN) // NSA_CMP_STRIDE + 1
    buckets = sorted({min(b, n_cmp) for b in range(CMP_KEY_BUCKET, n_cmp + CMP_KEY_BUCKET, CMP_KEY_BUCKET)})
    for i, nk in enumerate(buckets):
        lo = buckets[i - 1] if i else -(1 << 30)
        cond = (needed > lo) if i == len(buckets) - 1 else ((needed > lo) & (needed <= nk))
        pl.when(cond)(functools.partial(attend, nk))


def _nsa_cmp(nq, kc, vc, ov, kw, vw, gates, k_sel):
    T = nq.shape[0]
    n_cmp = kc.shape[0]
    n_selp = ov.shape[0]
    vca, vcb = _values_with_ones(vc)
    wa, wb = vw if isinstance(vw, tuple) else _values_with_ones(vw)
    cmp_full = pl.BlockSpec((n_cmp, LANES), lambda i: (0, 0))
    full = pl.BlockSpec((T, LANES), lambda i: (0, 0))
    return pl.pallas_call(
        functools.partial(_nsa_cmp_kernel, k_sel=k_sel),
        grid=(T // Q_TILE,),
        in_specs=[
            pl.BlockSpec((Q_TILE, 512), lambda i: (i, 0)),
            cmp_full, cmp_full, cmp_full,
            pl.BlockSpec((n_selp, n_cmp), lambda i: (0, 0)),
            full, full, full,
            pl.BlockSpec((Q_TILE, LANES), lambda i: (i, 0)),
        ],
        out_specs=[
            pl.BlockSpec((Q_TILE, 512), lambda i: (i, 0)),
            pl.BlockSpec((2, Q_TILE, n_selp), lambda i: (0, i, 0)),
        ],
        out_shape=[
            jax.ShapeDtypeStruct((T, 512), F32),
            jax.ShapeDtypeStruct((2, T, n_selp), BF16),
        ],
        compiler_params=_params("arbitrary"),
        name="nsa_cmp_win_select",
    )(nq, kc, vca, vcb, ov, kw, wa, wb, gates)


def _flash_scratch(rows):
    return [
        pltpu.VMEM((rows, K_TILE), F32), pltpu.VMEM((rows, K_TILE), F32),
        pltpu.VMEM((rows, K_TILE), BF16), pltpu.VMEM((rows, K_TILE), BF16),
        pltpu.VMEM((rows, 1), F32), pltpu.VMEM((rows, 1), F32),
        pltpu.VMEM((rows, 1), F32),
        pltpu.VMEM((rows, LANES), F32),
    ]


def _flash_body(lhs_ref, k_ref, e_ref, va_ref, vb_ref, t_of_rows, n_full, tokens_per_half, scratch):
    s_a, s_b, p_a, p_b, al_a, al_b, m_ref, acc_ref = scratch
    tk = K_TILE
    rows = acc_ref.shape[0]
    half_rows = rows // 2
    last_mem_tile = k_ref.shape[0] // tk - 1

    def tile_start(j):
        return pl.multiple_of(jnp.clip(j, 0, last_mem_tile) * tk, tk)

    def scores(j, s_out):
        k0 = tile_start(j)
        rhs = jnp.concatenate([k_ref[pl.ds(k0, tk), :], e_ref[pl.ds(k0, tk), :]], axis=1)
        s_out[...] = _dot_nt(lhs_ref[k0 // tokens_per_half], rhs)

    def softmax(j, s_in, p_out, al_out, causal):
        s = s_in[...]
        if causal:
            kpos = j * tk + lax.broadcasted_iota(jnp.int32, (1, tk), 1)
            s = jnp.where(kpos <= t_of_rows(0, rows), s, NEG_INF)
        m_old = m_ref[...]
        m_new = jnp.maximum(m_old, jnp.max(s, axis=-1, keepdims=True))
        m_ref[...] = m_new
        al_out[...] = jnp.exp2(m_old - m_new)
        p_out[...] = jnp.exp2(s - m_new).astype(BF16)

    def values(j, p_in, al_in):
        k0 = tile_start(j)
        for rs, v_ref in ((slice(0, half_rows), va_ref), (slice(half_rows, rows), vb_ref)):
            acc_ref[rs, :] = al_in[rs, :] * acc_ref[rs, :] + _dot(p_in[rs, :], v_ref[pl.ds(k0, tk), :])

    m_ref[...] = jnp.full(m_ref.shape, NEG_INF, F32)
    acc_ref[...] = jnp.zeros(acc_ref.shape, F32)
    p_b[...] = jnp.zeros(p_b.shape, BF16)
    al_b[...] = jnp.ones(al_b.shape, F32)
    scores(0, s_a)

    def two_full_tiles(i, carry):
        j = 2 * i
        values(j - 1, p_b, al_b)
        scores(j + 1, s_b)
        softmax(j, s_a, p_a, al_a, False)
        values(j, p_a, al_a)
        scores(j + 2, s_a)
        softmax(j + 1, s_b, p_b, al_b, False)
        return carry

    n_pairs = n_full // 2
    lax.fori_loop(0, n_pairs, two_full_tiles, 0)
    j = 2 * n_pairs
    odd = (n_full & 1) == 1

    @pl.when(odd)
    def _():
        values(j - 1, p_b, al_b)
        scores(j + 1, s_b)
        softmax(j, s_a, p_a, al_a, True)
        values(j, p_a, al_a)
        softmax(j + 1, s_b, p_b, al_b, True)
        values(j + 1, p_b, al_b)

    @pl.when(jnp.logical_not(odd))
    def _():
        values(j - 1, p_b, al_b)
        softmax(j, s_a, p_a, al_a, True)
        values(j, p_a, al_a)

    acc = acc_ref[...]
    return acc / pltpu.roll(acc, HEAD_DIM, 1)


def _nsa_sel_kernel(q_ref, bias_ref, ks_ref, va_ref, vb_ref, e_ref, gate_ref, ocw_ref, sz_ref, out_ref,
                    lhs_ref, *scratch):
    tq = q_ref.shape[0]
    t0 = pl.program_id(0) * tq
    lane = lax.broadcasted_iota(jnp.int32, (tq, LANES), 1)
    q = q_ref[...]
    for half in range(lhs_ref.shape[0]):
        for h in range(NSA_HEADS):
            b = bias_ref[h // NSA_GROUP_HEADS, :, half * LANES:(half + 1) * LANES]
            lhs_ref[half, h * tq:(h + 1) * tq, :] = jnp.concatenate([_nsa_head_query(q, h, lane), b], axis=1)

    def t_of_rows(r0, n):
        return t0 + ((r0 + lax.broadcasted_iota(jnp.int32, (n, 1), 0)) & (tq - 1))

    o = _flash_body(lhs_ref, ks_ref, e_ref, va_ref, vb_ref, t_of_rows, t0 // K_TILE, NSA_SEL_LEN * LANES, scratch)
    gates = gate_ref[...]
    outs = [o[h * tq:(h + 1) * tq] * gates[:, NSA_HEADS + h:NSA_HEADS + h + 1] for h in range(NSA_HEADS)]
    for j in range(4):
        cs = slice(j * LANES, (j + 1) * LANES)
        a = jnp.where(lane < HEAD_DIM, outs[j], outs[4 + j]) + ocw_ref[:, cs]
        out_ref[:, cs] = (a * sz_ref[:, cs].astype(F32)).astype(out_ref.dtype)


def _nsa_sel(nq, bias, ks, vs, e_sel, gates, ocw, sz):
    T = nq.shape[0]
    n_selp = bias.shape[2]
    rows = NSA_HEADS * Q_TILE
    row = lambda w: pl.BlockSpec((Q_TILE, w), lambda qi: (qi, 0))
    full = pl.BlockSpec((T, LANES), lambda qi: (0, 0))
    va, vb = vs if isinstance(vs, tuple) else _values_with_ones(vs)
    return pl.pallas_call(
        _nsa_sel_kernel,
        grid=(T // Q_TILE,),
        in_specs=[row(512), pl.BlockSpec((2, Q_TILE, n_selp), lambda qi: (0, qi, 0)), full, full, full, full,
                  row(LANES), row(512), row(512)],
        out_specs=row(512),
        out_shape=jax.ShapeDtypeStruct((T, 512), BF16),
        scratch_shapes=[pltpu.VMEM((n_selp // LANES, rows, 2 * LANES), BF16)] + _flash_scratch(rows),
        compiler_params=_params("arbitrary"),
        name="nsa_selected",
    )(nq, bias, ks, va, vb, e_sel, gates, ocw, sz)


MOBA_Q_TILE = 512


def _moba_kernel(q_ref, kmean_ref, k_ref, va_ref, vb_ref, e_ref, sz_ref, out_ref, lhs_ref, *scratch, top):
    tq = q_ref.shape[0]
    t0 = pl.program_id(1) * tq
    lane = lax.broadcasted_iota(jnp.int32, (tq, LANES), 1)
    colv = q_ref[...]
    zero = jnp.zeros_like(colv)
    qp = jnp.concatenate([jnp.where(lane < HEAD_DIM, colv, zero), jnp.where(lane >= HEAD_DIM, colv, zero)], axis=0)

    def t_of_rows(r0, n):
        return t0 + ((r0 + lax.broadcasted_iota(jnp.int32, (n, 1), 0)) & (tq - 1))

    n_blk_rows = kmean_ref.shape[0]
    cur = (t0 + (lax.broadcasted_iota(jnp.int32, (1, 2 * tq), 1) & (tq - 1))) >> 8
    blk = lax.broadcasted_iota(jnp.int32, (n_blk_rows, 2 * tq), 0)
    past = blk < cur
    gate = jnp.where(past, _dot_nt(kmean_ref[...], qp), NEG_INF)
    sel = _top_k_flags(gate, blk, top)
    bias_t = jnp.where((sel & past) | (blk == cur), 0.0, MASK_BIAS)
    if n_blk_rows < LANES:
        bias_t = jnp.concatenate([bias_t, jnp.full((LANES - n_blk_rows, 2 * tq), MASK_BIAS, F32)], axis=0)
    lhs_ref[0] = jnp.concatenate([qp, bias_t.T.astype(BF16)], axis=1)
    o = _flash_body(lhs_ref, k_ref, e_ref, va_ref, vb_ref, t_of_rows, t0 // K_TILE, MOBA_BLOCK * LANES, scratch)
    a = jnp.where(lane < HEAD_DIM, o[:tq], o[tq:])
    out_ref[...] = (a * sz_ref[...].astype(F32)).astype(out_ref.dtype)


def _moba(mq, kmean, mk, mv, e_blk, sz, top):
    T = mq.shape[0]
    tq = MOBA_Q_TILE
    assert MOBA_BLOCK == 256 and T * 1 <= MOBA_BLOCK * LANES
    rows = 2 * tq
    row = pl.BlockSpec((tq, LANES), lambda p, qi: (qi, p))
    col = pl.BlockSpec((T, LANES), lambda p, qi: (0, p))
    if isinstance(mv, tuple):
        va, vb = mv
    else:
        v4 = mv.reshape(T, MOBA_HEADS // 2, 2, HEAD_DIM)
        ones = jnp.ones((T, MOBA_HEADS // 2, HEAD_DIM), mv.dtype)
        va = jnp.stack([v4[:, :, 0], ones], axis=2).reshape(T, 512)
        vb = jnp.stack([ones, v4[:, :, 1]], axis=2).reshape(T, 512)
    return pl.pallas_call(
        functools.partial(_moba_kernel, top=top),
        grid=(MOBA_HEADS // 2, T // tq),
        in_specs=[row, pl.BlockSpec((kmean.shape[0], LANES), lambda p, qi: (0, p)), col, col, col,
                  pl.BlockSpec((T, LANES), lambda p, qi: (0, 0)), row],
        out_specs=row,
        out_shape=jax.ShapeDtypeStruct((T, 512), BF16),
        scratch_shapes=[pltpu.VMEM((1, rows, 2 * LANES), BF16)] + _flash_scratch(rows),
        compiler_params=_params("arbitrary", "arbitrary"),
        name="moba",
    )(mq, kmean, mk, va, vb, e_blk, sz)


def _mem_kv_kernel(mem_ref, g_ref, w0_ref, w1_ref, kv0_ref, kv1_ref):
    mem_n = _rms(mem_ref[...], g_ref[...]).astype(BF16)
    kv0_ref[...] = _dot(mem_n, w0_ref[...]).astype(kv0_ref.dtype)
    kv1_ref[...] = _dot(mem_n, w1_ref[...]).astype(kv1_ref.dtype)


def _mem_kv(mem, g, w0, w1):
    n = mem.shape[0]
    kv_w = 2 * MEM_HEADS * HEAD_DIM
    return pl.pallas_call(
        _mem_kv_kernel,
        grid=(1,),
        in_specs=[
            pl.BlockSpec((n, D_MODEL), lambda i: (0, 0)),
            pl.BlockSpec((1, D_MODEL), lambda i: (0, 0)),
            pl.BlockSpec((D_MODEL, kv_w), lambda i: (0, 0)),
            pl.BlockSpec((D_MODEL, kv_w), lambda i: (0, 0)),
        ],
        out_specs=[pl.BlockSpec((n, kv_w), lambda i: (0, 0)), pl.BlockSpec((n, kv_w), lambda i: (0, 0))],
        out_shape=[jax.ShapeDtypeStruct((n, kv_w), BF16), jax.ShapeDtypeStruct((n, kv_w), BF16)],
        compiler_params=_params("arbitrary"),
        name="mem_kv",
    )(mem, g, w0, w1)


def _mem_attention(eq, kv, sez, lane):
    rows = eq.shape[0]
    mem_w = MEM_HEADS * HEAD_DIM
    cols = []
    for p in range(MEM_HEADS // 2):
        cs = slice(p * LANES, (p + 1) * LANES)
        colv = eq[:, cs]
        zero = jnp.zeros_like(colv)
        qp = jnp.concatenate([jnp.where(lane < HEAD_DIM, colv, zero), jnp.where(lane >= HEAD_DIM, colv, zero)], axis=0)
        s = _dot_nt(qp, kv[:, cs])
        pe = jnp.exp(s - jnp.max(s, axis=-1, keepdims=True))
        o = _dot(pe.astype(BF16), kv[:, mem_w + p * LANES:mem_w + (p + 1) * LANES]) / jnp.sum(pe, axis=-1, keepdims=True)
        cols.append(jnp.where(lane < HEAD_DIM, o[:rows], o[rows:]) * sez[:, cs].astype(F32))
    return jnp.concatenate(cols, axis=1).astype(BF16)


def _out0_kernel(ag_ref, bg_ref, eq_ref, sez_ref, kv_ref, wa_ref, wb_ref, wm_ref, x_ref, out_ref):
    rows = x_ref.shape[0]
    lane = lax.broadcasted_iota(jnp.int32, (rows, LANES), 1)
    mg = _mem_attention(eq_ref[...], kv_ref[...], sez_ref[...], lane)
    y = _dot(ag_ref[...], wa_ref[...]) + _dot(bg_ref[...], wb_ref[...]) + _dot(mg, wm_ref[...])
    out_ref[...] = x_ref[...] + y


def _out0(ag, bg, eq, sez, kv, wa, wb, wm, x):
    T = x.shape[0]
    rt = OUT_ROWS
    row = lambda w: pl.BlockSpec((rt, w), lambda i: (i, 0))
    full = lambda a: pl.BlockSpec(a.shape, lambda i: (0, 0))
    return pl.pallas_call(
        _out0_kernel,
        grid=(T // rt,),
        in_specs=[row(512), row(512), row(256), row(256), full(kv), full(wa), full(wb), full(wm), row(D_MODEL)],
        out_specs=row(D_MODEL),
        out_shape=jax.ShapeDtypeStruct((T, D_MODEL), F32),
        compiler_params=_params("arbitrary"),
        name="out_proj0",
    )(ag, bg, eq, sez, kv, wa, wb, wm, x)


def _out1_kernel(rg_ref, eq_ref, sez_ref, kv_ref, wr_ref, wm_ref, x_ref, g_ref, out_ref):
    rows = x_ref.shape[0]
    lane = lax.broadcasted_iota(jnp.int32, (rows, LANES), 1)
    mg = _mem_attention(eq_ref[...], kv_ref[...], sez_ref[...], lane)
    y = _dot(rg_ref[...], wr_ref[...]) + _dot(mg, wm_ref[...])
    out_ref[...] = _rms(x_ref[...] + y, g_ref[...])


def _out1(rg, eq, sez, kv, wr, wm, x, g):
    T = x.shape[0]
    rt = OUT_ROWS
    row = lambda w: pl.BlockSpec((rt, w), lambda i: (i, 0))
    full = lambda a: pl.BlockSpec(a.shape, lambda i: (0, 0))
    return pl.pallas_call(
        _out1_kernel,
        grid=(T // rt,),
        in_specs=[row(RET_HEADS * RET_V_DIM), row(256), row(256), full(kv), full(wr), full(wm), row(D_MODEL), full(g)],
        out_specs=row(D_MODEL),
        out_shape=jax.ShapeDtypeStruct((T, D_MODEL), F32),
        compiler_params=_params("arbitrary"),
        name="out_proj1_norm",
    )(rg, eq, sez, kv, wr, wm, x, g)


def _retention_kernel(q_ref, k_ref, v_ref, sz_ref, decay_ref, qd_ref, kd_ref, cd_ref, out_ref, state_ref):
    @pl.when(pl.program_id(0) == 0)
    def _():
        state_ref[...] = jnp.zeros(state_ref.shape, F32)

    for h in range(RET_HEADS):
        qs = slice(h * RET_QK_DIM, (h + 1) * RET_QK_DIM)
        vsl = slice(h * RET_V_DIM, (h + 1) * RET_V_DIM)
        qh = q_ref[:, qs]
        kh = k_ref[:, qs]
        vh = v_ref[:, vsl]
        st = state_ref[h]
        inner = _dot_nt(qh, kh.astype(BF16)) * decay_ref[h]
        o = _dot(inner.astype(BF16), vh) + _dot(qh, st.astype(BF16)) * qd_ref[h]
        state_ref[h] = st * cd_ref[h] + _dot_tn((kh * kd_ref[h]).astype(BF16), vh)
        mu = jnp.mean(o, axis=-1, keepdims=True)
        d = o - mu
        on = d * lax.rsqrt(jnp.mean(d * d, axis=-1, keepdims=True) + NORM_EPS)
        out_ref[:, vsl] = (on * sz_ref[:, vsl].astype(F32)).astype(out_ref.dtype)


def _retention(rq, rk, rv, sz):
    T = rq.shape[0]
    C = RET_CHUNK
    log_g = jnp.log(1.0 - 2.0 ** (-5.0 - jnp.arange(RET_HEADS, dtype=F32)))
    i = jnp.arange(C, dtype=F32)
    diff = i[:, None] - i[None, :]
    decay = jnp.where(diff >= 0, jnp.exp(jnp.maximum(diff, 0.0)[None] * log_g[:, None, None]), 0.0)
    q_decay = jnp.exp((i + 1.0)[None, :] * log_g[:, None])[:, :, None]
    k_decay = jnp.exp((C - 1.0 - i)[None, :] * log_g[:, None])[:, :, None]
    chunk_decay = jnp.exp(C * log_g)[:, None, None]
    qk_w = RET_HEADS * RET_QK_DIM
    v_w = RET_HEADS * RET_V_DIM
    return pl.pallas_call(
        _retention_kernel,
        grid=(T // C,),
        in_specs=[
            pl.BlockSpec((C, qk_w), lambda c: (c, 0)),
            pl.BlockSpec((C, qk_w), lambda c: (c, 0)),
            pl.BlockSpec((C, v_w), lambda c: (c, 0)),
            pl.BlockSpec((C, v_w), lambda c: (c, 0)),
            pl.BlockSpec((RET_HEADS, C, C), lambda c: (0, 0, 0)),
            pl.BlockSpec((RET_HEADS, C, 1), lambda c: (0, 0, 0)),
            pl.BlockSpec((RET_HEADS, C, 1), lambda c: (0, 0, 0)),
            pl.BlockSpec((RET_HEADS, 1, 1), lambda c: (0, 0, 0)),
        ],
        out_specs=pl.BlockSpec((C, v_w), lambda c: (c, 0)),
        out_shape=jax.ShapeDtypeStruct((T, v_w), BF16),
        scratch_shapes=[pltpu.VMEM((RET_HEADS, RET_QK_DIM, RET_V_DIM), F32)],
        compiler_params=_params("arbitrary"),
        name="retention",
    )(rq, rk, rv, sz, decay, q_decay, k_decay, chunk_decay)


def _head_cols(order):
    return np.concatenate([np.arange(h * HEAD_DIM, (h + 1) * HEAD_DIM) for h in order])


def _one_hot_blocks(T, block):
    b = (np.arange(T) // block) % LANES
    return jnp.asarray(np.eye(LANES, dtype=np.float32)[b], dtype=BF16)


def kernel(x, mem, positions, l0_norm_g, l0_w_in, l0_nsa_gate_b, l0_cmp_pe_k, l0_cmp_w1_k, l0_cmp_w2_k,
           l0_cmp_pe_v, l0_cmp_w1_v, l0_cmp_w2_v, l0_w_mem_kv, l0_w_out,
           l1_norm_g, l1_w_in, l1_w_mem_kv, l1_w_out, mem_norm_g, final_norm_g):
    B, T, _ = x.shape
    assert B == 1 and T % K_TILE == 0 and T >= NSA_WINDOW + Q_TILE and NSA_WINDOW % Q_TILE == 0
    assert T % RET_CHUNK == 0 and T % MOBA_Q_TILE == 0 and T % OUT_ROWS == 0
    n_sel = T // NSA_SEL_LEN
    n_selp = -(-n_sel // LANES) * LANES
    k_sel = min(NSA_SEL_TOPK, n_sel)
    assert k_sel >= 3
    n_blk = T // MOBA_BLOCK
    assert n_blk <= LANES
    moba_top = max(1, min(MOBA_TOPK, n_blk - 1))
    x2 = x[0]
    pos = positions[0].astype(F32)

    attn_inv = 1.0 / (ROPE_THETA ** (jnp.arange(0, HEAD_DIM, 2, dtype=F32) / HEAD_DIM))
    ret_inv = 1.0 / (ROPE_THETA ** jnp.linspace(0.0, 1.0, RET_QK_DIM // 2, dtype=F32))
    ang = pos[:, None] * attn_inv
    cos32, sin32 = lax.optimization_barrier((jnp.cos(ang), jnp.sin(ang)))
    cos64 = jnp.tile(cos32, (1, 4))
    sin64 = jnp.tile(jnp.concatenate([-sin32, sin32], axis=1), (1, 2))
    rang = pos[:, None] * ret_inv
    cos256, sin256 = jnp.cos(rang), jnp.sin(rang)

    offs = np.cumsum([0, 512, 128, 128, 128, 128, 128, 128, 24, 512, 512, 512, 512, 512, 256, 256])
    nsa_cols = _head_cols(NSA_HEAD_ORDER)
    n_gate = offs[8] - offs[7]
    col_idx = np.concatenate([
        offs[0] + nsa_cols,
        np.arange(offs[1], offs[7]),
        np.arange(offs[7], offs[8]), np.zeros(LANES - n_gate, np.int64),
        offs[8] + nsa_cols,
        np.arange(offs[9], offs[-1]),
    ])
    keep = np.ones(col_idx.shape, np.float32)
    keep[offs[7] + n_gate:offs[7] + LANES] = 0.0
    w0 = jnp.take(l0_w_in.astype(BF16), col_idx, axis=1) * jnp.asarray(keep, BF16)[None, :]
    gate_b = jnp.concatenate([l0_nsa_gate_b, jnp.zeros((LANES - 24,), F32)])[None, :]
    groups0 = [
        (512, "rot64", LOG2_SCALE), (128, "rot64", 1.0), (128, "plain", 1.0), (128, "rot64", 1.0),
        (128, "ones_pair", 1.0), (128, "rot64", 1.0), (128, "ones_pair", 1.0), (128, "gate", 1.0), (512, "silu", 1.0),
        (512, "rot64", LOG2_SCALE), (512, "rot64_mean", 1.0), (512, "ones_pair", 1.0), (512, "silu", 1.0),
        (256, "plain", 0.125), (256, "silu", 1.0),
    ]
    dts0 = [BF16] * 7 + [F32] + [BF16] * 7
    (nq, nkc, nvc, nks, nvs_a, nvs_b, nkw, nvw_a, nvw_b, gates, snz, mq, mk, kmean, mv_a, mv_b, smz, eq0, sez0) = _proj(
        x2, l0_norm_g[None, :], w0, cos64, sin64, gate_b, groups0, dts0, "proj0")

    kv0, kv1 = _mem_kv(mem[0], mem_norm_g[None, :], l0_w_mem_kv.astype(BF16), l1_w_mem_kv.astype(BF16))

    kc = _compress(nkc, l0_cmp_pe_k, l0_cmp_w1_k, l0_cmp_w2_k, "compress_k")
    vc = _compress(nvc, l0_cmp_pe_v, l0_cmp_w1_v, l0_cmp_w2_v, "compress_v")
    n_cmp_rows = T // NSA_CMP_STRIDE
    cmp_start = np.arange(n_cmp_rows)[:, None] * NSA_CMP_STRIDE
    sel_start = np.arange(n_selp)[None, :] * NSA_SEL_LEN
    overlap = ((cmp_start < sel_start + NSA_SEL_LEN) & (cmp_start + NSA_CMP_LEN > sel_start)
               & (np.arange(n_selp)[None, :] < n_sel))
    ov = jnp.asarray(overlap.T.astype(np.float32), dtype=BF16)
    ocw, sel_bias = _nsa_cmp(nq, kc, vc, ov, nkw, (nvw_a, nvw_b), gates, k_sel)
    ag = _nsa_sel(nq, sel_bias, nks, (nvs_a, nvs_b), _one_hot_blocks(T, NSA_SEL_LEN), gates, ocw, snz)

    n_blk_rows = -(-n_blk // 16) * 16
    kmean_p = jnp.concatenate([kmean[:, 0, :], jnp.zeros((n_blk_rows - n_blk, 512), F32)], axis=0).astype(BF16)
    bg = _moba(mq, kmean_p, mk, (mv_a, mv_b), _one_hot_blocks(T, MOBA_BLOCK), smz, moba_top)

    w_out0 = l0_w_out.astype(BF16)
    wa = w_out0[:512][nsa_cols]
    wb = w_out0[512:1024]
    wm0 = w_out0[1024:]
    x1 = _out0(ag, bg, eq0, sez0, kv0, wa, wb, wm0, x2)

    groups1 = [
        (1024, "rot256", 1.0), (1024, "rot256", RET_QK_DIM ** -0.5), (2048, "plain", 1.0), (2048, "silu", 1.0),
        (256, "plain", 0.125), (256, "silu", 1.0),
    ]
    dts1 = [BF16, F32, BF16, BF16, BF16, BF16]
    rq, rk, rv, srz, eq1, sez1 = _proj(x1, l1_norm_g[None, :], l1_w_in.astype(BF16), cos256, sin256,
                                       jnp.zeros((1, LANES), F32), groups1, dts1, "proj1")
    rg = _retention(rq, rk, rv, srz)
    v_w = RET_HEADS * RET_V_DIM
    out = _out1(rg, eq1, sez1, kv1, l1_w_out[:v_w].astype(BF16), l1_w_out[v_w:].astype(BF16), x1,
                final_norm_g[None, :])
    return out[None]
```

```python
import functools

import numpy as np
import jax
import jax.numpy as jnp
from jax import lax
from jax.experimental import pallas as pl
from jax.experimental.pallas import tpu as pltpu

F32 = jnp.float32
BF16 = jnp.bfloat16

D_MODEL = 1024
HEAD_DIM = 64
ROPE_THETA = 10000.0
NORM_EPS = 1e-6
NEG_INF = -1e30
MASK_BIAS = NEG_INF
LOG2_SCALE = HEAD_DIM ** -0.5 * float(np.log2(np.e))

NSA_HEADS = 8
NSA_GROUP_HEADS = 4
NSA_CMP_LEN = 32
NSA_CMP_STRIDE = 16
NSA_SEL_LEN = 64
NSA_SEL_TOPK = 16
NSA_WINDOW = 512
MOBA_HEADS = 8
MOBA_BLOCK = 256
MOBA_TOPK = 3
MEM_LEN = 256
MEM_HEADS = 4
RET_HEADS = 4
RET_QK_DIM = 256
RET_V_DIM = 512
RET_CHUNK = 256

LANES = 128
PROJ_ROWS = 512
PROJ_COLS = 512
OUT_ROWS = 512
Q_TILE = 128
K_TILE = 512
VMEM_LIMIT = 56 * 1024 * 1024

NSA_HEAD_ORDER = (0, 4, 1, 5, 2, 6, 3, 7)


def _dot(a, b):
    return jnp.dot(a, b, preferred_element_type=F32)


def _dot_nt(a, b):
    return lax.dot_general(a, b, (((1,), (1,)), ((), ())), preferred_element_type=F32)


def _dot_tn(a, b):
    return lax.dot_general(a, b, (((0,), (0,)), ((), ())), preferred_element_type=F32)


def _silu(y):
    return y * (1.0 / (1.0 + jnp.exp(-y)))


def _rms(x, g):
    return x * lax.rsqrt(jnp.mean(x * x, axis=-1, keepdims=True) + NORM_EPS) * g


def _params(*sem):
    return pltpu.CompilerParams(dimension_semantics=sem, vmem_limit_bytes=VMEM_LIMIT)


def _proj_kernel(x_ref, g_ref, w_ref, c_ref, s_ref, b_ref, *out_refs, groups):
    _proj_body(x_ref[...], g_ref, w_ref, c_ref, s_ref, b_ref, out_refs, groups)


def _proj_body(x, g_ref, w_ref, c_ref, s_ref, b_ref, out_refs, groups):
    h = _rms(x, g_ref[...]).astype(BF16)
    rows = x.shape[0]
    lane = lax.broadcasted_iota(jnp.int32, (rows, LANES), 1)
    low_half = (lane & 32) == 0
    pieces = []
    col = 0
    oi = 0
    for width, kind, scale in groups:
        out_ref = out_refs[oi]
        oi += 1
        mean_ref = None
        if kind in ("rot64_mean", "ones_pair"):
            mean_ref = out_refs[oi]
            oi += 1
        pw = RET_QK_DIM if kind == "rot256" else LANES
        for c0 in range(0, width, pw):
            pieces.append((col + c0, pw, kind, scale, out_ref, c0, mean_ref))
        col += width

    def epilogue(z, kind, scale):
        if kind in ("rot64", "rot64_mean"):
            partner = jnp.where(low_half, pltpu.roll(z, LANES - 32, 1), pltpu.roll(z, 32, 1))
            z = z * c_ref[...] + partner * s_ref[...]
        elif kind == "rot256":
            x1, x2 = z[:, :LANES], z[:, LANES:]
            z = jnp.concatenate([x1 * c_ref[...] - x2 * s_ref[...], x1 * s_ref[...] + x2 * c_ref[...]], axis=1)
        elif kind == "silu":
            z = _silu(z)
        elif kind == "gate":
            z = 1.0 / (1.0 + jnp.exp(-(z + b_ref[...])))
        return z if scale == 1.0 else z * scale

    i = 0
    while i < len(pieces):
        j, run_w = i, 0
        while j < len(pieces) and run_w + pieces[j][1] <= PROJ_COLS:
            run_w += pieces[j][1]
            j += 1
        c_start = pieces[i][0]
        y = _dot(h, w_ref[:, c_start:c_start + run_w])
        for pc, pw, kind, scale, out_ref, oc, mean_ref in pieces[i:j]:
            z = epilogue(y[:, pc - c_start:pc - c_start + pw], kind, scale)
            if kind == "ones_pair":
                out_ref[:, oc:oc + pw] = jnp.where(lane < HEAD_DIM, z, 1.0).astype(out_ref.dtype)
                mean_ref[:, oc:oc + pw] = jnp.where(lane >= HEAD_DIM, z, 1.0).astype(mean_ref.dtype)
                continue
            if mean_ref is not None:
                for r in range(rows // MOBA_BLOCK):
                    mean_ref[r, :, oc:oc + pw] = jnp.mean(z[r * MOBA_BLOCK:(r + 1) * MOBA_BLOCK], axis=0, keepdims=True)
            out_ref[:, oc:oc + pw] = z.astype(out_ref.dtype)
        i = j


def _proj_out_specs(T, groups, out_dtypes):
    out_shapes = []
    out_specs = []
    for (width, kind, _), dt in zip(groups, out_dtypes):
        out_shapes.append(jax.ShapeDtypeStruct((T, width), dt))
        out_specs.append(pl.BlockSpec((PROJ_ROWS, width), lambda i: (i, 0)))
        if kind == "rot64_mean":
            out_shapes.append(jax.ShapeDtypeStruct((T // MOBA_BLOCK, 1, width), F32))
            out_specs.append(pl.BlockSpec((PROJ_ROWS // MOBA_BLOCK, 1, width), lambda i: (i, 0, 0)))
        if kind == "ones_pair":
            out_shapes.append(jax.ShapeDtypeStruct((T, width), dt))
            out_specs.append(pl.BlockSpec((PROJ_ROWS, width), lambda i: (i, 0)))
    return out_shapes, out_specs


def _proj_in_specs(w):
    return [
        pl.BlockSpec((1, D_MODEL), lambda i: (0, 0)),
        pl.BlockSpec(w.shape, lambda i: (0, 0), pipeline_mode=pl.Buffered(1)),
        pl.BlockSpec((PROJ_ROWS, LANES), lambda i: (i, 0)),
        pl.BlockSpec((PROJ_ROWS, LANES), lambda i: (i, 0)),
        pl.BlockSpec((1, LANES), lambda i: (0, 0)),
    ]


def _proj(x, g, w, cos_t, sin_t, bias, groups, out_dtypes, name):
    T = x.shape[0]
    out_shapes, out_specs = _proj_out_specs(T, groups, out_dtypes)
    return pl.pallas_call(
        functools.partial(_proj_kernel, groups=tuple(groups)),
        grid=(T // PROJ_ROWS,),
        in_specs=[pl.BlockSpec((PROJ_ROWS, D_MODEL), lambda i: (i, 0))] + _proj_in_specs(w),
        out_specs=out_specs,
        out_shape=out_shapes,
        compiler_params=_params("arbitrary"),
        name=name,
    )(x, g, w, cos_t, sin_t, bias)


def _compress_kernel(a_ref, an_ref, pea_ref, peb_ref, w1a_ref, w1b_ref, w2_ref, out_ref):
    xa = (a_ref[...].astype(F32) + pea_ref[...]).astype(BF16)
    xb = (an_ref[...].astype(F32) + peb_ref[...]).astype(BF16)
    hid = _silu(_dot(xa, w1a_ref[...]) + _dot(xb, w1b_ref[...]))
    out_ref[...] = _dot(hid.astype(BF16), w2_ref[...]).astype(out_ref.dtype)


def _compress(a, pe, w1, w2, name):
    T = a.shape[0]
    n_rows = T // NSA_CMP_STRIDE
    row_w = NSA_CMP_STRIDE * LANES
    a2 = a.reshape(n_rows, row_w)
    a2n = jnp.concatenate([a2[1:], jnp.zeros((1, row_w), a2.dtype)], axis=0)
    eye2 = jnp.eye(2, dtype=F32)

    def expand(w1_half):
        return jnp.einsum("lde,gh->lgdhe", w1_half, eye2).reshape(row_w, LANES).astype(BF16)

    def pe_row(pe_half):
        return jnp.broadcast_to(pe_half[:, None, :], (NSA_CMP_STRIDE, 2, HEAD_DIM)).reshape(1, row_w)

    w2bd = jnp.einsum("ef,gh->gehf", w2, eye2).reshape(LANES, LANES).astype(BF16)
    tile = min(256, n_rows)
    return pl.pallas_call(
        _compress_kernel,
        grid=(n_rows // tile,),
        in_specs=[
            pl.BlockSpec((tile, row_w), lambda i: (i, 0)),
            pl.BlockSpec((tile, row_w), lambda i: (i, 0)),
            pl.BlockSpec((1, row_w), lambda i: (0, 0)),
            pl.BlockSpec((1, row_w), lambda i: (0, 0)),
            pl.BlockSpec((row_w, LANES), lambda i: (0, 0)),
            pl.BlockSpec((row_w, LANES), lambda i: (0, 0)),
            pl.BlockSpec((LANES, LANES), lambda i: (0, 0)),
        ],
        out_specs=pl.BlockSpec((tile, LANES), lambda i: (i, 0)),
        out_shape=jax.ShapeDtypeStruct((n_rows, LANES), BF16),
        compiler_params=_params("arbitrary"),
        name=name,
    )(a2, a2n, pe_row(pe[:NSA_CMP_STRIDE]), pe_row(pe[NSA_CMP_STRIDE:]),
      expand(w1[:NSA_CMP_STRIDE]), expand(w1[NSA_CMP_STRIDE:]), w2bd)


def _nsa_head_query(q, h, lane):
    colv = q[:, (h % 4) * LANES:(h % 4 + 1) * LANES]
    keep = (lane < HEAD_DIM) if h < NSA_GROUP_HEADS else (lane >= HEAD_DIM)
    return jnp.where(keep, colv, jnp.zeros_like(colv))


def _nsa_stacked_queries(q, lane):
    return jnp.concatenate([_nsa_head_query(q, h, lane) for h in range(NSA_HEADS)], axis=0)


def _nsa_interleave_heads(o, gates, gate0, lane, out_ref, accumulate=False):
    tq = out_ref.shape[0]
    outs = [o[h * tq:(h + 1) * tq] * gates[:, gate0 + h:gate0 + h + 1] for h in range(NSA_HEADS)]
    for j in range(4):
        cs = slice(j * LANES, (j + 1) * LANES)
        a = jnp.where(lane < HEAD_DIM, outs[j], outs[4 + j])
        out_ref[:, cs] = out_ref[:, cs] + a if accumulate else a


def _values_with_ones(v):
    ones = jnp.ones((v.shape[0], HEAD_DIM), v.dtype)
    return jnp.concatenate([v[:, :HEAD_DIM], ones], axis=1), jnp.concatenate([ones, v[:, HEAD_DIM:]], axis=1)


def _top_k_flags(x, idx, k):
    sel = jnp.zeros(x.shape, jnp.bool_)
    for _ in range(k):
        mx = jnp.max(x, axis=0, keepdims=True)
        first = jnp.min(jnp.where(x == mx, idx, x.shape[0]), axis=0, keepdims=True)
        hit = idx == first
        sel = sel | hit
        x = jnp.where(hit, -jnp.inf, x)
    return sel


CMP_KEY_BUCKET = 128


def _nsa_cmp_kernel(q_ref, kc_ref, vca_ref, vcb_ref, ovt_ref, kw_ref, wa_ref, wb_ref, gate_ref, oc_ref, bias_ref,
                    *, k_sel):
    tq = q_ref.shape[0]
    n_cmp = kc_ref.shape[0]
    n_selp = ovt_ref.shape[0]
    rows = NSA_HEADS * tq
    half = rows // 2
    t0 = pl.program_id(0) * tq
    lane = lax.broadcasted_iota(jnp.int32, (tq, LANES), 1)
    t_rows = t0 + (lax.broadcasted_iota(jnp.int32, (rows, 1), 0) & (tq - 1))

    span = NSA_WINDOW + tq
    start = pl.multiple_of(jnp.maximum(t0 - NSA_WINDOW, 0), tq)

    def window(mask_scores):
        s = mask_scores(_dot_nt(_nsa_stacked_queries(q_ref[...], lane), kw_ref[pl.ds(start, span), :]))
        pb = jnp.exp2(s - jnp.max(s, axis=-1, keepdims=True)).astype(BF16)
        acc = jnp.concatenate([_dot(pb[:half], wa_ref[pl.ds(start, span), :]),
                               _dot(pb[half:], wb_ref[pl.ds(start, span), :])], axis=0)
        _nsa_interleave_heads(acc / pltpu.roll(acc, HEAD_DIM, 1), gate_ref[...], 2 * NSA_HEADS, lane, oc_ref)

    @pl.when(t0 >= NSA_WINDOW)
    def _():
        q_local = lax.broadcasted_iota(jnp.int32, (rows, 1), 0) & (tq - 1)
        k_local = lax.broadcasted_iota(jnp.int32, (1, tq), 1)
        window(lambda s: jnp.concatenate([jnp.where(k_local > q_local, s[:, :tq], NEG_INF), s[:, tq:NSA_WINDOW],
                                          jnp.where(k_local <= q_local, s[:, NSA_WINDOW:], NEG_INF)], axis=1))

    @pl.when(t0 < NSA_WINDOW)
    def _():
        wpos = lax.broadcasted_iota(jnp.int32, (1, span), 1)
        window(lambda s: jnp.where(wpos <= t_rows, s, NEG_INF))

    def attend(nk):
        cmp_end = lax.broadcasted_iota(jnp.int32, (1, nk), 1) * NSA_CMP_STRIDE + (NSA_CMP_LEN - 1)
        s = _dot_nt(_nsa_stacked_queries(q_ref[...], lane), kc_ref[0:nk, :])
        s = jnp.where(cmp_end <= t_rows, s, NEG_INF)
        m = jnp.max(s, axis=-1, keepdims=True)
        p = jnp.exp2(s - m)
        pb = p.astype(BF16)
        acc_a = _dot(pb[:half], vca_ref[0:nk, :])
        acc_b = _dot(pb[half:], vcb_ref[0:nk, :])
        l = jnp.concatenate([acc_a[:, HEAD_DIM:HEAD_DIM + 1], acc_b[:, 0:1]], axis=0)
        rl = jnp.where(m > 0.5 * NEG_INF, 1.0 / jnp.maximum(l, 1e-30), 0.0)
        _nsa_interleave_heads(jnp.concatenate([acc_a, acc_b], axis=0) * rl, gate_ref[...], 0, lane, oc_ref,
                              accumulate=True)
        pr = p * rl
        nb = min(n_selp, nk * NSA_CMP_STRIDE // NSA_SEL_LEN)
        blk = lax.broadcasted_iota(jnp.int32, (nb, tq), 0)
        cur = (t0 + lax.broadcasted_iota(jnp.int32, (1, tq), 1)) >> 6
        valid = blk <= cur
        forced = (blk == 0) | (blk == cur) | (blk == cur - 1)
        for g in range(2):
            r0 = g * half
            pg = pr[r0:r0 + tq] + pr[r0 + tq:r0 + 2 * tq] + pr[r0 + 2 * tq:r0 + 3 * tq] + pr[r0 + 3 * tq:r0 + 4 * tq]
            imp_t = _dot_nt(ovt_ref[0:nb, 0:nk], pg.astype(BF16))
            x = jnp.where(valid & jnp.logical_not(forced), imp_t, NEG_INF)
            sel = _top_k_flags(x, blk, k_sel - 3) | forced
            bias_t = jnp.where(sel & valid, 0.0, MASK_BIAS)
            if nb < n_selp:
                bias_t = jnp.concatenate([bias_t, jnp.full((n_selp - nb, tq), MASK_BIAS, F32)], axis=0)
            bias_ref[g] = bias_t.T.astype(BF16)

    needed = (t0 + tq - NSA_CMP_LEN) // NSA_CMP_STRIDE + 1
    buckets = sorted({min(b, n_cmp) for b in range(CMP_KEY_BUCKET, n_cmp + CMP_KEY_BUCKET, CMP_KEY_BUCKET)})
    for i, nk in enumerate(buckets):
        lo = buckets[i - 1] if i else -(1 << 30)
        cond = (needed > lo) if i == len(buckets) - 1 else ((needed > lo) & (needed <= nk))
        pl.when(cond)(functools.partial(attend, nk))


def _nsa_cmp(nq, kc, vc, ov, kw, vw, gates, k_sel):
    T = nq.shape[0]
    n_cmp = kc.shape[0]
    n_selp = ov.shape[0]
    vca, vcb = _values_with_ones(vc)
    wa, wb = vw if isinstance(vw, tuple) else _values_with_ones(vw)
    cmp_full = pl.BlockSpec((n_cmp, LANES), lambda i: (0, 0))
    full = pl.BlockSpec((T, LANES), lambda i: (0, 0))
    return pl.pallas_call(
        functools.partial(_nsa_cmp_kernel, k_sel=k_sel),
        grid=(T // Q_TILE,),
        in_specs=[
            pl.BlockSpec((Q_TILE, 512), lambda i: (i, 0)),
            cmp_full, cmp_full, cmp_full,
            pl.BlockSpec((n_selp, n_cmp), lambda i: (0, 0)),
            full, full, full,
            pl.BlockSpec((Q_TILE, LANES), lambda i: (i, 0)),
        ],
        out_specs=[
            pl.BlockSpec((Q_TILE, 512), lambda i: (i, 0)),
            pl.BlockSpec((2, Q_TILE, n_selp), lambda i: (0, i, 0)),
        ],
        out_shape=[
            jax.ShapeDtypeStruct((T, 512), F32),
            jax.ShapeDtypeStruct((2, T, n_selp), BF16),
        ],
        compiler_params=_params("arbitrary"),
        name="nsa_cmp_win_select",
    )(nq, kc, vca, vcb, ov, kw, wa, wb, gates)


def _flash_scratch(rows):
    return [
        pltpu.VMEM((rows, K_TILE), F32), pltpu.VMEM((rows, K_TILE), F32),
        pltpu.VMEM((rows, K_TILE), BF16), pltpu.VMEM((rows, K_TILE), BF16),
        pltpu.VMEM((rows, 1), F32), pltpu.VMEM((rows, 1), F32),
        pltpu.VMEM((rows, 1), F32),
        pltpu.VMEM((rows, LANES), F32),
    ]


def _flash_body(lhs_ref, k_ref, e_ref, va_ref, vb_ref, t_of_rows, n_full, tokens_per_half, scratch):
    s_a, s_b, p_a, p_b, al_a, al_b, m_ref, acc_ref = scratch
    tk = K_TILE
    rows = acc_ref.shape[0]
    half_rows = rows // 2
    last_mem_tile = k_ref.shape[0] // tk - 1

    def tile_start(j):
        return pl.multiple_of(jnp.clip(j, 0, last_mem_tile) * tk, tk)

    def scores(j, s_out):
        k0 = tile_start(j)
        rhs = jnp.concatenate([k_ref[pl.ds(k0, tk), :], e_ref[pl.ds(k0, tk), :]], axis=1)
        s_out[...] = _dot_nt(lhs_ref[k0 // tokens_per_half], rhs)

    def softmax(j, s_in, p_out, al_out, causal):
        s = s_in[...]
        if causal:
            kpos = j * tk + lax.broadcasted_iota(jnp.int32, (1, tk), 1)
            s = jnp.where(kpos <= t_of_rows(0, rows), s, NEG_INF)
        m_old = m_ref[...]
        m_new = jnp.maximum(m_old, jnp.max(s, axis=-1, keepdims=True))
        m_ref[...] = m_new
        al_out[...] = jnp.exp2(m_old - m_new)
        p_out[...] = jnp.exp2(s - m_new).astype(BF16)

    def values(j, p_in, al_in):
        k0 = tile_start(j)
        for rs, v_ref in ((slice(0, half_rows), va_ref), (slice(half_rows, rows), vb_ref)):
            acc_ref[rs, :] = al_in[rs, :] * acc_ref[rs, :] + _dot(p_in[rs, :], v_ref[pl.ds(k0, tk), :])

    m_ref[...] = jnp.full(m_ref.shape, NEG_INF, F32)
    acc_ref[...] = jnp.zeros(acc_ref.shape, F32)
    p_b[...] = jnp.zeros(p_b.shape, BF16)
    al_b[...] = jnp.ones(al_b.shape, F32)
    scores(0, s_a)

    def two_full_tiles(i, carry):
        j = 2 * i
        values(j - 1, p_b, al_b)
        scores(j + 1, s_b)
        softmax(j, s_a, p_a, al_a, False)
        values(j, p_a, al_a)
        scores(j + 2, s_a)
        softmax(j + 1, s_b, p_b, al_b, False)
        return carry

    n_pairs = n_full // 2
    lax.fori_loop(0, n_pairs, two_full_tiles, 0)
    j = 2 * n_pairs
    odd = (n_full & 1) == 1

    @pl.when(odd)
    def _():
        values(j - 1, p_b, al_b)
        scores(j + 1, s_b)
        softmax(j, s_a, p_a, al_a, True)
        values(j, p_a, al_a)
        softmax(j + 1, s_b, p_b, al_b, True)
        values(j + 1, p_b, al_b)

    @pl.when(jnp.logical_not(odd))
    def _():
        values(j - 1, p_b, al_b)
        softmax(j, s_a, p_a, al_a, True)
        values(j, p_a, al_a)

    acc = acc_ref[...]
    return acc / pltpu.roll(acc, HEAD_DIM, 1)


def _nsa_sel_kernel(q_ref, bias_ref, ks_ref, va_ref, vb_ref, e_ref, gate_ref, ocw_ref, sz_ref, out_ref,
                    lhs_ref, *scratch):
    tq = q_ref.shape[0]
    t0 = pl.program_id(0) * tq
    lane = lax.broadcasted_iota(jnp.int32, (tq, LANES), 1)
    q = q_ref[...]
    for half in range(lhs_ref.shape[0]):
        for h in range(NSA_HEADS):
            b = bias_ref[h // NSA_GROUP_HEADS, :, half * LANES:(half + 1) * LANES]
            lhs_ref[half, h * tq:(h + 1) * tq, :] = jnp.concatenate([_nsa_head_query(q, h, lane), b], axis=1)

    def t_of_rows(r0, n):
        return t0 + ((r0 + lax.broadcasted_iota(jnp.int32, (n, 1), 0)) & (tq - 1))

    o = _flash_body(lhs_ref, ks_ref, e_ref, va_ref, vb_ref, t_of_rows, t0 // K_TILE, NSA_SEL_LEN * LANES, scratch)
    gates = gate_ref[...]
    outs = [o[h * tq:(h + 1) * tq] * gates[:, NSA_HEADS + h:NSA_HEADS + h + 1] for h in range(NSA_HEADS)]
    for j in range(4):
        cs = slice(j * LANES, (j + 1) * LANES)
        a = jnp.where(lane < HEAD_DIM, outs[j], outs[4 + j]) + ocw_ref[:, cs]
        out_ref[:, cs] = (a * sz_ref[:, cs].astype(F32)).astype(out_ref.dtype)


def _nsa_sel(nq, bias, ks, vs, e_sel, gates, ocw, sz):
    T = nq.shape[0]
    n_selp = bias.shape[2]
    rows = NSA_HEADS * Q_TILE
    row = lambda w: pl.BlockSpec((Q_TILE, w), lambda qi: (qi, 0))
    full = pl.BlockSpec((T, LANES), lambda qi: (0, 0))
    va, vb = vs if isinstance(vs, tuple) else _values_with_ones(vs)
    return pl.pallas_call(
        _nsa_sel_kernel,
        grid=(T // Q_TILE,),
        in_specs=[row(512), pl.BlockSpec((2, Q_TILE, n_selp), lambda qi: (0, qi, 0)), full, full, full, full,
                  row(LANES), row(512), row(512)],
        out_specs=row(512),
        out_shape=jax.ShapeDtypeStruct((T, 512), BF16),
        scratch_shapes=[pltpu.VMEM((n_selp // LANES, rows, 2 * LANES), BF16)] + _flash_scratch(rows),
        compiler_params=_params("arbitrary"),
        name="nsa_selected",
    )(nq, bias, ks, va, vb, e_sel, gates, ocw, sz)


MOBA_Q_TILE = 512


def _moba_kernel(q_ref, kmean_ref, k_ref, va_ref, vb_ref, e_ref, sz_ref, out_ref, lhs_ref, *scratch, top):
    tq = q_ref.shape[0]
    t0 = pl.program_id(1) * tq
    lane = lax.broadcasted_iota(jnp.int32, (tq, LANES), 1)
    colv = q_ref[...]
    zero = jnp.zeros_like(colv)
    qp = jnp.concatenate([jnp.where(lane < HEAD_DIM, colv, zero), jnp.where(lane >= HEAD_DIM, colv, zero)], axis=0)

    def t_of_rows(r0, n):
        return t0 + ((r0 + lax.broadcasted_iota(jnp.int32, (n, 1), 0)) & (tq - 1))

    n_blk_rows = kmean_ref.shape[0]
    cur = (t0 + (lax.broadcasted_iota(jnp.int32, (1, 2 * tq), 1) & (tq - 1))) >> 8
    blk = lax.broadcasted_iota(jnp.int32, (n_blk_rows, 2 * tq), 0)
    past = blk < cur
    gate = jnp.where(past, _dot_nt(kmean_ref[...], qp), NEG_INF)
    sel = _top_k_flags(gate, blk, top)
    bias_t = jnp.where((sel & past) | (blk == cur), 0.0, MASK_BIAS)
    if n_blk_rows < LANES:
        bias_t = jnp.concatenate([bias_t, jnp.full((LANES - n_blk_rows, 2 * tq), MASK_BIAS, F32)], axis=0)
    lhs_ref[0] = jnp.concatenate([qp, bias_t.T.astype(BF16)], axis=1)
    o = _flash_body(lhs_ref, k_ref, e_ref, va_ref, vb_ref, t_of_rows, t0 // K_TILE, MOBA_BLOCK * LANES, scratch)
    a = jnp.where(lane < HEAD_DIM, o[:tq], o[tq:])
    out_ref[...] = (a * sz_ref[...].astype(F32)).astype(out_ref.dtype)


def _moba(mq, kmean, mk, mv, e_blk, sz, top):
    T = mq.shape[0]
    tq = MOBA_Q_TILE
    assert MOBA_BLOCK == 256 and T * 1 <= MOBA_BLOCK * LANES
    rows = 2 * tq
    row = pl.BlockSpec((tq, LANES), lambda p, qi: (qi, p))
    col = pl.BlockSpec((T, LANES), lambda p, qi: (0, p))
    if isinstance(mv, tuple):
        va, vb = mv
    else:
        v4 = mv.reshape(T, MOBA_HEADS // 2, 2, HEAD_DIM)
        ones = jnp.ones((T, MOBA_HEADS // 2, HEAD_DIM), mv.dtype)
        va = jnp.stack([v4[:, :, 0], ones], axis=2).reshape(T, 512)
        vb = jnp.stack([ones, v4[:, :, 1]], axis=2).reshape(T, 512)
    return pl.pallas_call(
        functools.partial(_moba_kernel, top=top),
        grid=(MOBA_HEADS // 2, T // tq),
        in_specs=[row, pl.BlockSpec((kmean.shape[0], LANES), lambda p, qi: (0, p)), col, col, col,
                  pl.BlockSpec((T, LANES), lambda p, qi: (0, 0)), row],
        out_specs=row,
        out_shape=jax.ShapeDtypeStruct((T, 512), BF16),
        scratch_shapes=[pltpu.VMEM((1, rows, 2 * LANES), BF16)] + _flash_scratch(rows),
        compiler_params=_params("arbitrary", "arbitrary"),
        name="moba",
    )(mq, kmean, mk, va, vb, e_blk, sz)


def _mem_kv_kernel(mem_ref, g_ref, w0_ref, w1_ref, kv0_ref, kv1_ref):
    mem_n = _rms(mem_ref[...], g_ref[...]).astype(BF16)
    kv0_ref[...] = _dot(mem_n, w0_ref[...]).astype(kv0_ref.dtype)
    kv1_ref[...] = _dot(mem_n, w1_ref[...]).astype(kv1_ref.dtype)


def _mem_kv(mem, g, w0, w1):
    n = mem.shape[0]
    kv_w = 2 * MEM_HEADS * HEAD_DIM
    return pl.pallas_call(
        _mem_kv_kernel,
        grid=(1,),
        in_specs=[
            pl.BlockSpec((n, D_MODEL), lambda i: (0, 0)),
            pl.BlockSpec((1, D_MODEL), lambda i: (0, 0)),
            pl.BlockSpec((D_MODEL, kv_w), lambda i: (0, 0)),
            pl.BlockSpec((D_MODEL, kv_w), lambda i: (0, 0)),
        ],
        out_specs=[pl.BlockSpec((n, kv_w), lambda i: (0, 0)), pl.BlockSpec((n, kv_w), lambda i: (0, 0))],
        out_shape=[jax.ShapeDtypeStruct((n, kv_w), BF16), jax.ShapeDtypeStruct((n, kv_w), BF16)],
        compiler_params=_params("arbitrary"),
        name="mem_kv",
    )(mem, g, w0, w1)


def _mem_attention(eq, kv, sez, lane):
    rows = eq.shape[0]
    mem_w = MEM_HEADS * HEAD_DIM
    cols = []
    for p in range(MEM_HEADS // 2):
        cs = slice(p * LANES, (p + 1) * LANES)
        colv = eq[:, cs]
        zero = jnp.zeros_like(colv)
        qp = jnp.concatenate([jnp.where(lane < HEAD_DIM, colv, zero), jnp.where(lane >= HEAD_DIM, colv, zero)], axis=0)
        s = _dot_nt(qp, kv[:, cs])
        pe = jnp.exp(s - jnp.max(s, axis=-1, keepdims=True))
        o = _dot(pe.astype(BF16), kv[:, mem_w + p * LANES:mem_w + (p + 1) * LANES]) / jnp.sum(pe, axis=-1, keepdims=True)
        cols.append(jnp.where(lane < HEAD_DIM, o[:rows], o[rows:]) * sez[:, cs].astype(F32))
    return jnp.concatenate(cols, axis=1).astype(BF16)


def _out0_proj1_kernel(ag_ref, bg_ref, eq_ref, sez_ref, kv_ref, wa_ref, wb_ref, wm_ref, x_ref,
                       g_ref, w_ref, c_ref, s_ref, b_ref, x1_ref, *proj_refs, groups):
    rows = x_ref.shape[0]
    lane = lax.broadcasted_iota(jnp.int32, (rows, LANES), 1)
    mg = _mem_attention(eq_ref[...], kv_ref[...], sez_ref[...], lane)
    y = _dot(ag_ref[...], wa_ref[...]) + _dot(bg_ref[...], wb_ref[...]) + _dot(mg, wm_ref[...])
    x1 = x_ref[...] + y
    x1_ref[...] = x1
    _proj_body(x1, g_ref, w_ref, c_ref, s_ref, b_ref, proj_refs, groups)


def _out0_proj1(ag, bg, eq, sez, kv, wa, wb, wm, x, g, w, cos_t, sin_t, bias, groups, out_dtypes):
    T = x.shape[0]
    assert OUT_ROWS == PROJ_ROWS
    rt = OUT_ROWS
    row = lambda w_: pl.BlockSpec((rt, w_), lambda i: (i, 0))
    full = lambda a: pl.BlockSpec(a.shape, lambda i: (0, 0))
    out_shapes, out_specs = _proj_out_specs(T, groups, out_dtypes)
    return pl.pallas_call(
        functools.partial(_out0_proj1_kernel, groups=tuple(groups)),
        grid=(T // rt,),
        in_specs=[row(512), row(512), row(256), row(256), full(kv), full(wa), full(wb), full(wm), row(D_MODEL)]
        + _proj_in_specs(w),
        out_specs=[row(D_MODEL)] + out_specs,
        out_shape=[jax.ShapeDtypeStruct((T, D_MODEL), F32)] + out_shapes,
        compiler_params=_params("arbitrary"),
        name="out_proj0_proj1",
    )(ag, bg, eq, sez, kv, wa, wb, wm, x, g, w, cos_t, sin_t, bias)


def _out1_kernel(rg_ref, eq_ref, sez_ref, kv_ref, wr_ref, wm_ref, x_ref, g_ref, out_ref):
    rows = x_ref.shape[0]
    lane = lax.broadcasted_iota(jnp.int32, (rows, LANES), 1)
    mg = _mem_attention(eq_ref[...], kv_ref[...], sez_ref[...], lane)
    y = _dot(rg_ref[...], wr_ref[...]) + _dot(mg, wm_ref[...])
    out_ref[...] = _rms(x_ref[...] + y, g_ref[...])


def _out1(rg, eq, sez, kv, wr, wm, x, g):
    T = x.shape[0]
    rt = OUT_ROWS
    row = lambda w: pl.BlockSpec((rt, w), lambda i: (i, 0))
    full = lambda a: pl.BlockSpec(a.shape, lambda i: (0, 0))
    return pl.pallas_call(
        _out1_kernel,
        grid=(T // rt,),
        in_specs=[row(RET_HEADS * RET_V_DIM), row(256), row(256), full(kv), full(wr), full(wm), row(D_MODEL), full(g)],
        out_specs=row(D_MODEL),
        out_shape=jax.ShapeDtypeStruct((T, D_MODEL), F32),
        compiler_params=_params("arbitrary"),
        name="out_proj1_norm",
    )(rg, eq, sez, kv, wr, wm, x, g)


def _retention_kernel(q_ref, k_ref, v_ref, sz_ref, decay_ref, qd_ref, kd_ref, cd_ref, out_ref, state_ref):
    @pl.when(pl.program_id(0) == 0)
    def _():
        state_ref[...] = jnp.zeros(state_ref.shape, F32)

    for h in range(RET_HEADS):
        qs = slice(h * RET_QK_DIM, (h + 1) * RET_QK_DIM)
        vsl = slice(h * RET_V_DIM, (h + 1) * RET_V_DIM)
        qh = q_ref[:, qs]
        kh = k_ref[:, qs]
        vh = v_ref[:, vsl]
        st = state_ref[h]
        inner = _dot_nt(qh, kh.astype(BF16)) * decay_ref[h]
        o = _dot(inner.astype(BF16), vh) + _dot(qh, st.astype(BF16)) * qd_ref[h]
        state_ref[h] = st * cd_ref[h] + _dot_tn((kh * kd_ref[h]).astype(BF16), vh)
        mu = jnp.mean(o, axis=-1, keepdims=True)
        d = o - mu
        on = d * lax.rsqrt(jnp.mean(d * d, axis=-1, keepdims=True) + NORM_EPS)
        out_ref[:, vsl] = (on * sz_ref[:, vsl].astype(F32)).astype(out_ref.dtype)


def _retention(rq, rk, rv, sz):
    T = rq.shape[0]
    C = RET_CHUNK
    log_g = jnp.log(1.0 - 2.0 ** (-5.0 - jnp.arange(RET_HEADS, dtype=F32)))
    i = jnp.arange(C, dtype=F32)
    diff = i[:, None] - i[None, :]
    decay = jnp.where(diff >= 0, jnp.exp(jnp.maximum(diff, 0.0)[None] * log_g[:, None, None]), 0.0)
    q_decay = jnp.exp((i + 1.0)[None, :] * log_g[:, None])[:, :, None]
    k_decay = jnp.exp((C - 1.0 - i)[None, :] * log_g[:, None])[:, :, None]
    chunk_decay = jnp.exp(C * log_g)[:, None, None]
    qk_w = RET_HEADS * RET_QK_DIM
    v_w = RET_HEADS * RET_V_DIM
    return pl.pallas_call(
        _retention_kernel,
        grid=(T // C,),
        in_specs=[
            pl.BlockSpec((C, qk_w), lambda c: (c, 0)),
            pl.BlockSpec((C, qk_w), lambda c: (c, 0)),
            pl.BlockSpec((C, v_w), lambda c: (c, 0)),
            pl.BlockSpec((C, v_w), lambda c: (c, 0)),
            pl.BlockSpec((RET_HEADS, C, C), lambda c: (0, 0, 0)),
            pl.BlockSpec((RET_HEADS, C, 1), lambda c: (0, 0, 0)),
            pl.BlockSpec((RET_HEADS, C, 1), lambda c: (0, 0, 0)),
            pl.BlockSpec((RET_HEADS, 1, 1), lambda c: (0, 0, 0)),
        ],
        out_specs=pl.BlockSpec((C, v_w), lambda c: (c, 0)),
        out_shape=jax.ShapeDtypeStruct((T, v_w), BF16),
        scratch_shapes=[pltpu.VMEM((RET_HEADS, RET_QK_DIM, RET_V_DIM), F32)],
        compiler_params=_params("arbitrary"),
        name="retention",
    )(rq, rk, rv, sz, decay, q_decay, k_decay, chunk_decay)


def _head_cols(order):
    return np.concatenate([np.arange(h * HEAD_DIM, (h + 1) * HEAD_DIM) for h in order])


def _one_hot_blocks(T, block):
    b = (np.arange(T) // block) % LANES
    return jnp.asarray(np.eye(LANES, dtype=np.float32)[b], dtype=BF16)


def kernel(x, mem, positions, l0_norm_g, l0_w_in, l0_nsa_gate_b, l0_cmp_pe_k, l0_cmp_w1_k, l0_cmp_w2_k,
           l0_cmp_pe_v, l0_cmp_w1_v, l0_cmp_w2_v, l0_w_mem_kv, l0_w_out,
           l1_norm_g, l1_w_in, l1_w_mem_kv, l1_w_out, mem_norm_g, final_norm_g):
    B, T, _ = x.shape
    assert B == 1 and T % K_TILE == 0 and T >= NSA_WINDOW + Q_TILE and NSA_WINDOW % Q_TILE == 0
    assert T % RET_CHUNK == 0 and T % MOBA_Q_TILE == 0 and T % OUT_ROWS == 0
    n_sel = T // NSA_SEL_LEN
    n_selp = -(-n_sel // LANES) * LANES
    k_sel = min(NSA_SEL_TOPK, n_sel)
    assert k_sel >= 3
    n_blk = T // MOBA_BLOCK
    assert n_blk <= LANES
    moba_top = max(1, min(MOBA_TOPK, n_blk - 1))
    x2 = x[0]
    pos = positions[0].astype(F32)

    attn_inv = 1.0 / (ROPE_THETA ** (jnp.arange(0, HEAD_DIM, 2, dtype=F32) / HEAD_DIM))
    ret_inv = 1.0 / (ROPE_THETA ** jnp.linspace(0.0, 1.0, RET_QK_DIM // 2, dtype=F32))
    ang = pos[:, None] * attn_inv
    cos64 = jnp.tile(jnp.cos(ang), (1, 4))
    sin64 = jnp.tile(jnp.concatenate([-jnp.sin(ang), jnp.sin(ang)], axis=1), (1, 2))
    rang = pos[:, None] * ret_inv
    cos256, sin256 = jnp.cos(rang), jnp.sin(rang)

    offs = np.cumsum([0, 512, 128, 128, 128, 128, 128, 128, 24, 512, 512, 512, 512, 512, 256, 256])
    nsa_cols = _head_cols(NSA_HEAD_ORDER)
    n_gate = offs[8] - offs[7]
    col_idx = np.concatenate([
        offs[0] + nsa_cols,
        np.arange(offs[1], offs[7]),
        np.arange(offs[7], offs[8]), np.zeros(LANES - n_gate, np.int64),
        offs[8] + nsa_cols,
        np.arange(offs[9], offs[-1]),
    ])
    keep = np.ones(col_idx.shape, np.float32)
    keep[offs[7] + n_gate:offs[7] + LANES] = 0.0
    w0 = jnp.take(l0_w_in.astype(BF16), col_idx, axis=1) * jnp.asarray(keep, BF16)[None, :]
    gate_b = jnp.concatenate([l0_nsa_gate_b, jnp.zeros((LANES - 24,), F32)])[None, :]
    groups0 = [
        (512, "rot64", LOG2_SCALE), (128, "rot64", 1.0), (128, "plain", 1.0), (128, "rot64", 1.0),
        (128, "ones_pair", 1.0), (128, "rot64", 1.0), (128, "ones_pair", 1.0), (128, "gate", 1.0), (512, "silu", 1.0),
        (512, "rot64", LOG2_SCALE), (512, "rot64_mean", 1.0), (512, "ones_pair", 1.0), (512, "silu", 1.0),
        (256, "plain", 0.125), (256, "silu", 1.0),
    ]
    dts0 = [BF16] * 7 + [F32] + [BF16] * 7
    (nq, nkc, nvc, nks, nvs_a, nvs_b, nkw, nvw_a, nvw_b, gates, snz, mq, mk, kmean, mv_a, mv_b, smz, eq0, sez0) = _proj(
        x2, l0_norm_g[None, :], w0, cos64, sin64, gate_b, groups0, dts0, "proj0")

    kv0, kv1 = _mem_kv(mem[0], mem_norm_g[None, :], l0_w_mem_kv.astype(BF16), l1_w_mem_kv.astype(BF16))

    kc = _compress(nkc, l0_cmp_pe_k, l0_cmp_w1_k, l0_cmp_w2_k, "compress_k")
    vc = _compress(nvc, l0_cmp_pe_v, l0_cmp_w1_v, l0_cmp_w2_v, "compress_v")
    n_cmp_rows = T // NSA_CMP_STRIDE
    cmp_start = np.arange(n_cmp_rows)[:, None] * NSA_CMP_STRIDE
    sel_start = np.arange(n_selp)[None, :] * NSA_SEL_LEN
    overlap = ((cmp_start < sel_start + NSA_SEL_LEN) & (cmp_start + NSA_CMP_LEN > sel_start)
               & (np.arange(n_selp)[None, :] < n_sel))
    ov = jnp.asarray(overlap.T.astype(np.float32), dtype=BF16)
    ocw, sel_bias = _nsa_cmp(nq, kc, vc, ov, nkw, (nvw_a, nvw_b), gates, k_sel)
    ag = _nsa_sel(nq, sel_bias, nks, (nvs_a, nvs_b), _one_hot_blocks(T, NSA_SEL_LEN), gates, ocw, snz)

    n_blk_rows = -(-n_blk // 16) * 16
    kmean_p = jnp.concatenate([kmean[:, 0, :], jnp.zeros((n_blk_rows - n_blk, 512), F32)], axis=0).astype(BF16)
    bg = _moba(mq, kmean_p, mk, (mv_a, mv_b), _one_hot_blocks(T, MOBA_BLOCK), smz, moba_top)

    w_out0 = l0_w_out.astype(BF16)
    wa = w_out0[:512][nsa_cols]
    wb = w_out0[512:1024]
    wm0 = w_out0[1024:]
    groups1 = [
        (1024, "rot256", 1.0), (1024, "rot256", RET_QK_DIM ** -0.5), (2048, "plain", 1.0), (2048, "silu", 1.0),
        (256, "plain", 0.125), (256, "silu", 1.0),
    ]
    dts1 = [BF16, F32, BF16, BF16, BF16, BF16]
    x1, rq, rk, rv, srz, eq1, sez1 = _out0_proj1(
        ag, bg, eq0, sez0, kv0, wa, wb, wm0, x2, l1_norm_g[None, :], l1_w_in.astype(BF16), cos256, sin256,
        jnp.zeros((1, LANES), F32), groups1, dts1)
    rg = _retention(rq, rk, rv, srz)
    v_w = RET_HEADS * RET_V_DIM
    out = _out1(rg, eq1, sez1, kv1, l1_w_out[:v_w].astype(BF16), l1_w_out[v_w:].astype(BF16), x1,
                final_norm_g[None, :])
    return out[None]
```
